```python
import math
import jax, jax.numpy as jnp
from jax import lax
import numpy as np

D_MODEL = 1024
BATCH = 4
SEQ = 8192
DEPTH = 2
DEC_BATCH = 8
DEC_SEQ = 4096
PAST_LEN = 128

EPS = 1e-6
NEG_INF = -1e30
RET_HEADS = 4
RET_DK = 64
RET_DV = 128
RET_CHUNK = 128
DIL_HEADS = 8
DIL_DH = 64
DIL_PATTERNS = ((128, 1), (512, 4), (2048, 16))
DIL_BLOCK = 128
MLA_HEADS = 8
MLA_Q_RANK = 256
MLA_KV_RANK = 128
MLA_NOPE = 64
MLA_ROPE = 32
MLA_V = 64
MLA_QBLOCK = 128
ROPE_BASE = 10000.0
S5_GROUPS = 32
S5_GROUP_CH = 16
S5_STATE = 64
S5_CH = S5_GROUPS * S5_GROUP_CH
EVEN_IN = 2 * RET_HEADS * RET_DK + 2 * RET_HEADS * RET_DV + 3 * DIL_HEADS * DIL_DH
EVEN_MIX = RET_HEADS * RET_DV + DIL_HEADS * DIL_DH
ODD_IN = MLA_Q_RANK + MLA_KV_RANK + MLA_ROPE + S5_CH
ODD_MIX = MLA_HEADS * MLA_V + S5_CH
FFN_DENSE = 2816
N_EXPERTS = 8
TOP_K = 2
FFN_EXPERT = 3584

kernel_name = 'hybrid_bidir_retention_dilated_mla_s5_encoder'


def _rms_norm(x, g):
    xf = x.astype(jnp.float32)
    y = xf * lax.rsqrt(jnp.mean(xf * xf, axis=-1, keepdims=True) + EPS)
    return (y * g.astype(jnp.float32)).astype(x.dtype)


def _split(t, sizes):
    cuts = [int(c) for c in np.cumsum(sizes)[:-1]]
    return jnp.split(t, cuts, axis=-1)


def _swiglu(h, w_gate, w_up, w_down):
    return (jax.nn.silu(h @ w_gate) * (h @ w_up)) @ w_down


def _retention_scan(q, k, v, log_gamma, include_diag):
    b, h, s, dk = q.shape
    dv = v.shape[-1]
    c = RET_CHUNK
    n = s // c
    qc = q.reshape(b, h, n, c, dk)
    kc = k.reshape(b, h, n, c, dk)
    vc = v.reshape(b, h, n, c, dv)
    j = jnp.arange(c, dtype=jnp.float32)
    diff = j[:, None] - j[None, :]
    mask = (diff >= 0) if include_diag else (diff > 0)
    decay = jnp.where(mask, jnp.exp(log_gamma[:, None, None] * jnp.maximum(diff, 0.0)), 0.0)
    scores = jnp.einsum('bhnqd,bhnkd->bhnqk', qc, kc) * decay[None, :, None]
    intra = jnp.einsum('bhnqk,bhnkv->bhnqv', scores, vc)
    k_w = jnp.exp(log_gamma[:, None] * (c - 1.0 - j)[None, :])
    chunk_kv = jnp.einsum('bhncd,bhncv,hc->nbhdv', kc, vc, k_w)
    chunk_decay = jnp.exp(log_gamma * c)[None, :, None, None]

    def step(state, kv):
        return state * chunk_decay + kv, state

    init = jnp.zeros((b, h, dk, dv), jnp.float32)
    _, prev = lax.scan(step, init, chunk_kv)
    q_w = jnp.exp(log_gamma[:, None] * (j + 1.0)[None, :])
    cross = jnp.einsum('bhncd,nbhdv,hc->bhncv', qc, prev, q_w)
    return (intra + cross).reshape(b, h, s, dv)


def _retention(q, k, v, g, decay_f, decay_b, gn):
    b, s, _ = q.shape

    def heads(t, d):
        return t.astype(jnp.float32).reshape(b, s, RET_HEADS, d).transpose(0, 2, 1, 3)

    qh = heads(q, RET_DK) * (RET_DK ** -0.5)
    kh = heads(k, RET_DK)
    vh = heads(v, RET_DV)
    lg_f = jax.nn.log_sigmoid(decay_f.astype(jnp.float32))
    lg_b = jax.nn.log_sigmoid(decay_b.astype(jnp.float32))
    fwd = _retention_scan(qh, kh, vh, lg_f, True)
    rev = lambda t: jnp.flip(t, axis=2)
    bwd = rev(_retention_scan(rev(qh), rev(kh), rev(vh), lg_b, False))
    y = (fwd + bwd).transpose(0, 2, 1, 3)
    mu = jnp.mean(y, axis=-1, keepdims=True)
    var = jnp.mean(jnp.square(y - mu), axis=-1, keepdims=True)
    y = ((y - mu) * lax.rsqrt(var + EPS)).reshape(b, s, RET_HEADS * RET_DV) * gn.astype(jnp.float32)
    return jax.nn.silu(g.astype(jnp.float32)) * y


def _alibi_slopes(n):
    return jnp.exp2(-8.0 * jnp.arange(1, n + 1, dtype=jnp.float32) / n)


def _banded_attention(q, k, v, slopes, half):
    bp, h, l, dh = q.shape
    nb = -(-l // DIL_BLOCK)
    lp = nb * DIL_BLOCK
    kw = DIL_BLOCK + 2 * half
    qp = jnp.pad(q, ((0, 0), (0, 0), (0, lp - l), (0, 0)))
    kp = jnp.pad(k, ((0, 0), (0, 0), (half, lp - l + half), (0, 0)))
    vp = jnp.pad(v, ((0, 0), (0, 0), (half, lp - l + half), (0, 0)))
    qpos = jnp.arange(nb)[:, None] * DIL_BLOCK + jnp.arange(DIL_BLOCK)[None, :]
    kidx = jnp.arange(nb)[:, None] * DIL_BLOCK + jnp.arange(kw)[None, :]
    kpos = kidx - half
    kb = kp[:, :, kidx]
    vb = vp[:, :, kidx]
    qb = qp.reshape(bp, h, nb, DIL_BLOCK, dh)
    s = jnp.einsum('bhnqd,bhnkd->bhnqk', qb, kb).astype(jnp.float32) * (dh ** -0.5)
    dist = jnp.abs(qpos[:, :, None] - kpos[:, None, :])
    valid = (dist <= half) & ((kpos >= 0) & (kpos < l))[:, None, :]
    s = s - slopes[:, None, None, None] * dist.astype(jnp.float32)
    s = jnp.where(valid, s, NEG_INF)
    m = jnp.max(s, axis=-1, keepdims=True)
    p = jnp.exp(s - m)
    den = jnp.sum(p, axis=-1)
    o = jnp.einsum('bhnqk,bhnkd->bhnqd', p, vb.astype(jnp.float32)) / den[..., None]
    lse = m[..., 0] + jnp.log(den)
    return o.reshape(bp, h, lp, dh)[:, :, :l], lse.reshape(bp, h, lp)[:, :, :l]


def _dilated_attention(q, k, v):
    b, s, h, dh = q.shape
    slopes = _alibi_slopes(h)
    outs = []
    lses = []
    for window, dil in DIL_PATTERNS:
        half = window // (2 * dil)
        l = s // dil

        def to_sub(t):
            return t.reshape(b, l, dil, h, dh).transpose(0, 2, 3, 1, 4).reshape(b * dil, h, l, dh)

        o, lse = _banded_attention(to_sub(q), to_sub(k), to_sub(v), slopes * dil, half)
        outs.append(o.reshape(b, dil, h, l, dh).transpose(0, 3, 1, 2, 4).reshape(b, s, h, dh))
        lses.append(lse.reshape(b, dil, h, l).transpose(0, 3, 1, 2).reshape(b, s, h))
    w = jax.nn.softmax(jnp.stack(lses, axis=0), axis=0)
    return jnp.einsum('gbsh,gbshd->bshd', w, jnp.stack(outs, axis=0))


def _rope_tables(s, dim):
    inv = ROPE_BASE ** (-jnp.arange(0, dim, 2, dtype=jnp.float32) / dim)
    ang = jnp.arange(s, dtype=jnp.float32)[:, None] * inv[None, :]
    return jnp.cos(ang), jnp.sin(ang)


def _apply_rope(x, cos, sin):
    x = x.astype(jnp.float32)
    half = x.shape[-1] // 2
    x1, x2 = x[..., :half], x[..., half:]
    return jnp.concatenate([x1 * cos - x2 * sin, x1 * sin + x2 * cos], axis=-1)


def _mla(cq, ckv, k_pe, q_norm, w_uq, kv_norm, w_ukv):
    b, s, _ = cq.shape
    q = (_rms_norm(cq, q_norm) @ w_uq).astype(jnp.float32).reshape(b, s, MLA_HEADS, MLA_NOPE + MLA_ROPE)
    kv = (_rms_norm(ckv, kv_norm) @ w_ukv).astype(jnp.float32).reshape(b, s, MLA_HEADS, MLA_NOPE + MLA_V)
    q_nope, q_pe = q[..., :MLA_NOPE], q[..., MLA_NOPE:]
    k_nope, v = kv[..., :MLA_NOPE], kv[..., MLA_NOPE:]
    cos, sin = _rope_tables(s, MLA_ROPE)
    q_pe = _apply_rope(q_pe, cos[:, None, :], sin[:, None, :])
    k_pe = _apply_rope(k_pe, cos, sin)
    scale = (MLA_NOPE + MLA_ROPE) ** -0.5
    nb = s // MLA_QBLOCK
    qn_blocks = q_nope.reshape(b, nb, MLA_QBLOCK, MLA_HEADS, MLA_NOPE).transpose(1, 0, 2, 3, 4)
    qp_blocks = q_pe.reshape(b, nb, MLA_QBLOCK, MLA_HEADS, MLA_ROPE).transpose(1, 0, 2, 3, 4)

    def block(args):
        qn, qp = args
        sc = (jnp.einsum('bqhd,bkhd->bhqk', qn, k_nope) + jnp.einsum('bqhr,bkr->bhqk', qp, k_pe)) * scale
        p = jax.nn.softmax(sc, axis=-1)
        return jnp.einsum('bhqk,bkhv->bqhv', p, v)

    o = lax.map(block, (qn_blocks, qp_blocks))
    return o.transpose(1, 0, 2, 3, 4).reshape(b, s, MLA_HEADS * MLA_V)


def _ssm_combine(e1, e2):
    a1, b1 = e1
    a2, b2 = e2
    return a2 * a1, a2 * b1 + b2


def _s5(u, lam_re_f, lam_im_f, log_step_f, lam_re_b, lam_im_b, log_step_b,
        b_re, b_im, c_re, c_im, d_skip, glu_w, glu_b):
    b, s, _ = u.shape
    uf = u.astype(jnp.float32).reshape(b, s, S5_GROUPS, S5_GROUP_CH)
    uc = uf.astype(jnp.complex64)
    bmat = lax.complex(b_re.astype(jnp.float32), b_im.astype(jnp.float32))
    cmat = lax.complex(c_re.astype(jnp.float32), c_im.astype(jnp.float32))

    def direction(lam_re, lam_im, log_step, reverse):
        lam = lax.complex(jnp.minimum(lam_re.astype(jnp.float32), -1e-4), lam_im.astype(jnp.float32))
        step = jnp.exp(log_step.astype(jnp.float32))[:, None]
        lam_bar = jnp.exp(lam * step)
        b_bar = ((lam_bar - 1.0) / lam)[:, :, None] * bmat
        bu = jnp.einsum('bsgc,gpc->bsgp', uc, b_bar)
        a = jnp.broadcast_to(lam_bar, bu.shape)
        _, xs = lax.associative_scan(_ssm_combine, (a, bu), axis=1, reverse=reverse)
        return jnp.einsum('bsgp,gcp->bsgc', xs, cmat).real

    y = (direction(lam_re_f, lam_im_f, log_step_f, False)
         + direction(lam_re_b, lam_im_b, log_step_b, True)
         + d_skip.astype(jnp.float32).reshape(S5_GROUPS, S5_GROUP_CH) * uf)
    z = jax.nn.gelu(y.reshape(b, s, S5_CH))
    return z * jax.nn.sigmoid(z @ glu_w + glu_b)


def _moe(h, router, e_gate, e_up, e_down):
    b, s, d = h.shape
    t = h.reshape(b * s, d)
    logits = (t @ router).astype(jnp.float32)
    top_val, top_idx = lax.top_k(logits, TOP_K)
    w = jax.nn.softmax(top_val, axis=-1)
    gates = jnp.sum(jax.nn.one_hot(top_idx, N_EXPERTS, dtype=jnp.float32) * w[..., None], axis=1)
    out = jnp.zeros((b * s, d), jnp.float32)
    for e in range(N_EXPERTS):
        out = out + gates[:, e:e + 1] * _swiglu(t, e_gate[e], e_up[e], e_down[e])
    return out.reshape(b, s, d).astype(h.dtype)


def _even_layer(x, attn_norm, w_in, ret_decay_f, ret_decay_b, ret_gn, w_out,
                ffn_norm, ffn_w_gate, ffn_w_up, ffn_w_down):
    b, s, _ = x.shape
    proj = _rms_norm(x, attn_norm) @ w_in
    rq, rk, rv, rg, dq, dk, dv = _split(proj, [RET_HEADS * RET_DK, RET_HEADS * RET_DK,
                                               RET_HEADS * RET_DV, RET_HEADS * RET_DV,
                                               DIL_HEADS * DIL_DH, DIL_HEADS * DIL_DH, DIL_HEADS * DIL_DH])
    ret = _retention(rq, rk, rv, rg, ret_decay_f, ret_decay_b, ret_gn)
    shp = (b, s, DIL_HEADS, DIL_DH)
    dil = _dilated_attention(dq.reshape(shp), dk.reshape(shp), dv.reshape(shp)).reshape(b, s, DIL_HEADS * DIL_DH)
    x = x + jnp.concatenate([ret, dil], axis=-1).astype(x.dtype) @ w_out
    return x + _swiglu(_rms_norm(x, ffn_norm), ffn_w_gate, ffn_w_up, ffn_w_down)


def _odd_layer(x, attn_norm, w_in, mla_q_norm, mla_w_uq, mla_kv_norm, mla_w_ukv,
               lam_re_f, lam_im_f, log_step_f, lam_re_b, lam_im_b, log_step_b,
               s5_b_re, s5_b_im, s5_c_re, s5_c_im, s5_d, s5_glu_w, s5_glu_b, w_out,
               ffn_norm, router, exp_w_gate, exp_w_up, exp_w_down):
    proj = _rms_norm(x, attn_norm) @ w_in
    cq, ckv, kpe, u = _split(proj, [MLA_Q_RANK, MLA_KV_RANK, MLA_ROPE, S5_CH])
    mla = _mla(cq, ckv, kpe, mla_q_norm, mla_w_uq, mla_kv_norm, mla_w_ukv)
    ssm = _s5(u, lam_re_f, lam_im_f, log_step_f, lam_re_b, lam_im_b, log_step_b,
              s5_b_re, s5_b_im, s5_c_re, s5_c_im, s5_d, s5_glu_w, s5_glu_b)
    x = x + jnp.concatenate([mla, ssm], axis=-1).astype(x.dtype) @ w_out
    return x + _moe(_rms_norm(x, ffn_norm), router, exp_w_gate, exp_w_up, exp_w_down)


def _trunk(x, even_params, odd_params, final_norm):
    layer_params = (even_params, odd_params)
    for layer in range(DEPTH):
        if layer % 2 == 0:
            x = _even_layer(x, *layer_params[layer])
        else:
            x = _odd_layer(x, *layer_params[layer])
    return _rms_norm(x, final_norm)


def setup_inputs(seed: int = 0) -> dict:
    key = jax.random.key(seed)
    ks = iter(jax.random.split(key, 64))
    f32 = jnp.float32

    def normal(shape, scale):
        return scale * jax.random.normal(next(ks), shape, f32)

    def gain(n):
        return 1.0 + 0.02 * jax.random.normal(next(ks), (n,), f32)

    hh = jnp.arange(RET_HEADS, dtype=f32)
    gamma = 1.0 - jnp.exp2(-5.0 - hh)
    decay_logit = jnp.log(gamma) - jnp.log1p(-gamma)
    lam_im0 = jnp.broadcast_to(math.pi * jnp.arange(S5_STATE, dtype=f32), (S5_GROUPS, S5_STATE))
    lo, hi = math.log(1e-3), math.log(1e-1)
    return {
        'x_prompt': jax.random.normal(next(ks), (BATCH, SEQ, D_MODEL), f32),
        'x_sample': jax.random.normal(next(ks), (DEC_BATCH, DEC_SEQ, D_MODEL), f32),
        'l0_attn_norm': gain(D_MODEL),
        'l0_w_in': normal((D_MODEL, EVEN_IN), D_MODEL ** -0.5),
        'l0_ret_decay_f': decay_logit + normal((RET_HEADS,), 0.1),
        'l0_ret_decay_b': decay_logit + normal((RET_HEADS,), 0.1),
        'l0_ret_gn': gain(RET_HEADS * RET_DV),
        'l0_w_out': normal((EVEN_MIX, D_MODEL), EVEN_MIX ** -0.5),
        'l0_ffn_norm': gain(D_MODEL),
        'l0_ffn_w_gate': normal((D_MODEL, FFN_DENSE), D_MODEL ** -0.5),
        'l0_ffn_w_up': normal((D_MODEL, FFN_DENSE), D_MODEL ** -0.5),
        'l0_ffn_w_down': normal((FFN_DENSE, D_MODEL), FFN_DENSE ** -0.5),
        'l1_attn_norm': gain(D_MODEL),
        'l1_w_in': normal((D_MODEL, ODD_IN), D_MODEL ** -0.5),
        'l1_mla_q_norm': gain(MLA_Q_RANK),
        'l1_mla_w_uq': normal((MLA_Q_RANK, MLA_HEADS * (MLA_NOPE + MLA_ROPE)), MLA_Q_RANK ** -0.5),
        'l1_mla_kv_norm': gain(MLA_KV_RANK),
        'l1_mla_w_ukv': normal((MLA_KV_RANK, MLA_HEADS * (MLA_NOPE + MLA_V)), MLA_KV_RANK ** -0.5),
        'l1_s5_lam_re_f': -0.5 + normal((S5_GROUPS, S5_STATE), 0.01),
        'l1_s5_lam_im_f': lam_im0 + normal((S5_GROUPS, S5_STATE), 0.01),
        'l1_s5_log_step_f': jax.random.uniform(next(ks), (S5_GROUPS,), f32, lo, hi),
        'l1_s5_lam_re_b': -0.5 + normal((S5_GROUPS, S5_STATE), 0.01),
        'l1_s5_lam_im_b': lam_im0 + normal((S5_GROUPS, S5_STATE), 0.01),
        'l1_s5_log_step_b': jax.random.uniform(next(ks), (S5_GROUPS,), f32, lo, hi),
        'l1_s5_b_re': normal((S5_GROUPS, S5_STATE, S5_GROUP_CH), (2 * S5_GROUP_CH) ** -0.5),
        'l1_s5_b_im': normal((S5_GROUPS, S5_STATE, S5_GROUP_CH), (2 * S5_GROUP_CH) ** -0.5),
        'l1_s5_c_re': normal((S5_GROUPS, S5_GROUP_CH, S5_STATE), (2 * S5_STATE) ** -0.5),
        'l1_s5_c_im': normal((S5_GROUPS, S5_GROUP_CH, S5_STATE), (2 * S5_STATE) ** -0.5),
        'l1_s5_d': normal((S5_CH,), 1.0),
        'l1_s5_glu_w': normal((S5_CH, S5_CH), S5_CH ** -0.5),
        'l1_s5_glu_b': normal((S5_CH,), 0.01),
        'l1_w_out': normal((ODD_MIX, D_MODEL), ODD_MIX ** -0.5),
        'l1_ffn_norm': gain(D_MODEL),
        'l1_router': normal((D_MODEL, N_EXPERTS), D_MODEL ** -0.5),
        'l1_exp_w_gate': normal((N_EXPERTS, D_MODEL, FFN_EXPERT), D_MODEL ** -0.5),
        'l1_exp_w_up': normal((N_EXPERTS, D_MODEL, FFN_EXPERT), D_MODEL ** -0.5),
        'l1_exp_w_down': normal((N_EXPERTS, FFN_EXPERT, D_MODEL), FFN_EXPERT ** -0.5),
        'final_norm': gain(D_MODEL),
    }


def reference(x_prompt, x_sample,
              l0_attn_norm, l0_w_in, l0_ret_decay_f, l0_ret_decay_b, l0_ret_gn, l0_w_out,
              l0_ffn_norm, l0_ffn_w_gate, l0_ffn_w_up, l0_ffn_w_down,
              l1_attn_norm, l1_w_in, l1_mla_q_norm, l1_mla_w_uq, l1_mla_kv_norm, l1_mla_w_ukv,
              l1_s5_lam_re_f, l1_s5_lam_im_f, l1_s5_log_step_f,
              l1_s5_lam_re_b, l1_s5_lam_im_b, l1_s5_log_step_b,
              l1_s5_b_re, l1_s5_b_im, l1_s5_c_re, l1_s5_c_im, l1_s5_d, l1_s5_glu_w, l1_s5_glu_b,
              l1_w_out, l1_ffn_norm, l1_router, l1_exp_w_gate, l1_exp_w_up, l1_exp_w_down,
              final_norm):
    even_params = (l0_attn_norm, l0_w_in, l0_ret_decay_f, l0_ret_decay_b, l0_ret_gn, l0_w_out,
                   l0_ffn_norm, l0_ffn_w_gate, l0_ffn_w_up, l0_ffn_w_down)
    odd_params = (l1_attn_norm, l1_w_in, l1_mla_q_norm, l1_mla_w_uq, l1_mla_kv_norm, l1_mla_w_ukv,
                  l1_s5_lam_re_f, l1_s5_lam_im_f, l1_s5_log_step_f,
                  l1_s5_lam_re_b, l1_s5_lam_im_b, l1_s5_log_step_b,
                  l1_s5_b_re, l1_s5_b_im, l1_s5_c_re, l1_s5_c_im, l1_s5_d, l1_s5_glu_w, l1_s5_glu_b,
                  l1_w_out, l1_ffn_norm, l1_router, l1_exp_w_gate, l1_exp_w_up, l1_exp_w_down)
    y_prompt = _trunk(x_prompt, even_params, odd_params, final_norm)
    y_sample = _trunk(x_sample, even_params, odd_params, final_norm)
    return (y_prompt, y_sample)
```

```python
import functools
import math

import jax
import jax.numpy as jnp
from jax import lax
from jax.experimental import pallas as pl
from jax.experimental.pallas import tpu as pltpu

F32 = jnp.float32
BF16 = jnp.bfloat16

D_MODEL = 1024
EPS = 1e-6
NEG_INF = -1e30
RET_HEADS = 4
RET_DK = 64
RET_DV = 128
RET_CHUNK = 128
DIL_HEADS = 8
DIL_DH = 64
DIL_PATTERNS = ((128, 1), (512, 4), (2048, 16))
DIL_QBLOCK = 128
MLA_HEADS = 8
MLA_Q_RANK = 256
MLA_KV_RANK = 128
MLA_NOPE = 64
MLA_ROPE = 32
MLA_V = 64
ROPE_BASE = 10000.0
S5_GROUPS = 32
S5_GROUP_CH = 16
S5_STATE = 64
S5_CH = S5_GROUPS * S5_GROUP_CH
S5_NSTATE = S5_GROUPS * S5_STATE
N_EXPERTS = 8
TOP_K = 2

LANES = 128
SUBLANES = 8
VMEM_LIMIT = 56 * 1024 * 1024

L0_COLS = 3584
L0_BLOCKS = L0_COLS // LANES
L0_RQ, L0_RK, L0_RV, L0_RG, L0_DQ, L0_DK, L0_DV = 0, 4, 8, 12, 16, 20, 24


def _cparams(sem):
    return pltpu.CompilerParams(dimension_semantics=sem, vmem_limit_bytes=VMEM_LIMIT)


def _rms(x, g):
    return x * lax.rsqrt(jnp.mean(x * x, axis=-1, keepdims=True) + EPS) * g


def _norm_matmul_kernel(x_ref, g_ref, w_ref, o_ref, xn_ref):
    @pl.when(pl.program_id(1) == 0)
    def _():
        xn_ref[...] = _rms(x_ref[...], g_ref[...]).astype(BF16)

    o_ref[...] = jnp.dot(xn_ref[...], w_ref[...], preferred_element_type=F32).astype(o_ref.dtype)


def _norm_matmul(x2d, gain, w, tm=512, tn=512):
    t, d = x2d.shape
    n = w.shape[1]
    return pl.pallas_call(
        _norm_matmul_kernel,
        grid=(t // tm, n // tn),
        in_specs=[
            pl.BlockSpec((tm, d), lambda i, j: (i, 0)),
            pl.BlockSpec((1, d), lambda i, j: (0, 0)),
            pl.BlockSpec((d, tn), lambda i, j: (0, j)),
        ],
        out_specs=pl.BlockSpec((tm, tn), lambda i, j: (i, j)),
        out_shape=jax.ShapeDtypeStruct((t, n), BF16),
        scratch_shapes=[pltpu.VMEM((tm, d), BF16)],
        compiler_params=_cparams(("parallel", "arbitrary")),
        name="l0_norm_in_proj",
    )(x2d, gain.reshape(1, d), w)


def _retention_kernel(q_ref, k_ref, v_ref, g_ref, d_ref, qw_ref, kw_ref, cd_ref, gn_ref,
                      o_ref, kv_ref, p_ref, *, n_chunks):
    c = RET_CHUNK
    kw = kw_ref[0]
    qw = qw_ref[0]
    dmat = d_ref[0]
    gn = gn_ref[...]

    def kv_body(n, carry):
        sl = pl.ds(pl.multiple_of(n * c, c), c)
        kc = (k_ref[0, sl, :].astype(F32) * kw).T.astype(BF16)
        kv_ref[n] = jnp.dot(kc, v_ref[0, sl, :], preferred_element_type=F32)
        return carry

    lax.fori_loop(0, n_chunks, kv_body, 0)

    half = c // 2
    dec_f = cd_ref[0, 0:half, :]
    dec_b = cd_ref[0, half:c, :]

    def fwd_body(n, s):
        p_ref[n, 0:half, :] = s.astype(BF16)
        return s * dec_f + kv_ref[n, 0:half, :]

    lax.fori_loop(0, n_chunks, fwd_body, jnp.zeros((half, RET_DV), F32))

    def bwd_body(i, s):
        n = n_chunks - 1 - i
        p_ref[n, half:c, :] = s.astype(BF16)
        return s * dec_b + kv_ref[n, half:c, :]

    lax.fori_loop(0, n_chunks, bwd_body, jnp.zeros((half, RET_DV), F32))

    def out_body(n, carry):
        sl = pl.ds(pl.multiple_of(n * c, c), c)
        qc = q_ref[0, sl, :]
        s = lax.dot_general(qc, k_ref[0, sl, :], (((1,), (1,)), ((), ())), preferred_element_type=F32)
        intra = jnp.dot((s * dmat).astype(BF16), v_ref[0, sl, :], preferred_element_type=F32)
        qq = (qc.astype(F32) * qw).astype(BF16)
        y = intra + jnp.dot(qq, p_ref[n], preferred_element_type=F32)
        mu = jnp.mean(y, axis=-1, keepdims=True)
        yc = y - mu
        var = jnp.mean(yc * yc, axis=-1, keepdims=True)
        yn = yc * lax.rsqrt(var + EPS) * gn
        gg = g_ref[0, sl, :].astype(F32)
        o_ref[0, sl, :] = (gg * jax.nn.sigmoid(gg) * yn).astype(o_ref.dtype)
        return carry

    lax.fori_loop(0, n_chunks, out_body, 0)


def _retention_tables(decay_f, decay_b):
    c = RET_CHUNK
    lg_f = jax.nn.log_sigmoid(decay_f.astype(F32))[:, None, None]
    lg_b = jax.nn.log_sigmoid(decay_b.astype(F32))[:, None, None]
    j = jnp.arange(c, dtype=F32)
    diff = j[:, None] - j[None, :]
    dmat = 0.5 * jnp.where(diff >= 0, jnp.exp(lg_f * jnp.maximum(diff, 0.0)),
                           jnp.exp(lg_b * jnp.maximum(-diff, 0.0)))
    lane_f = (jnp.arange(LANES) < RET_DK)[None, None, :]
    jj = j[None, :, None]
    qw = jnp.where(lane_f, jnp.exp(lg_f * (jj + 1.0)), jnp.exp(lg_b * (c - jj)))
    kw = jnp.where(lane_f, jnp.exp(lg_f * (c - 1.0 - jj)), jnp.exp(lg_b * jj))
    row_f = (jnp.arange(c) < c // 2)[None, :, None]
    cd = jnp.where(row_f, jnp.exp(lg_f * c), jnp.exp(lg_b * c)) * jnp.ones((1, 1, RET_DV), F32)
    return dmat.astype(F32), qw.astype(F32), kw.astype(F32), cd.astype(F32)


def _retention(p0, tables, gn):
    b, s, _ = p0.shape
    dmat, qw, kw, cd = tables
    n_chunks = s // RET_CHUNK
    seq = lambda col: pl.BlockSpec((1, s, LANES), lambda i, h: (i, 0, col + h))
    tab = pl.BlockSpec((1, RET_CHUNK, LANES), lambda i, h: (h, 0, 0))
    return pl.pallas_call(
        functools.partial(_retention_kernel, n_chunks=n_chunks),
        grid=(b, RET_HEADS),
        in_specs=[seq(L0_RQ), seq(L0_RK), seq(L0_RV), seq(L0_RG), tab, tab, tab, tab,
                  pl.BlockSpec((1, LANES), lambda i, h: (0, h))],
        out_specs=pl.BlockSpec((1, s, LANES), lambda i, h: (i, 0, h)),
        out_shape=jax.ShapeDtypeStruct((b, s, RET_HEADS * RET_DV), BF16),
        scratch_shapes=[pltpu.VMEM((n_chunks, RET_CHUNK, RET_DV), F32),
                        pltpu.VMEM((n_chunks, RET_CHUNK, RET_DV), BF16)],
        compiler_params=_cparams(("parallel", "parallel")),
        name="l0_retention",
    )(p0, p0, p0, p0, dmat, qw, kw, cd, gn.reshape(1, -1))


def _dilated_kernel(q_ref, k_ref, v_ref, slope_ref, o_ref, lse_ref, *, seq_len, half, q_rows):
    tq = DIL_QBLOCK
    kw = tq + 2 * half
    base = pl.program_id(3) * q_rows
    lane = lax.broadcasted_iota(jnp.int32, (tq, LANES), 1)
    lo = lane < DIL_DH
    rq = lax.broadcasted_iota(jnp.int32, (tq, kw), 0)
    rk = lax.broadcasted_iota(jnp.int32, (tq, kw), 1)

    def blk(i, carry):
        off = pl.multiple_of(i * tq, tq)
        qs = base + off
        ks = pl.multiple_of(jnp.clip(qs - half, 0, seq_len - kw), half)
        q = q_ref[0, pl.ds(off, tq), :]
        kwin = k_ref[0, pl.ds(ks, kw), :]
        vwin = v_ref[0, pl.ds(ks, kw), :]
        dist = jnp.abs((qs + rq) - (ks + rk))
        valid = dist <= half
        distf = dist.astype(F32)
        outs, lses = [], []
        for hh in range(2):
            qh = jnp.where(lo if hh == 0 else jnp.logical_not(lo), q, jnp.zeros_like(q))
            s = lax.dot_general(qh, kwin, (((1,), (1,)), ((), ())), preferred_element_type=F32)
            s = s - slope_ref[0, hh:hh + 1, 0:1] * distf
            s = jnp.where(valid, s, NEG_INF)
            m = jnp.max(s, axis=-1, keepdims=True)
            p = jnp.exp(s - m)
            den = jnp.sum(p, axis=-1, keepdims=True)
            outs.append(jnp.dot(p.astype(BF16), vwin, preferred_element_type=F32) / den)
            lses.append(m + jnp.log(den))
        o_ref[0, pl.ds(off, tq), :] = jnp.where(lo, outs[0], outs[1])
        lse_ref[0, pl.ds(off, tq), :] = jnp.where(lo, lses[0], lses[1])
        return carry

    lax.fori_loop(0, q_rows // tq, blk, 0)


def _dilated_pattern(p0, window, dil):
    b, s, _ = p0.shape
    half = window // (2 * dil)
    seq_len = s // dil
    q_rows = min(seq_len, 1024)
    pv = p0.reshape(b, seq_len, dil * L0_COLS)
    hpairs = DIL_HEADS // 2
    slopes = jnp.exp2(-8.0 * jnp.arange(1, DIL_HEADS + 1, dtype=F32) / DIL_HEADS) * dil
    slope_tab = jnp.zeros((hpairs, SUBLANES, LANES), F32).at[:, 0:2, :].set(
        jnp.broadcast_to(slopes.reshape(hpairs, 2, 1), (hpairs, 2, LANES)))
    kernel = functools.partial(_dilated_kernel, seq_len=seq_len, half=half, q_rows=q_rows)
    full = lambda col: pl.BlockSpec((1, seq_len, LANES), lambda i, r, hp, j: (i, 0, r * L0_BLOCKS + col + hp))
    oblk = pl.BlockSpec((1, q_rows, LANES), lambda i, r, hp, j: (i, j, r * hpairs + hp))
    o, lse = pl.pallas_call(
        kernel,
        grid=(b, dil, hpairs, seq_len // q_rows),
        in_specs=[pl.BlockSpec((1, q_rows, LANES), lambda i, r, hp, j: (i, j, r * L0_BLOCKS + L0_DQ + hp)),
                  full(L0_DK), full(L0_DV),
                  pl.BlockSpec((1, SUBLANES, LANES), lambda i, r, hp, j: (hp, 0, 0))],
        out_specs=[oblk, oblk],
        out_shape=[jax.ShapeDtypeStruct((b, seq_len, dil * DIL_HEADS * DIL_DH), F32)] * 2,
        compiler_params=_cparams(("parallel", "parallel", "parallel", "arbitrary")),
        name=f"l0_dilated_d{dil}",
    )(pv, pv, pv, slope_tab)
    shape = (b, s, DIL_HEADS * DIL_DH)
    return o.reshape(shape), lse.reshape(shape)


def _dilated_combine_kernel(o1, l1, o2, l2, o3, l3, out_ref):
    a, b, c = l1[...], l2[...], l3[...]
    m = jnp.maximum(jnp.maximum(a, b), c)
    ea, eb, ec = jnp.exp(a - m), jnp.exp(b - m), jnp.exp(c - m)
    den = ea + eb + ec
    out_ref[...] = ((ea / den) * o1[...] + (eb / den) * o2[...] + (ec / den) * o3[...]).astype(out_ref.dtype)


def _dilated_attention(p0):
    b, s, _ = p0.shape
    parts = []
    for window, dil in DIL_PATTERNS:
        parts.extend(_dilated_pattern(p0, window, dil))
    width = DIL_HEADS * DIL_DH
    parts = [p.reshape(b * s, width) for p in parts]
    tm = 1024
    blk = pl.BlockSpec((tm, width), lambda i: (i, 0))
    out = pl.pallas_call(
        _dilated_combine_kernel,
        grid=(b * s // tm,),
        in_specs=[blk] * 6,
        out_specs=blk,
        out_shape=jax.ShapeDtypeStruct((b * s, width), BF16),
        compiler_params=_cparams(("parallel",)),
        name="l0_dilated_combine",
    )(*parts)
    return out.reshape(b, s, width)


def _l0_out_ffn_kernel(x_ref, ret_ref, dil_ref, wo_ref, g_ref, wg_ref, wu_ref, wd_ref, o_ref,
                       x1_ref, hn_ref, acc_ref):
    j = pl.program_id(1)

    @pl.when(j == 0)
    def _():
        half = wo_ref.shape[0] // 2
        x1 = (x_ref[...]
              + jnp.dot(ret_ref[...], wo_ref[0:half, :], preferred_element_type=F32)
              + jnp.dot(dil_ref[...], wo_ref[half:, :], preferred_element_type=F32))
        x1_ref[...] = x1
        hn_ref[...] = _rms(x1, g_ref[...]).astype(BF16)
        acc_ref[...] = jnp.zeros_like(acc_ref)

    h = hn_ref[...]
    a = jnp.dot(h, wg_ref[...], preferred_element_type=F32)
    u = jnp.dot(h, wu_ref[...], preferred_element_type=F32)
    act = (a * jax.nn.sigmoid(a) * u).astype(BF16)
    acc_ref[...] += jnp.dot(act, wd_ref[...], preferred_element_type=F32)

    @pl.when(j == pl.num_programs(1) - 1)
    def _():
        o_ref[...] = x1_ref[...] + acc_ref[...]


def _l0_out_ffn(x2d, ret2d, dil2d, w_out, ffn_norm, w_gate, w_up, w_down, tm=512):
    t, d = x2d.shape
    f = w_gate.shape[1]
    tf = f // 2 if (f // 2) % LANES == 0 else f
    half = w_out.shape[0] // 2
    return pl.pallas_call(
        _l0_out_ffn_kernel,
        grid=(t // tm, f // tf),
        in_specs=[
            pl.BlockSpec((tm, d), lambda i, j: (i, 0)),
            pl.BlockSpec((tm, half), lambda i, j: (i, 0)),
            pl.BlockSpec((tm, half), lambda i, j: (i, 0)),
            pl.BlockSpec((2 * half, d), lambda i, j: (0, 0)),
            pl.BlockSpec((1, d), lambda i, j: (0, 0)),
            pl.BlockSpec((d, tf), lambda i, j: (0, j)),
            pl.BlockSpec((d, tf), lambda i, j: (0, j)),
            pl.BlockSpec((tf, d), lambda i, j: (j, 0)),
        ],
        out_specs=pl.BlockSpec((tm, d), lambda i, j: (i, 0)),
        out_shape=jax.ShapeDtypeStruct((t, d), F32),
        scratch_shapes=[pltpu.VMEM((tm, d), F32), pltpu.VMEM((tm, d), BF16), pltpu.VMEM((tm, d), F32)],
        compiler_params=_cparams(("parallel", "arbitrary")),
        name="l0_out_proj_ffn",
    )(x2d, ret2d, dil2d, w_out, ffn_norm.reshape(1, d), w_gate, w_up, w_down)


L1_W1_COLS = MLA_Q_RANK + MLA_KV_RANK + 2 * LANES + S5_CH
HEAD_BLOCK = LANES


def _l1_proj_kernel(x_ref, an_ref, w1_ref, qn_ref, wq_ref, kvn_ref, wkv_ref, ct_ref, st_ref,
                    q_out, k_out, v_out, u_out):
    xn = _rms(x_ref[0], an_ref[...]).astype(BF16)
    proj = jnp.dot(xn, w1_ref[...], preferred_element_type=F32)
    c0 = MLA_Q_RANK
    c1 = c0 + MLA_KV_RANK
    cq = proj[:, 0:c0]
    ckv = proj[:, c0:c1]
    ka = proj[:, c1:c1 + LANES]
    kb = proj[:, c1 + LANES:c1 + 2 * LANES]
    u_out[...] = proj[:, c1 + 2 * LANES:]

    ct = ct_ref[...]
    st = st_ref[...]
    ct8 = jnp.concatenate([ct] * MLA_HEADS, axis=1)
    st8 = jnp.concatenate([st] * MLA_HEADS, axis=1)
    width = MLA_HEADS * HEAD_BLOCK

    q2 = jnp.dot(_rms(cq, qn_ref[...]).astype(BF16), wq_ref[...], preferred_element_type=F32)
    scale = (MLA_NOPE + MLA_ROPE) ** -0.5
    q_out[0] = ((q2[:, 0:width] * ct8 + q2[:, width:] * st8) * scale).astype(q_out.dtype)

    kv = jnp.dot(_rms(ckv, kvn_ref[...]).astype(BF16), wkv_ref[...], preferred_element_type=F32)
    krot = ka * ct + kb * st
    k_out[0] = (kv[:, 0:width] + jnp.concatenate([krot] * MLA_HEADS, axis=1)).astype(k_out.dtype)
    v_out[0] = kv[:, width:].astype(v_out.dtype)


def _l1_weights(w_in, w_uq, w_ukv):
    c0 = MLA_Q_RANK
    c1 = c0 + MLA_KV_RANK
    c2 = c1 + MLA_ROPE
    hr = MLA_ROPE // 2
    kpe = w_in[:, c1:c2]
    zeros = lambda n: jnp.zeros((w_in.shape[0], n), w_in.dtype)
    ka = jnp.concatenate([zeros(MLA_NOPE), kpe, zeros(LANES - MLA_NOPE - MLA_ROPE)], axis=1)
    kb = jnp.concatenate([zeros(MLA_NOPE), kpe[:, hr:], kpe[:, :hr], zeros(LANES - MLA_NOPE - MLA_ROPE)], axis=1)
    w1 = jnp.concatenate([w_in[:, :c1], ka, kb, w_in[:, c2:]], axis=1).astype(BF16)

    qd = MLA_NOPE + MLA_ROPE
    wq = w_uq.reshape(MLA_Q_RANK, MLA_HEADS, qd)
    zq = lambda n: jnp.zeros((MLA_Q_RANK, MLA_HEADS, n), w_uq.dtype)
    qa = jnp.concatenate([wq, zq(HEAD_BLOCK - qd)], axis=2)
    qb = jnp.concatenate([zq(MLA_NOPE), wq[:, :, MLA_NOPE + hr:], wq[:, :, MLA_NOPE:MLA_NOPE + hr],
                          zq(HEAD_BLOCK - qd)], axis=2)
    wq2 = jnp.concatenate([qa.reshape(MLA_Q_RANK, -1), qb.reshape(MLA_Q_RANK, -1)], axis=1).astype(BF16)

    wkv = w_ukv.reshape(MLA_KV_RANK, MLA_HEADS, MLA_NOPE + MLA_V)
    kpart = jnp.concatenate([wkv[:, :, :MLA_NOPE],
                             jnp.zeros((MLA_KV_RANK, MLA_HEADS, HEAD_BLOCK - MLA_NOPE), w_ukv.dtype)], axis=2)
    vpart = wkv[:, :, MLA_NOPE:]
    wkv2 = jnp.concatenate([kpart.reshape(MLA_KV_RANK, -1), vpart.reshape(MLA_KV_RANK, -1)], axis=1).astype(BF16)
    return w1, wq2, wkv2


def _rope_lane_tables(s):
    hr = MLA_ROPE // 2
    inv = ROPE_BASE ** (-jnp.arange(0, MLA_ROPE, 2, dtype=F32) / MLA_ROPE)
    ang = jnp.arange(s, dtype=F32)[:, None] * inv[None, :]
    cos, sin = jnp.cos(ang), jnp.sin(ang)
    pad = jnp.zeros((s, HEAD_BLOCK - MLA_NOPE - MLA_ROPE), F32)
    ct = jnp.concatenate([jnp.ones((s, MLA_NOPE), F32), cos, cos, pad], axis=1)
    st = jnp.concatenate([jnp.zeros((s, MLA_NOPE), F32), -sin, sin, pad], axis=1)
    assert ct.shape[1] == HEAD_BLOCK and hr * 2 == MLA_ROPE
    return ct, st


def _l1_proj(x, attn_norm, w1, q_norm, wq2, kv_norm, wkv2, ct, st, nb_pad, tm=512):
    b, s, d = x.shape
    width = MLA_HEADS * HEAD_BLOCK
    const = lambda shape: pl.BlockSpec(shape, lambda i, j: (0, 0))
    tok = lambda n: pl.BlockSpec((1, tm, n), lambda i, j: (i, j, 0))
    return pl.pallas_call(
        _l1_proj_kernel,
        grid=(b, s // tm),
        in_specs=[tok(d), const((1, d)), const(w1.shape), const((1, MLA_Q_RANK)), const(wq2.shape),
                  const((1, MLA_KV_RANK)), const(wkv2.shape),
                  pl.BlockSpec((tm, HEAD_BLOCK), lambda i, j: (j, 0)),
                  pl.BlockSpec((tm, HEAD_BLOCK), lambda i, j: (j, 0))],
        out_specs=[tok(width), tok(width), tok(MLA_HEADS * MLA_V),
                   pl.BlockSpec((tm, S5_CH), lambda i, j: (j, i))],
        out_shape=[jax.ShapeDtypeStruct((b, s, width), BF16),
                   jax.ShapeDtypeStruct((b, s, width), BF16),
                   jax.ShapeDtypeStruct((b, s, MLA_HEADS * MLA_V), BF16),
                   jax.ShapeDtypeStruct((s, nb_pad * S5_CH), F32)],
        compiler_params=_cparams(("parallel", "parallel")),
        name="l1_norm_in_proj",
    )(x, attn_norm.reshape(1, d), w1, q_norm.reshape(1, -1), wq2, kv_norm.reshape(1, -1), wkv2, ct, st)


def _mla_kernel(q_ref, k_ref, v_ref, o_ref, *, seq_len, tk):
    tq = q_ref.shape[1]
    lane = lax.broadcasted_iota(jnp.int32, (tq, LANES), 1)
    lo = lane < MLA_V
    qs = [q_ref[0, :, 0:HEAD_BLOCK], q_ref[0, :, HEAD_BLOCK:]]

    def body(j, carry):
        sl = pl.ds(pl.multiple_of(j * tk, tk), tk)
        vt = v_ref[0, sl, :]
        new = []
        for hh in range(2):
            m, l, acc = carry[hh]
            kt = k_ref[0, sl, hh * HEAD_BLOCK:(hh + 1) * HEAD_BLOCK]
            s = lax.dot_general(qs[hh], kt, (((1,), (1,)), ((), ())), preferred_element_type=F32)
            m_new = jnp.maximum(m, jnp.max(s, axis=-1, keepdims=True))
            alpha = jnp.exp(m - m_new)
            p = jnp.exp(s - m_new)
            l_new = alpha * l + jnp.sum(p, axis=-1, keepdims=True)
            acc_new = alpha * acc + jnp.dot(p.astype(BF16), vt, preferred_element_type=F32)
            new.append((m_new, l_new, acc_new))
        return tuple(new)

    init = tuple((jnp.full((tq, 1), NEG_INF, F32), jnp.zeros((tq, 1), F32), jnp.zeros((tq, LANES), F32))
                 for _ in range(2))
    (m0, l0, a0), (m1, l1, a1) = lax.fori_loop(0, seq_len // tk, body, init)
    o_ref[0] = jnp.where(lo, a0 / l0, a1 / l1).astype(o_ref.dtype)


def _mla_attention(q, k, v, tq=256, tk=512):
    b, s, _ = q.shape
    hpairs = MLA_HEADS // 2
    return pl.pallas_call(
        functools.partial(_mla_kernel, seq_len=s, tk=tk),
        grid=(b, hpairs, s // tq),
        in_specs=[pl.BlockSpec((1, tq, 2 * HEAD_BLOCK), lambda i, hp, j: (i, j, hp)),
                  pl.BlockSpec((1, s, 2 * HEAD_BLOCK), lambda i, hp, j: (i, 0, hp)),
                  pl.BlockSpec((1, s, LANES), lambda i, hp, j: (i, 0, hp))],
        out_specs=pl.BlockSpec((1, tq, LANES), lambda i, hp, j: (i, j, hp)),
        out_shape=jax.ShapeDtypeStruct((b, s, MLA_HEADS * MLA_V), BF16),
        compiler_params=_cparams(("parallel", "parallel", "arbitrary")),
        name="l1_latent_attention",
    )(q, k, v)


def _s5_kernel(u_ref, bd_ref, are_ref, aim_ref, cd_ref, y_ref, bu_ref, x_ref, *, reverse):
    tc, nb, _ = u_ref.shape
    n = S5_NSTATE

    @pl.when(pl.program_id(0) == 0)
    def _():
        x_ref[...] = jnp.zeros_like(x_ref)

    u2 = u_ref[...].reshape(tc * nb, S5_CH).astype(BF16)
    bu_ref[...] = jnp.dot(u2, bd_ref[...], preferred_element_type=F32).reshape(tc, nb, 2 * n)

    lane_blk = 512
    for c0 in range(0, n, lane_blk):
        re = slice(c0, c0 + lane_blk)
        im = slice(n + c0, n + c0 + lane_blk)
        a_re = are_ref[:, re]
        a_im = aim_ref[:, re]

        def step(i, carry):
            t = tc - 1 - i if reverse else i
            xr, xi = carry
            nr = a_re * xr - a_im * xi + bu_ref[t, :, re]
            ni = a_re * xi + a_im * xr + bu_ref[t, :, im]
            bu_ref[t, :, re] = nr
            bu_ref[t, :, im] = ni
            return nr, ni

        xr, xi = lax.fori_loop(0, tc, step, (x_ref[:, re], x_ref[:, im]))
        x_ref[:, re] = xr
        x_ref[:, im] = xi

    xs = bu_ref[...].reshape(tc * nb, 2 * n).astype(BF16)
    y_ref[...] = jnp.dot(xs, cd_ref[...], preferred_element_type=F32).reshape(tc, nb, S5_CH)


def _s5_direction_params(lam_re, lam_im, log_step, b_re, b_im, c_re, c_im, nb):
    lam = lax.complex(jnp.minimum(lam_re.astype(F32), -1e-4), lam_im.astype(F32))
    step = jnp.exp(log_step.astype(F32))[:, None]
    lam_bar = jnp.exp(lam * step)
    bmat = lax.complex(b_re.astype(F32), b_im.astype(F32))
    b_bar = ((lam_bar - 1.0) / lam)[:, :, None] * bmat
    eye = jnp.eye(S5_GROUPS, dtype=F32)
    bd = lambda t: jnp.einsum('gpc,gh->gchp', t, eye).reshape(S5_CH, S5_NSTATE)
    bdense = jnp.concatenate([bd(jnp.real(b_bar)), bd(jnp.imag(b_bar))], axis=1).astype(BF16)
    cdm = lambda t: jnp.einsum('gcp,gh->gphc', t, eye).reshape(S5_NSTATE, S5_CH)
    cdense = jnp.concatenate([cdm(c_re.astype(F32)), -cdm(c_im.astype(F32))], axis=0).astype(BF16)
    a_re = jnp.broadcast_to(jnp.real(lam_bar).reshape(1, S5_NSTATE), (nb, S5_NSTATE)).astype(F32)
    a_im = jnp.broadcast_to(jnp.imag(lam_bar).reshape(1, S5_NSTATE), (nb, S5_NSTATE)).astype(F32)
    return bdense, a_re, a_im, cdense


def _s5_scan(u_tm, params, reverse, tc=64):
    s, nb, _ = u_tm.shape
    bdense, a_re, a_im, cdense = params
    nchunks = s // tc
    tmap = (lambda i: (nchunks - 1 - i, 0, 0)) if reverse else (lambda i: (i, 0, 0))
    const = lambda shape: pl.BlockSpec(shape, lambda i: (0, 0))
    return pl.pallas_call(
        functools.partial(_s5_kernel, reverse=reverse),
        grid=(nchunks,),
        in_specs=[pl.BlockSpec((tc, nb, S5_CH), tmap), const(bdense.shape), const(a_re.shape),
                  const(a_im.shape), const(cdense.shape)],
        out_specs=pl.BlockSpec((tc, nb, S5_CH), tmap),
        out_shape=jax.ShapeDtypeStruct((s, nb, S5_CH), F32),
        scratch_shapes=[pltpu.VMEM((tc, nb, 2 * S5_NSTATE), F32), pltpu.VMEM((nb, 2 * S5_NSTATE), F32)],
        compiler_params=_cparams(("arbitrary",)),
        name="l1_s5_scan_bwd" if reverse else "l1_s5_scan_fwd",
    )(u_tm, bdense, a_re, a_im, cdense)


def _l1_out_router_kernel(x_ref, mla_ref, yf_ref, yb_ref, u_ref, dskip_ref, gw_ref, gb_ref, wo_ref,
                          fn_ref, rt_ref, x1_out, hb_out, route_out):
    u = u_ref[...]
    y = yf_ref[...] + yb_ref[...] + dskip_ref[...] * u
    z = jax.nn.gelu(y)
    gate = jax.nn.sigmoid(jnp.dot(z.astype(BF16), gw_ref[...], preferred_element_type=F32) + gb_ref[...])
    ssm = (z * gate).astype(BF16)
    half = wo_ref.shape[0] // 2
    x1 = (x_ref[0]
          + jnp.dot(mla_ref[0], wo_ref[0:half, :], preferred_element_type=F32)
          + jnp.dot(ssm, wo_ref[half:, :], preferred_element_type=F32))
    x1_out[0] = x1
    hn = _rms(x1, fn_ref[...])
    hb_out[0] = hn.astype(BF16)

    logits = jnp.dot(hn, rt_ref[...], preferred_element_type=F32, precision=lax.Precision.HIGHEST)
    lane = lax.broadcasted_iota(jnp.int32, logits.shape, 1)
    lg = jnp.where(lane < N_EXPERTS, logits, -jnp.inf)
    m1 = jnp.max(lg, axis=-1, keepdims=True)
    i1 = jnp.min(jnp.where(lg == m1, lane, LANES), axis=-1, keepdims=True)
    lg2 = jnp.where(lane == i1, -jnp.inf, lg)
    m2 = jnp.max(lg2, axis=-1, keepdims=True)
    i2 = jnp.min(jnp.where(lg2 == m2, lane, LANES), axis=-1, keepdims=True)
    e2 = jnp.exp(m2 - m1)
    w1 = 1.0 / (1.0 + e2)
    w2 = e2 / (1.0 + e2)
    route_out[0] = jnp.where(lane == 0, i1.astype(F32),
                             jnp.where(lane == 1, i2.astype(F32),
                                       jnp.where(lane == 2, w1, jnp.where(lane == 3, w2, 0.0))))


def _l1_out_router(x, mla, yf, yb, u_tm, d_skip, glu_w, glu_b, w_out, ffn_norm, router_pad, tm=512):
    b, s, d = x.shape
    tok = lambda n: pl.BlockSpec((1, tm, n), lambda i, j: (i, j, 0))
    tmaj = pl.BlockSpec((tm, S5_CH), lambda i, j: (j, i))
    const = lambda shape: pl.BlockSpec(shape, lambda i, j: (0, 0))
    return pl.pallas_call(
        _l1_out_router_kernel,
        grid=(b, s // tm),
        in_specs=[tok(d), tok(MLA_HEADS * MLA_V), tmaj, tmaj, tmaj, const((1, S5_CH)),
                  const(glu_w.shape), const((1, S5_CH)), const(w_out.shape), const((1, d)),
                  const(router_pad.shape)],
        out_specs=[tok(d), tok(d), tok(LANES)],
        out_shape=[jax.ShapeDtypeStruct((b, s, d), F32),
                   jax.ShapeDtypeStruct((b, s, d), BF16),
                   jax.ShapeDtypeStruct((b, s, LANES), F32)],
        compiler_params=_cparams(("parallel", "parallel")),
        name="l1_out_proj_router",
    )(x, mla, yf, yb, u_tm, d_skip.reshape(1, -1), glu_w, glu_b.reshape(1, -1), w_out,
      ffn_norm.reshape(1, d), router_pad)


GATHER_ROWS = 256


def _gather_rows_kernel(idx_ref, src_ref, out_ref, sem):
    base = pl.program_id(0) * GATHER_ROWS

    def issue(r, carry):
        pltpu.make_async_copy(src_ref.at[idx_ref[r]], out_ref.at[base + r], sem).start()
        return carry

    lax.fori_loop(0, GATHER_ROWS, issue, 0)
    pltpu.make_async_copy(src_ref.at[pl.ds(0, GATHER_ROWS)], out_ref.at[pl.ds(base, GATHER_ROWS)], sem).wait()


def _gather_rows(src, idx):
    n = idx.shape[0]
    rows, width = src.shape
    src3 = src.reshape(rows, width // LANES, LANES)
    out = pl.pallas_call(
        _gather_rows_kernel,
        grid=(n // GATHER_ROWS,),
        in_specs=[pl.BlockSpec((GATHER_ROWS,), lambda i: (i,), memory_space=pltpu.SMEM),
                  pl.BlockSpec(memory_space=pl.ANY)],
        out_specs=pl.BlockSpec(memory_space=pl.ANY),
        out_shape=jax.ShapeDtypeStruct((n, width // LANES, LANES), src.dtype),
        scratch_shapes=[pltpu.SemaphoreType.DMA(())],
        compiler_params=_cparams(("arbitrary",)),
        name="row_gather",
    )(idx, src3)
    return out.reshape(n, width)


def _expert_ffn_kernel(te_ref, nt_ref, x_ref, gw_ref, wg_ref, wu_ref, wd_ref, o_ref, acc_ref):
    i = pl.program_id(0)
    j = pl.program_id(1)

    @pl.when(i < nt_ref[0])
    def _():
        @pl.when(j == 0)
        def _():
            acc_ref[...] = jnp.zeros_like(acc_ref)

        h = x_ref[...]
        a = jnp.dot(h, wg_ref[0], preferred_element_type=F32)
        u = jnp.dot(h, wu_ref[0], preferred_element_type=F32)
        act = (a * jax.nn.sigmoid(a) * u).astype(BF16)
        acc_ref[...] += jnp.dot(act, wd_ref[0], preferred_element_type=F32)

        @pl.when(j == pl.num_programs(1) - 1)
        def _():
            o_ref[...] = gw_ref[...] * acc_ref[...]

    @pl.when(jnp.logical_and(i >= nt_ref[0], j == pl.num_programs(1) - 1))
    def _():
        o_ref[...] = jnp.zeros_like(o_ref)


def _expert_ffn(xs, gate_w, tile_expert, n_tiles_used, wg, wu, wd, tm, tf=896):
    a_pad, d = xs.shape
    f = wg.shape[2]
    grid_spec = pltpu.PrefetchScalarGridSpec(
        num_scalar_prefetch=2,
        grid=(a_pad // tm, f // tf),
        in_specs=[
            pl.BlockSpec((tm, d), lambda i, j, te, nt: (i, 0)),
            pl.BlockSpec((tm, 1), lambda i, j, te, nt: (i, 0)),
            pl.BlockSpec((1, d, tf), lambda i, j, te, nt: (te[i], 0, j)),
            pl.BlockSpec((1, d, tf), lambda i, j, te, nt: (te[i], 0, j)),
            pl.BlockSpec((1, tf, d), lambda i, j, te, nt: (te[i], j, 0)),
        ],
        out_specs=pl.BlockSpec((tm, d), lambda i, j, te, nt: (i, 0)),
        scratch_shapes=[pltpu.VMEM((tm, d), F32)],
    )
    return pl.pallas_call(
        _expert_ffn_kernel,
        grid_spec=grid_spec,
        out_shape=jax.ShapeDtypeStruct((a_pad, d), F32),
        compiler_params=_cparams(("arbitrary", "arbitrary")),
        name="l1_expert_ffn",
    )(tile_expert, n_tiles_used, xs, gate_w, wg, wu, wd)


def _combine_norm_kernel(x_ref, a_ref, b_ref, g_ref, o_ref):
    o_ref[...] = _rms(x_ref[...] + (a_ref[...] + b_ref[...]), g_ref[...])


def _combine_norm(x1, picked, final_norm, tm=512):
    t, d = x1.shape
    nblk = t // tm
    return pl.pallas_call(
        _combine_norm_kernel,
        grid=(nblk,),
        in_specs=[pl.BlockSpec((tm, d), lambda i: (i, 0)),
                  pl.BlockSpec((tm, d), lambda i: (i, 0)),
                  pl.BlockSpec((tm, d), lambda i: (i + nblk, 0)),
                  pl.BlockSpec((1, d), lambda i: (0, 0))],
        out_specs=pl.BlockSpec((tm, d), lambda i: (i, 0)),
        out_shape=jax.ShapeDtypeStruct((t, d), F32),
        compiler_params=_cparams(("parallel",)),
        name="l1_combine_final_norm",
    )(x1, picked, picked, final_norm.reshape(1, d))


def _moe(x1, hb, route, wg, wu, wd, final_norm, tm=512):
    t, d = x1.shape
    e_idx = jnp.concatenate([route[:, 0], route[:, 1]]).astype(jnp.int32)
    e_w = jnp.concatenate([route[:, 2], route[:, 3]])
    n_assign = TOP_K * t
    order = jnp.argsort(e_idx, stable=True).astype(jnp.int32)
    inv = jnp.argsort(order).astype(jnp.int32)
    counts = jnp.sum(e_idx[:, None] == jnp.arange(N_EXPERTS, dtype=jnp.int32)[None, :], axis=0).astype(jnp.int32)
    starts = jnp.cumsum(counts) - counts
    padded = ((counts + tm - 1) // tm) * tm
    pad_ends = jnp.cumsum(padded)
    pad_starts = pad_ends - padded
    a_pad = n_assign + N_EXPERTS * tm
    n_tiles = a_pad // tm

    tile_start = jnp.arange(n_tiles, dtype=jnp.int32) * tm
    tile_expert = jnp.minimum(jnp.sum(tile_start[:, None] >= pad_ends[None, :], axis=1), N_EXPERTS - 1).astype(jnp.int32)
    n_tiles_used = (pad_ends[-1] // tm).astype(jnp.int32).reshape(1)

    slot = jnp.arange(a_pad, dtype=jnp.int32)
    slot_e = jnp.repeat(tile_expert, tm)
    within = slot - pad_starts[slot_e]
    valid = within < counts[slot_e]
    src = order[jnp.clip(starts[slot_e] + within, 0, n_assign - 1)]
    src_tok = jnp.where(valid, src % t, 0).astype(jnp.int32)
    gate_w = jnp.where(valid, e_w[src], 0.0).astype(F32).reshape(a_pad, 1)

    xs = _gather_rows(hb, src_tok)
    ys = _expert_ffn(xs, gate_w, tile_expert, n_tiles_used, wg, wu, wd, tm)
    dest = (pad_starts[e_idx] + inv - starts[e_idx]).astype(jnp.int32)
    picked = _gather_rows(ys, dest)
    return _combine_norm(x1, picked, final_norm)


def _l0_in_weight(w_in):
    hq = RET_HEADS * RET_DK
    hv = RET_HEADS * RET_DV
    hd = DIL_HEADS * DIL_DH
    rq, rk, rv, rg, dq, dk, dv = jnp.split(
        w_in, [hq, 2 * hq, 2 * hq + hv, 2 * hq + 2 * hv, 2 * hq + 2 * hv + hd, 2 * hq + 2 * hv + 2 * hd], axis=1)
    dup = lambda t: jnp.concatenate([t.reshape(-1, RET_HEADS, 1, RET_DK)] * 2, axis=2).reshape(-1, 2 * hq)
    w = jnp.concatenate([dup(rq) * RET_DK ** -0.5, dup(rk), rv, rg, dq * DIL_DH ** -0.5, dk, dv], axis=1)
    assert w.shape[1] == L0_COLS
    return w.astype(BF16)


def _prepare(l0_w_in, l0_ret_decay_f, l0_ret_decay_b, l0_w_out, l0_ffn_w_gate, l0_ffn_w_up, l0_ffn_w_down,
             l1_w_in, l1_mla_w_uq, l1_mla_w_ukv, l1_s5_glu_w, l1_w_out, l1_router,
             l1_exp_w_gate, l1_exp_w_up, l1_exp_w_down):
    w1, wq2, wkv2 = _l1_weights(l1_w_in, l1_mla_w_uq, l1_mla_w_ukv)
    router_pad = jnp.zeros((D_MODEL, LANES), F32).at[:, :N_EXPERTS].set(l1_router.astype(F32))
    return dict(
        l0_w_in=_l0_in_weight(l0_w_in),
        ret_tables=_retention_tables(l0_ret_decay_f, l0_ret_decay_b),
        l0_w_out=l0_w_out.astype(BF16),
        l0_wg=l0_ffn_w_gate.astype(BF16), l0_wu=l0_ffn_w_up.astype(BF16), l0_wd=l0_ffn_w_down.astype(BF16),
        l1_w1=w1, l1_wq2=wq2, l1_wkv2=wkv2,
        glu_w=l1_s5_glu_w.astype(BF16), l1_w_out=l1_w_out.astype(BF16), router_pad=router_pad,
        exp_wg=l1_exp_w_gate.astype(BF16), exp_wu=l1_exp_w_up.astype(BF16), exp_wd=l1_exp_w_down.astype(BF16),
    )


def _trunk(x, prep, p):
    b, s, d = x.shape
    t = b * s
    p0 = _norm_matmul(x.reshape(t, d), p['l0_attn_norm'], prep['l0_w_in']).reshape(b, s, L0_COLS)
    ret = _retention(p0, prep['ret_tables'], p['l0_ret_gn'])
    dil = _dilated_attention(p0)
    x = _l0_out_ffn(x.reshape(t, d), ret.reshape(t, -1), dil.reshape(t, -1), prep['l0_w_out'],
                    p['l0_ffn_norm'], prep['l0_wg'], prep['l0_wu'], prep['l0_wd']).reshape(b, s, d)
    nb = -(-b // SUBLANES) * SUBLANES
    ct, st = _rope_lane_tables(s)
    q, k, v, u_tm = _l1_proj(x, p['l1_attn_norm'], prep['l1_w1'], p['l1_mla_q_norm'], prep['l1_wq2'],
                             p['l1_mla_kv_norm'], prep['l1_wkv2'], ct, st, b)
    mla = _mla_attention(q, k, v)
    u3 = u_tm.reshape(s, b, S5_CH)
    if nb != b:
        u3 = jnp.pad(u3, ((0, 0), (0, nb - b), (0, 0)))
    sf = _s5_direction_params(p['l1_s5_lam_re_f'], p['l1_s5_lam_im_f'], p['l1_s5_log_step_f'],
                              p['l1_s5_b_re'], p['l1_s5_b_im'], p['l1_s5_c_re'], p['l1_s5_c_im'], nb)
    sb = _s5_direction_params(p['l1_s5_lam_re_b'], p['l1_s5_lam_im_b'], p['l1_s5_log_step_b'],
                              p['l1_s5_b_re'], p['l1_s5_b_im'], p['l1_s5_c_re'], p['l1_s5_c_im'], nb)
    yf = _s5_scan(u3, sf, reverse=False).reshape(s, nb * S5_CH)
    yb = _s5_scan(u3, sb, reverse=True).reshape(s, nb * S5_CH)
    x1, hb, route = _l1_out_router(x, mla, yf, yb, u_tm, p['l1_s5_d'], prep['glu_w'], p['l1_s5_glu_b'],
                                   prep['l1_w_out'], p['l1_ffn_norm'], prep['router_pad'])
    out = _moe(x1.reshape(t, d), hb.reshape(t, d), route.reshape(t, LANES),
               prep['exp_wg'], prep['exp_wu'], prep['exp_wd'], p['final_norm'])
    return out.reshape(b, s, d)


def kernel(x_prompt, x_sample, l0_attn_norm, l0_w_in, l0_ret_decay_f, l0_ret_decay_b, l0_ret_gn, l0_w_out,
           l0_ffn_norm, l0_ffn_w_gate, l0_ffn_w_up, l0_ffn_w_down, l1_attn_norm, l1_w_in, l1_mla_q_norm,
           l1_mla_w_uq, l1_mla_kv_norm, l1_mla_w_ukv, l1_s5_lam_re_f, l1_s5_lam_im_f, l1_s5_log_step_f,
           l1_s5_lam_re_b, l1_s5_lam_im_b, l1_s5_log_step_b, l1_s5_b_re, l1_s5_b_im, l1_s5_c_re, l1_s5_c_im,
           l1_s5_d, l1_s5_glu_w, l1_s5_glu_b, l1_w_out, l1_ffn_norm, l1_router, l1_exp_w_gate, l1_exp_w_up,
           l1_exp_w_down, final_norm):
    p = dict(locals())
    prep = _prepare(l0_w_in, l0_ret_decay_f, l0_ret_decay_b, l0_w_out, l0_ffn_w_gate, l0_ffn_w_up,
                    l0_ffn_w_down, l1_w_in, l1_mla_w_uq, l1_mla_w_ukv, l1_s5_glu_w, l1_w_out, l1_router,
                    l1_exp_w_gate, l1_exp_w_up, l1_exp_w_down)
    return (_trunk(x_prompt, prep, p), _trunk(x_sample, prep, p))
```

```python
import functools
import math

import jax
import jax.numpy as jnp
from jax import lax
from jax.experimental import pallas as pl
from jax.experimental.pallas import tpu as pltpu

F32 = jnp.float32
BF16 = jnp.bfloat16

D_MODEL = 1024
EPS = 1e-6
NEG_INF = -1e30
RET_HEADS = 4
RET_DK = 64
RET_DV = 128
RET_CHUNK = 128
RET_UNROLL = 4
DIL_HEADS = 8
DIL_DH = 64
DIL_PATTERNS = ((128, 1), (512, 4), (2048, 16))
DIL_QBLOCK = 128
DIL_UNROLL = 4
MLA_HEADS = 8
MLA_Q_RANK = 256
MLA_KV_RANK = 128
MLA_NOPE = 64
MLA_ROPE = 32
MLA_V = 64
ROPE_BASE = 10000.0
S5_GROUPS = 32
S5_GROUP_CH = 16
S5_STATE = 64
S5_CH = S5_GROUPS * S5_GROUP_CH
S5_NSTATE = S5_GROUPS * S5_STATE
N_EXPERTS = 8
TOP_K = 2

LANES = 128
SUBLANES = 8
VMEM_LIMIT = 56 * 1024 * 1024

L0_COLS = 3584
L0_BLOCKS = L0_COLS // LANES
L0_RQ, L0_RK, L0_RV, L0_RG, L0_DQ, L0_DK, L0_DV = 0, 4, 8, 12, 16, 20, 24


def _cparams(sem):
    return pltpu.CompilerParams(dimension_semantics=sem, vmem_limit_bytes=VMEM_LIMIT)


def _rms(x, g):
    return x * lax.rsqrt(jnp.mean(x * x, axis=-1, keepdims=True) + EPS) * g


SLAB_ROWS = D_MODEL // LANES


def _store_token_slabs(ref2d, x, n_tok):
    for s in range(SLAB_ROWS):
        ref2d[pl.ds(s, n_tok, stride=SLAB_ROWS), :] = x[:, s * LANES:(s + 1) * LANES]


def _load_token_slabs(ref2d, n_tok):
    return jnp.concatenate([ref2d[pl.ds(s, n_tok, stride=SLAB_ROWS), :] for s in range(SLAB_ROWS)], axis=1)


def _norm_matmul_kernel(x_ref, g_ref, w_ref, o_ref, xn_ref):
    @pl.when(pl.program_id(1) == 0)
    def _():
        xn_ref[...] = _rms(x_ref[...], g_ref[...]).astype(BF16)

    o_ref[...] = jnp.dot(xn_ref[...], w_ref[...], preferred_element_type=F32).astype(o_ref.dtype)


def _norm_matmul(x2d, gain, w, tm=512, tn=512):
    t, d = x2d.shape
    n = w.shape[1]
    return pl.pallas_call(
        _norm_matmul_kernel,
        grid=(t // tm, n // tn),
        in_specs=[
            pl.BlockSpec((tm, d), lambda i, j: (i, 0)),
            pl.BlockSpec((1, d), lambda i, j: (0, 0)),
            pl.BlockSpec((d, tn), lambda i, j: (0, j)),
        ],
        out_specs=pl.BlockSpec((tm, tn), lambda i, j: (i, j)),
        out_shape=jax.ShapeDtypeStruct((t, n), BF16),
        scratch_shapes=[pltpu.VMEM((tm, d), BF16)],
        compiler_params=_cparams(("parallel", "arbitrary")),
        name="l0_norm_in_proj",
    )(x2d, gain.reshape(1, d), w)


def _retention_kernel(q_ref, k_ref, v_ref, g_ref, d_ref, qw_ref, kw_ref, cd_ref, gn_ref,
                      o_ref, kv_ref, p_ref, *, n_chunks):
    c = RET_CHUNK
    kw = kw_ref[0]
    qw = qw_ref[0]
    dmat = d_ref[0]
    gn = gn_ref[...]

    def kv_body(n, carry):
        sl = pl.ds(pl.multiple_of(n * c, c), c)
        kc = (k_ref[0, sl, :].astype(F32) * kw).T.astype(BF16)
        kv_ref[n] = jnp.dot(kc, v_ref[0, sl, :], preferred_element_type=F32)
        return carry

    lax.fori_loop(0, n_chunks, kv_body, 0, unroll=RET_UNROLL)

    half = c // 2
    dec_f = cd_ref[0, 0:half, :]
    dec_b = cd_ref[0, half:c, :]

    def fwd_body(n, s):
        p_ref[n, 0:half, :] = s.astype(BF16)
        return s * dec_f + kv_ref[n, 0:half, :]

    lax.fori_loop(0, n_chunks, fwd_body, jnp.zeros((half, RET_DV), F32))

    def bwd_body(i, s):
        n = n_chunks - 1 - i
        p_ref[n, half:c, :] = s.astype(BF16)
        return s * dec_b + kv_ref[n, half:c, :]

    lax.fori_loop(0, n_chunks, bwd_body, jnp.zeros((half, RET_DV), F32))

    def out_body(n, carry):
        sl = pl.ds(pl.multiple_of(n * c, c), c)
        qc = q_ref[0, sl, :]
        s = lax.dot_general(qc, k_ref[0, sl, :], (((1,), (1,)), ((), ())), preferred_element_type=F32)
        intra = jnp.dot((s * dmat).astype(BF16), v_ref[0, sl, :], preferred_element_type=F32)
        qq = (qc.astype(F32) * qw).astype(BF16)
        y = intra + jnp.dot(qq, p_ref[n], preferred_element_type=F32)
        mu = jnp.mean(y, axis=-1, keepdims=True)
        yc = y - mu
        var = jnp.mean(yc * yc, axis=-1, keepdims=True)
        yn = yc * lax.rsqrt(var + EPS) * gn
        gg = g_ref[0, sl, :].astype(F32)
        o_ref[0, sl, :] = (gg * jax.nn.sigmoid(gg) * yn).astype(o_ref.dtype)
        return carry

    lax.fori_loop(0, n_chunks, out_body, 0, unroll=RET_UNROLL)


def _retention_tables(decay_f, decay_b):
    c = RET_CHUNK
    lg_f = jax.nn.log_sigmoid(decay_f.astype(F32))[:, None, None]
    lg_b = jax.nn.log_sigmoid(decay_b.astype(F32))[:, None, None]
    j = jnp.arange(c, dtype=F32)
    diff = j[:, None] - j[None, :]
    dmat = 0.5 * jnp.where(diff >= 0, jnp.exp(lg_f * jnp.maximum(diff, 0.0)),
                           jnp.exp(lg_b * jnp.maximum(-diff, 0.0)))
    lane_f = (jnp.arange(LANES) < RET_DK)[None, None, :]
    jj = j[None, :, None]
    qw = jnp.where(lane_f, jnp.exp(lg_f * (jj + 1.0)), jnp.exp(lg_b * (c - jj)))
    kw = jnp.where(lane_f, jnp.exp(lg_f * (c - 1.0 - jj)), jnp.exp(lg_b * jj))
    row_f = (jnp.arange(c) < c // 2)[None, :, None]
    cd = jnp.where(row_f, jnp.exp(lg_f * c), jnp.exp(lg_b * c)) * jnp.ones((1, 1, RET_DV), F32)
    return dmat.astype(F32), qw.astype(F32), kw.astype(F32), cd.astype(F32)


def _retention(p0, tables, gn):
    b, s, _ = p0.shape
    dmat, qw, kw, cd = tables
    n_chunks = s // RET_CHUNK
    seq = lambda col: pl.BlockSpec((1, s, LANES), lambda i, h: (i, 0, col + h))
    tab = pl.BlockSpec((1, RET_CHUNK, LANES), lambda i, h: (h, 0, 0))
    return pl.pallas_call(
        functools.partial(_retention_kernel, n_chunks=n_chunks),
        grid=(b, RET_HEADS),
        in_specs=[seq(L0_RQ), seq(L0_RK), seq(L0_RV), seq(L0_RG), tab, tab, tab, tab,
                  pl.BlockSpec((1, LANES), lambda i, h: (0, h))],
        out_specs=pl.BlockSpec((1, s, LANES), lambda i, h: (i, 0, h)),
        out_shape=jax.ShapeDtypeStruct((b, s, RET_HEADS * RET_DV), BF16),
        scratch_shapes=[pltpu.VMEM((n_chunks, RET_CHUNK, RET_DV), F32),
                        pltpu.VMEM((n_chunks, RET_CHUNK, RET_DV), BF16)],
        compiler_params=_cparams(("parallel", "parallel")),
        name="l0_retention",
    )(p0, p0, p0, p0, dmat, qw, kw, cd, gn.reshape(1, -1))


def _dilated_kernel(*refs, seq_len, half, q_rows, n_hp, has_prev, final):
    q_ref, k_ref, v_ref, slope_ref = refs[:4]
    refs = refs[4:]
    if has_prev:
        po_ref, pl_ref = refs[:2]
        refs = refs[2:]
    o_ref = refs[0]
    lse_ref = None if final else refs[1]
    tq = DIL_QBLOCK
    kw = tq + 2 * half
    hp0 = pl.program_id(2) * n_hp
    base = pl.program_id(3) * q_rows
    lane = lax.broadcasted_iota(jnp.int32, (tq, LANES), 1)
    lo = lane < DIL_DH
    rq = lax.broadcasted_iota(jnp.int32, (tq, kw), 0)
    rk = lax.broadcasted_iota(jnp.int32, (tq, kw), 1)

    def blk(i, carry):
        off = pl.multiple_of(i * tq, tq)
        rows = pl.ds(off, tq)
        qs = base + off
        ks = pl.multiple_of(jnp.clip(qs - half, 0, seq_len - kw), half)
        dist = jnp.abs((qs + rq) - (ks + rk))
        valid = dist <= half
        distf = dist.astype(F32)
        for hp in range(n_hp):
            ls = slice(hp * LANES, (hp + 1) * LANES)
            q = q_ref[0, rows, ls]
            kwin = k_ref[0, pl.ds(ks, kw), ls]
            vwin = v_ref[0, pl.ds(ks, kw), ls]
            outs, lses = [], []
            for hh in range(2):
                qh = jnp.where(lo if hh == 0 else jnp.logical_not(lo), q, jnp.zeros_like(q))
                s = lax.dot_general(qh, kwin, (((1,), (1,)), ((), ())), preferred_element_type=F32)
                s = s - slope_ref[hp0 + hp, hh:hh + 1, 0:1] * distf
                s = jnp.where(valid, s, NEG_INF)
                m = jnp.max(s, axis=-1, keepdims=True)
                p = jnp.exp(s - m)
                den = jnp.sum(p, axis=-1, keepdims=True)
                outs.append(jnp.dot(p.astype(BF16), vwin, preferred_element_type=F32) / den)
                lses.append(m + jnp.log(den))
            o = jnp.where(lo, outs[0], outs[1])
            lse = jnp.where(lo, lses[0], lses[1])
            if has_prev:
                lse_p = pl_ref[0, rows, ls]
                top = jnp.maximum(lse_p, lse)
                wa = jnp.exp(lse_p - top)
                wb = jnp.exp(lse - top)
                den = wa + wb
                o = (wa * po_ref[0, rows, ls] + wb * o) / den
                lse = top + jnp.log(den)
            o_ref[0, rows, ls] = o.astype(o_ref.dtype)
            if not final:
                lse_ref[0, rows, ls] = lse
        return carry

    n_blk = q_rows // tq
    lax.fori_loop(0, n_blk, blk, 0, unroll=max(1, min(DIL_UNROLL // n_hp, n_blk)))


DIL_MAX_RESIDENT_ROWS = 2048


def _dilated_pattern(p0, window, dil, prev, final):
    b, s, _ = p0.shape
    width = DIL_HEADS * DIL_DH
    half = window // (2 * dil)
    seq_len = s // dil
    q_rows = min(seq_len, 1024)
    hpairs = DIL_HEADS // 2
    n_hp = hpairs if seq_len <= DIL_MAX_RESIDENT_ROWS else 1
    lanes = n_hp * LANES
    pv = p0.reshape(b, seq_len, dil * L0_COLS)
    slopes = jnp.exp2(-8.0 * jnp.arange(1, DIL_HEADS + 1, dtype=F32) / DIL_HEADS) * dil
    slope_tab = jnp.zeros((hpairs, SUBLANES, LANES), F32).at[:, 0:2, :].set(
        jnp.broadcast_to(slopes.reshape(hpairs, 2, 1), (hpairs, 2, LANES)))
    col = lambda c: (lambda i, r, hp, j: (i, 0, (r * L0_BLOCKS + c) // n_hp + hp))
    qcol = lambda i, r, hp, j: (i, j, (r * L0_BLOCKS + L0_DQ) // n_hp + hp)
    oblk = pl.BlockSpec((1, q_rows, lanes), lambda i, r, hp, j: (i, j, r * (hpairs // n_hp) + hp))
    in_specs = [pl.BlockSpec((1, q_rows, lanes), qcol),
                pl.BlockSpec((1, seq_len, lanes), col(L0_DK)),
                pl.BlockSpec((1, seq_len, lanes), col(L0_DV)),
                pl.BlockSpec((hpairs, SUBLANES, LANES), lambda i, r, hp, j: (0, 0, 0))]
    args = [pv, pv, pv, slope_tab]
    if prev is not None:
        in_specs += [oblk, oblk]
        args += [a.reshape(b, seq_len, dil * width) for a in prev]
    oshape = (b, seq_len, dil * width)
    if final:
        out_specs, out_shape = oblk, jax.ShapeDtypeStruct(oshape, BF16)
    else:
        out_specs, out_shape = [oblk, oblk], [jax.ShapeDtypeStruct(oshape, F32)] * 2
    res = pl.pallas_call(
        functools.partial(_dilated_kernel, seq_len=seq_len, half=half, q_rows=q_rows, n_hp=n_hp,
                          has_prev=prev is not None, final=final),
        grid=(b, dil, hpairs // n_hp, seq_len // q_rows),
        in_specs=in_specs,
        out_specs=out_specs,
        out_shape=out_shape,
        compiler_params=_cparams(("parallel", "parallel", "parallel", "arbitrary")),
        name=f"l0_dilated_d{dil}",
    )(*args)
    if final:
        return res.reshape(b, s, width)
    return tuple(a.reshape(b, s, width) for a in res)


def _dilated_attention(p0):
    prev = None
    for n, (window, dil) in enumerate(DIL_PATTERNS):
        prev = _dilated_pattern(p0, window, dil, prev, final=n == len(DIL_PATTERNS) - 1)
    return prev


def _l0_out_ffn_kernel(x_ref, ret_ref, dil_ref, wo_ref, g_ref, wg_ref, wu_ref, wd_ref, o_ref,
                       x1_ref, hn_ref, acc_ref):
    j = pl.program_id(1)

    @pl.when(j == 0)
    def _():
        half = wo_ref.shape[0] // 2
        x1 = (x_ref[...]
              + jnp.dot(ret_ref[...], wo_ref[0:half, :], preferred_element_type=F32)
              + jnp.dot(dil_ref[...], wo_ref[half:, :], preferred_element_type=F32))
        x1_ref[...] = x1
        hn_ref[...] = _rms(x1, g_ref[...]).astype(BF16)
        acc_ref[...] = jnp.zeros_like(acc_ref)

    h = hn_ref[...]
    a = jnp.dot(h, wg_ref[...], preferred_element_type=F32)
    u = jnp.dot(h, wu_ref[...], preferred_element_type=F32)
    act = (a * jax.nn.sigmoid(a) * u).astype(BF16)
    acc_ref[...] += jnp.dot(act, wd_ref[...], preferred_element_type=F32)

    @pl.when(j == pl.num_programs(1) - 1)
    def _():
        o_ref[...] = x1_ref[...] + acc_ref[...]


def _l0_out_ffn(x2d, ret2d, dil2d, w_out, ffn_norm, w_gate, w_up, w_down, tm=512):
    t, d = x2d.shape
    f = w_gate.shape[1]
    tf = f // 2 if (f // 2) % LANES == 0 else f
    half = w_out.shape[0] // 2
    return pl.pallas_call(
        _l0_out_ffn_kernel,
        grid=(t // tm, f // tf),
        in_specs=[
            pl.BlockSpec((tm, d), lambda i, j: (i, 0)),
            pl.BlockSpec((tm, half), lambda i, j: (i, 0)),
            pl.BlockSpec((tm, half), lambda i, j: (i, 0)),
            pl.BlockSpec((2 * half, d), lambda i, j: (0, 0)),
            pl.BlockSpec((1, d), lambda i, j: (0, 0)),
            pl.BlockSpec((d, tf), lambda i, j: (0, j)),
            pl.BlockSpec((d, tf), lambda i, j: (0, j)),
            pl.BlockSpec((tf, d), lambda i, j: (j, 0)),
        ],
        out_specs=pl.BlockSpec((tm, d), lambda i, j: (i, 0)),
        out_shape=jax.ShapeDtypeStruct((t, d), F32),
        scratch_shapes=[pltpu.VMEM((tm, d), F32), pltpu.VMEM((tm, d), BF16), pltpu.VMEM((tm, d), F32)],
        compiler_params=_cparams(("parallel", "arbitrary")),
        name="l0_out_proj_ffn",
    )(x2d, ret2d, dil2d, w_out, ffn_norm.reshape(1, d), w_gate, w_up, w_down)


L1_W1_COLS = MLA_Q_RANK + MLA_KV_RANK + 2 * LANES + S5_CH
HEAD_BLOCK = LANES


def _l1_proj_kernel(x_ref, an_ref, w1_ref, qn_ref, wq_ref, kvn_ref, wkv_ref, ct_ref, st_ref,
                    q_out, k_out, v_out, u_out):
    xn = _rms(x_ref[0], an_ref[...]).astype(BF16)
    proj = jnp.dot(xn, w1_ref[...], preferred_element_type=F32)
    c0 = MLA_Q_RANK
    c1 = c0 + MLA_KV_RANK
    cq = proj[:, 0:c0]
    ckv = proj[:, c0:c1]
    ka = proj[:, c1:c1 + LANES]
    kb = proj[:, c1 + LANES:c1 + 2 * LANES]
    u_out[...] = proj[:, c1 + 2 * LANES:]

    ct = ct_ref[...]
    st = st_ref[...]
    ct8 = jnp.concatenate([ct] * MLA_HEADS, axis=1)
    st8 = jnp.concatenate([st] * MLA_HEADS, axis=1)
    width = MLA_HEADS * HEAD_BLOCK

    q2 = jnp.dot(_rms(cq, qn_ref[...]).astype(BF16), wq_ref[...], preferred_element_type=F32)
    scale = (MLA_NOPE + MLA_ROPE) ** -0.5 * math.log2(math.e)
    q_out[0] = ((q2[:, 0:width] * ct8 + q2[:, width:] * st8) * scale).astype(q_out.dtype)

    kv = jnp.dot(_rms(ckv, kvn_ref[...]).astype(BF16), wkv_ref[...], preferred_element_type=F32)
    krot = ka * ct + kb * st
    k_out[0] = (kv[:, 0:width] + jnp.concatenate([krot] * MLA_HEADS, axis=1)).astype(k_out.dtype)
    lane = lax.broadcasted_iota(jnp.int32, (1, width), 1)
    ones = jnp.where(lane % HEAD_BLOCK >= MLA_V, 1.0, 0.0)
    v_out[0] = (kv[:, width:] + ones).astype(v_out.dtype)


def _l1_weights(w_in, w_uq, w_ukv):
    c0 = MLA_Q_RANK
    c1 = c0 + MLA_KV_RANK
    c2 = c1 + MLA_ROPE
    hr = MLA_ROPE // 2
    kpe = w_in[:, c1:c2]
    zeros = lambda n: jnp.zeros((w_in.shape[0], n), w_in.dtype)
    ka = jnp.concatenate([zeros(MLA_NOPE), kpe, zeros(LANES - MLA_NOPE - MLA_ROPE)], axis=1)
    kb = jnp.concatenate([zeros(MLA_NOPE), kpe[:, hr:], kpe[:, :hr], zeros(LANES - MLA_NOPE - MLA_ROPE)], axis=1)
    w1 = jnp.concatenate([w_in[:, :c1], ka, kb, w_in[:, c2:]], axis=1).astype(BF16)

    qd = MLA_NOPE + MLA_ROPE
    wq = w_uq.reshape(MLA_Q_RANK, MLA_HEADS, qd)
    zq = lambda n: jnp.zeros((MLA_Q_RANK, MLA_HEADS, n), w_uq.dtype)
    qa = jnp.concatenate([wq, zq(HEAD_BLOCK - qd)], axis=2)
    qb = jnp.concatenate([zq(MLA_NOPE), wq[:, :, MLA_NOPE + hr:], wq[:, :, MLA_NOPE:MLA_NOPE + hr],
                          zq(HEAD_BLOCK - qd)], axis=2)
    wq2 = jnp.concatenate([qa.reshape(MLA_Q_RANK, -1), qb.reshape(MLA_Q_RANK, -1)], axis=1).astype(BF16)

    wkv = w_ukv.reshape(MLA_KV_RANK, MLA_HEADS, MLA_NOPE + MLA_V)
    kpart = jnp.concatenate([wkv[:, :, :MLA_NOPE],
                             jnp.zeros((MLA_KV_RANK, MLA_HEADS, HEAD_BLOCK - MLA_NOPE), w_ukv.dtype)], axis=2)
    vpart = jnp.concatenate([wkv[:, :, MLA_NOPE:],
                             jnp.zeros((MLA_KV_RANK, MLA_HEADS, HEAD_BLOCK - MLA_V), w_ukv.dtype)], axis=2)
    wkv2 = jnp.concatenate([kpart.reshape(MLA_KV_RANK, -1), vpart.reshape(MLA_KV_RANK, -1)], axis=1).astype(BF16)
    return w1, wq2, wkv2


def _rope_lane_tables(s):
    hr = MLA_ROPE // 2
    inv = ROPE_BASE ** (-jnp.arange(0, MLA_ROPE, 2, dtype=F32) / MLA_ROPE)
    ang = jnp.arange(s, dtype=F32)[:, None] * inv[None, :]
    cos, sin = jnp.cos(ang), jnp.sin(ang)
    pad = jnp.zeros((s, HEAD_BLOCK - MLA_NOPE - MLA_ROPE), F32)
    ct = jnp.concatenate([jnp.ones((s, MLA_NOPE), F32), cos, cos, pad], axis=1)
    st = jnp.concatenate([jnp.zeros((s, MLA_NOPE), F32), -sin, sin, pad], axis=1)
    assert ct.shape[1] == HEAD_BLOCK and hr * 2 == MLA_ROPE
    return ct, st


def _l1_proj(x, attn_norm, w1, q_norm, wq2, kv_norm, wkv2, ct, st, nb_pad, tm=512):
    b, s, d = x.shape
    width = MLA_HEADS * HEAD_BLOCK
    const = lambda shape: pl.BlockSpec(shape, lambda i, j: (0, 0))
    tok = lambda n: pl.BlockSpec((1, tm, n), lambda i, j: (i, j, 0))
    return pl.pallas_call(
        _l1_proj_kernel,
        grid=(b, s // tm),
        in_specs=[tok(d), const((1, d)), const(w1.shape), const((1, MLA_Q_RANK)), const(wq2.shape),
                  const((1, MLA_KV_RANK)), const(wkv2.shape),
                  pl.BlockSpec((tm, HEAD_BLOCK), lambda i, j: (j, 0)),
                  pl.BlockSpec((tm, HEAD_BLOCK), lambda i, j: (j, 0))],
        out_specs=[tok(width), tok(width), tok(width),
                   pl.BlockSpec((tm, S5_CH), lambda i, j: (j, i))],
        out_shape=[jax.ShapeDtypeStruct((b, s, width), BF16),
                   jax.ShapeDtypeStruct((b, s, width), BF16),
                   jax.ShapeDtypeStruct((b, s, width), BF16),
                   jax.ShapeDtypeStruct((s, nb_pad * S5_CH), F32)],
        compiler_params=_cparams(("parallel", "parallel")),
        name="l1_norm_in_proj",
    )(x, attn_norm.reshape(1, d), w1, q_norm.reshape(1, -1), wq2, kv_norm.reshape(1, -1), wkv2, ct, st)


def _mla_kernel(q_ref, k_ref, v_ref, o_ref, *, seq_len, tk, unroll):
    tq = q_ref.shape[1]
    qs = [q_ref[0, :, 0:HEAD_BLOCK], q_ref[0, :, HEAD_BLOCK:]]

    def body(j, carry):
        sl = pl.ds(pl.multiple_of(j * tk, tk), tk)
        new = []
        for hh in range(2):
            m, acc = carry[hh]
            hs = slice(hh * HEAD_BLOCK, (hh + 1) * HEAD_BLOCK)
            s = lax.dot_general(qs[hh], k_ref[0, sl, hs], (((1,), (1,)), ((), ())), preferred_element_type=F32)
            m_new = jnp.maximum(m, jnp.max(s, axis=-1, keepdims=True))
            p = jnp.exp2(s - m_new).astype(BF16)
            acc_new = jnp.exp2(m - m_new) * acc + jnp.dot(p, v_ref[0, sl, hs], preferred_element_type=F32)
            new.append((m_new, acc_new))
        return tuple(new)

    init = tuple((jnp.full((tq, 1), NEG_INF, F32), jnp.zeros((tq, HEAD_BLOCK), F32)) for _ in range(2))
    (_, a0), (_, a1) = lax.fori_loop(0, seq_len // tk, body, init, unroll=unroll)
    lane = lax.broadcasted_iota(jnp.int32, (tq, LANES), 1)
    o0 = a0 / pltpu.roll(a0, MLA_V, 1)
    o1 = a1 / pltpu.roll(a1, MLA_V, 1)
    o_ref[0] = jnp.where(lane < MLA_V, o0, pltpu.roll(o1, MLA_V, 1)).astype(o_ref.dtype)


def _mla_attention(q, k, v, tq=512, tk=1024, unroll=2):
    b, s, _ = q.shape
    hpairs = MLA_HEADS // 2
    pair = 2 * HEAD_BLOCK
    return pl.pallas_call(
        functools.partial(_mla_kernel, seq_len=s, tk=tk, unroll=unroll),
        grid=(b, hpairs, s // tq),
        in_specs=[pl.BlockSpec((1, tq, pair), lambda i, hp, j: (i, j, hp)),
                  pl.BlockSpec((1, s, pair), lambda i, hp, j: (i, 0, hp)),
                  pl.BlockSpec((1, s, pair), lambda i, hp, j: (i, 0, hp))],
        out_specs=pl.BlockSpec((1, tq, LANES), lambda i, hp, j: (i, j, hp)),
        out_shape=jax.ShapeDtypeStruct((b, s, MLA_HEADS * MLA_V), BF16),
        compiler_params=_cparams(("parallel", "parallel", "arbitrary")),
        name="l1_latent_attention",
    )(q, k, v)


def _s5_kernel(u_ref, bd_ref, are_ref, aim_ref, cd_ref, y_ref, bu_ref, x_ref, *, reverse):
    tc, nb, _ = u_ref.shape
    n = S5_NSTATE

    @pl.when(pl.program_id(0) == 0)
    def _():
        x_ref[...] = jnp.zeros_like(x_ref)

    u2 = u_ref[...].reshape(tc * nb, S5_CH).astype(BF16)
    bu_ref[...] = jnp.dot(u2, bd_ref[...], preferred_element_type=F32).reshape(tc, nb, 2 * n)

    lane_blk = 512
    for c0 in range(0, n, lane_blk):
        re = slice(c0, c0 + lane_blk)
        im = slice(n + c0, n + c0 + lane_blk)
        a_re = are_ref[:, re]
        a_im = aim_ref[:, re]

        def step(i, carry):
            t = tc - 1 - i if reverse else i
            xr, xi = carry
            nr = a_re * xr - a_im * xi + bu_ref[t, :, re]
            ni = a_re * xi + a_im * xr + bu_ref[t, :, im]
            bu_ref[t, :, re] = nr
            bu_ref[t, :, im] = ni
            return nr, ni

        xr, xi = lax.fori_loop(0, tc, step, (x_ref[:, re], x_ref[:, im]))
        x_ref[:, re] = xr
        x_ref[:, im] = xi

    xs = bu_ref[...].reshape(tc * nb, 2 * n).astype(BF16)
    y_ref[...] = jnp.dot(xs, cd_ref[...], preferred_element_type=F32).reshape(tc, nb, S5_CH)


def _s5_direction_params(lam_re, lam_im, log_step, b_re, b_im, c_re, c_im, nb):
    lam = lax.complex(jnp.minimum(lam_re.astype(F32), -1e-4), lam_im.astype(F32))
    step = jnp.exp(log_step.astype(F32))[:, None]
    lam_bar = jnp.exp(lam * step)
    bmat = lax.complex(b_re.astype(F32), b_im.astype(F32))
    b_bar = ((lam_bar - 1.0) / lam)[:, :, None] * bmat
    eye = jnp.eye(S5_GROUPS, dtype=F32)
    bd = lambda t: jnp.einsum('gpc,gh->gchp', t, eye).reshape(S5_CH, S5_NSTATE)
    bdense = jnp.concatenate([bd(jnp.real(b_bar)), bd(jnp.imag(b_bar))], axis=1).astype(BF16)
    cdm = lambda t: jnp.einsum('gcp,gh->gphc', t, eye).reshape(S5_NSTATE, S5_CH)
    cdense = jnp.concatenate([cdm(c_re.astype(F32)), -cdm(c_im.astype(F32))], axis=0).astype(BF16)
    a_re = jnp.broadcast_to(jnp.real(lam_bar).reshape(1, S5_NSTATE), (nb, S5_NSTATE)).astype(F32)
    a_im = jnp.broadcast_to(jnp.imag(lam_bar).reshape(1, S5_NSTATE), (nb, S5_NSTATE)).astype(F32)
    return bdense, a_re, a_im, cdense


def _s5_scan(u_tm, params, reverse, tc=64):
    s, nb, _ = u_tm.shape
    bdense, a_re, a_im, cdense = params
    nchunks = s // tc
    tmap = (lambda i: (nchunks - 1 - i, 0, 0)) if reverse else (lambda i: (i, 0, 0))
    const = lambda shape: pl.BlockSpec(shape, lambda i: (0, 0))
    return pl.pallas_call(
        functools.partial(_s5_kernel, reverse=reverse),
        grid=(nchunks,),
        in_specs=[pl.BlockSpec((tc, nb, S5_CH), tmap), const(bdense.shape), const(a_re.shape),
                  const(a_im.shape), const(cdense.shape)],
        out_specs=pl.BlockSpec((tc, nb, S5_CH), tmap),
        out_shape=jax.ShapeDtypeStruct((s, nb, S5_CH), F32),
        scratch_shapes=[pltpu.VMEM((tc, nb, 2 * S5_NSTATE), F32), pltpu.VMEM((nb, 2 * S5_NSTATE), F32)],
        compiler_params=_cparams(("arbitrary",)),
        name="l1_s5_scan_bwd" if reverse else "l1_s5_scan_fwd",
    )(u_tm, bdense, a_re, a_im, cdense)


def _l1_out_router_kernel(x_ref, mla_ref, yf_ref, yb_ref, u_ref, dskip_ref, gw_ref, gb_ref, wo_ref,
                          fn_ref, rt_ref, x1_out, route_out):
    tm = x_ref.shape[1]
    u = u_ref[...]
    y = yf_ref[...] + yb_ref[...] + dskip_ref[...] * u
    z = jax.nn.gelu(y)
    gate = jax.nn.sigmoid(jnp.dot(z.astype(BF16), gw_ref[...], preferred_element_type=F32) + gb_ref[...])
    ssm = (z * gate).astype(BF16)
    half = wo_ref.shape[0] // 2
    x1 = (x_ref[0]
          + jnp.dot(mla_ref[0], wo_ref[0:half, :], preferred_element_type=F32)
          + jnp.dot(ssm, wo_ref[half:, :], preferred_element_type=F32))
    _store_token_slabs(x1_out.at[0], x1, tm)
    hn = _rms(x1, fn_ref[...])

    logits = jnp.dot(hn, rt_ref[...], preferred_element_type=F32, precision=lax.Precision.HIGHEST)
    lane = lax.broadcasted_iota(jnp.int32, logits.shape, 1)
    lg = jnp.where(lane < N_EXPERTS, logits, -jnp.inf)
    m1 = jnp.max(lg, axis=-1, keepdims=True)
    i1 = jnp.min(jnp.where(lg == m1, lane, LANES), axis=-1, keepdims=True)
    lg2 = jnp.where(lane == i1, -jnp.inf, lg)
    m2 = jnp.max(lg2, axis=-1, keepdims=True)
    i2 = jnp.min(jnp.where(lg2 == m2, lane, LANES), axis=-1, keepdims=True)
    e2 = jnp.exp(m2 - m1)
    w1 = 1.0 / (1.0 + e2)
    w2 = e2 / (1.0 + e2)
    route_out[0] = jnp.where(lane == 0, i1.astype(F32),
                             jnp.where(lane == 1, i2.astype(F32),
                                       jnp.where(lane == 2, w1, jnp.where(lane == 3, w2, 0.0))))


def _l1_out_router(x, mla, yf, yb, u_tm, d_skip, glu_w, glu_b, w_out, ffn_norm, router_pad, tm=512):
    b, s, d = x.shape
    tok = lambda n: pl.BlockSpec((1, tm, n), lambda i, j: (i, j, 0))
    tmaj = pl.BlockSpec((tm, S5_CH), lambda i, j: (j, i))
    const = lambda shape: pl.BlockSpec(shape, lambda i, j: (0, 0))
    return pl.pallas_call(
        _l1_out_router_kernel,
        grid=(b, s // tm),
        in_specs=[tok(d), tok(MLA_HEADS * MLA_V), tmaj, tmaj, tmaj, const((1, S5_CH)),
                  const(glu_w.shape), const((1, S5_CH)), const(w_out.shape), const((1, d)),
                  const(router_pad.shape)],
        out_specs=[pl.BlockSpec((1, tm * SLAB_ROWS, LANES), lambda i, j: (i, j, 0)), tok(LANES)],
        out_shape=[jax.ShapeDtypeStruct((b, s * SLAB_ROWS, LANES), F32),
                   jax.ShapeDtypeStruct((b, s, LANES), F32)],
        compiler_params=_cparams(("parallel", "parallel")),
        name="l1_out_proj_router",
    )(x, mla, yf, yb, u_tm, d_skip.reshape(1, -1), glu_w, glu_b.reshape(1, -1), w_out,
      ffn_norm.reshape(1, d), router_pad)


def _start_row_gather(idx_ref, src_hbm, dst, sem, n_rows):
    def issue(r, carry):
        src = pl.multiple_of(idx_ref[r] * SLAB_ROWS, SLAB_ROWS)
        pltpu.make_async_copy(src_hbm.at[pl.ds(src, SLAB_ROWS)],
                              dst.at[pl.ds(pl.multiple_of(r * SLAB_ROWS, SLAB_ROWS), SLAB_ROWS)], sem).start()
        return carry

    lax.fori_loop(0, n_rows, issue, 0, unroll=8)


def _wait_row_gather(src_hbm, dst, sem, n_rows):
    pltpu.make_async_copy(src_hbm.at[pl.ds(0, n_rows * SLAB_ROWS)], dst, sem).wait()


def _expert_ffn_kernel(te_ref, nt_ref, idx0_ref, idxn_ref, x_hbm, gw_ref, g_ref, wg_ref, wu_ref, wd_ref,
                       o_ref, xbuf, sem, hn_ref, acc_ref):
    i = pl.program_id(0)
    j = pl.program_id(1)
    last = pl.num_programs(1) - 1
    tm = hn_ref.shape[0]
    n_used = nt_ref[0]
    slot = i % 2

    @pl.when(jnp.logical_and(i == 0, j == 0))
    def _():
        _start_row_gather(idx0_ref, x_hbm, xbuf.at[0], sem.at[0], tm)

    @pl.when(jnp.logical_and(j == 0, i < n_used))
    def _():
        _wait_row_gather(x_hbm, xbuf.at[slot], sem.at[slot], tm)
        x = _load_token_slabs(xbuf.at[slot], tm)
        hn_ref[...] = _rms(x, g_ref[...]).astype(BF16)
        acc_ref[...] = jnp.zeros_like(acc_ref)

    @pl.when(jnp.logical_and(j == 0, i + 1 < n_used))
    def _():
        _start_row_gather(idxn_ref, x_hbm, xbuf.at[1 - slot], sem.at[1 - slot], tm)

    @pl.when(i < n_used)
    def _():
        h = hn_ref[...]
        a = jnp.dot(h, wg_ref[0], preferred_element_type=F32)
        u = jnp.dot(h, wu_ref[0], preferred_element_type=F32)
        act = (a * jax.nn.sigmoid(a) * u).astype(BF16)
        acc_ref[...] += jnp.dot(act, wd_ref[0], preferred_element_type=F32)

        @pl.when(j == last)
        def _():
            _store_token_slabs(o_ref, gw_ref[...] * acc_ref[...], tm)

    @pl.when(jnp.logical_and(i >= n_used, j == last))
    def _():
        o_ref[...] = jnp.zeros_like(o_ref)


def _expert_ffn(x_slabs, src_tok, gate_w, tile_expert, n_tiles_used, ffn_norm, wg, wu, wd, tm, tf=896):
    a_pad = src_tok.shape[0]
    n_tiles = a_pad // tm
    d = D_MODEL
    f = wg.shape[2]
    grid_spec = pltpu.PrefetchScalarGridSpec(
        num_scalar_prefetch=2,
        grid=(n_tiles, f // tf),
        in_specs=[
            pl.BlockSpec((tm,), lambda i, j, te, nt: (0,), memory_space=pltpu.SMEM),
            pl.BlockSpec((tm,), lambda i, j, te, nt: (jnp.minimum(i + 1, n_tiles - 1),), memory_space=pltpu.SMEM),
            pl.BlockSpec(memory_space=pl.ANY),
            pl.BlockSpec((tm, 1), lambda i, j, te, nt: (i, 0)),
            pl.BlockSpec((1, d), lambda i, j, te, nt: (0, 0)),
            pl.BlockSpec((1, d, tf), lambda i, j, te, nt: (te[i], 0, j)),
            pl.BlockSpec((1, d, tf), lambda i, j, te, nt: (te[i], 0, j)),
            pl.BlockSpec((1, tf, d), lambda i, j, te, nt: (te[i], j, 0)),
        ],
        out_specs=pl.BlockSpec((tm * SLAB_ROWS, LANES), lambda i, j, te, nt: (i, 0)),
        scratch_shapes=[pltpu.VMEM((2, tm * SLAB_ROWS, LANES), F32), pltpu.SemaphoreType.DMA((2,)),
                        pltpu.VMEM((tm, d), BF16), pltpu.VMEM((tm, d), F32)],
    )
    return pl.pallas_call(
        _expert_ffn_kernel,
        grid_spec=grid_spec,
        out_shape=jax.ShapeDtypeStruct((a_pad * SLAB_ROWS, LANES), F32),
        compiler_params=_cparams(("arbitrary", "arbitrary")),
        name="l1_expert_ffn",
    )(tile_expert, n_tiles_used, src_tok, src_tok, x_slabs, gate_w, ffn_norm.reshape(1, d), wg, wu, wd)


def _combine_norm_kernel(ia_ref, ib_ref, x_ref, ys_hbm, g_ref, o_ref, buf, sem):
    tm = o_ref.shape[0]
    _start_row_gather(ia_ref, ys_hbm, buf.at[0], sem.at[0], tm)
    _start_row_gather(ib_ref, ys_hbm, buf.at[1], sem.at[1], tm)
    _wait_row_gather(ys_hbm, buf.at[0], sem.at[0], tm)
    _wait_row_gather(ys_hbm, buf.at[1], sem.at[1], tm)
    y = _load_token_slabs(x_ref, tm) + (_load_token_slabs(buf.at[0], tm) + _load_token_slabs(buf.at[1], tm))
    o_ref[...] = _rms(y, g_ref[...])


def _combine_norm(x_slabs, ys_slabs, dest, final_norm, tm=256):
    t = x_slabs.shape[0] // SLAB_ROWS
    nblk = t // tm
    slab = (tm * SLAB_ROWS, LANES)
    return pl.pallas_call(
        _combine_norm_kernel,
        grid=(nblk,),
        in_specs=[pl.BlockSpec((tm,), lambda i: (i,), memory_space=pltpu.SMEM),
                  pl.BlockSpec((tm,), lambda i: (i + nblk,), memory_space=pltpu.SMEM),
                  pl.BlockSpec(slab, lambda i: (i, 0)),
                  pl.BlockSpec(memory_space=pl.ANY),
                  pl.BlockSpec((1, D_MODEL), lambda i: (0, 0))],
        out_specs=pl.BlockSpec((tm, D_MODEL), lambda i: (i, 0)),
        out_shape=jax.ShapeDtypeStruct((t, D_MODEL), F32),
        scratch_shapes=[pltpu.VMEM((2,) + slab, F32), pltpu.SemaphoreType.DMA((2,))],
        compiler_params=_cparams(("arbitrary",)),
        name="l1_combine_final_norm",
    )(dest, dest, x_slabs, ys_slabs, final_norm.reshape(1, D_MODEL))


def _moe(x_slabs, route, ffn_norm, wg, wu, wd, final_norm, tm=512):
    t = x_slabs.shape[0] // SLAB_ROWS
    e_idx = jnp.concatenate([route[:, 0], route[:, 1]]).astype(jnp.int32)
    e_w = jnp.concatenate([route[:, 2], route[:, 3]])
    n_assign = TOP_K * t
    order = jnp.argsort(e_idx, stable=True).astype(jnp.int32)
    inv = jnp.argsort(order).astype(jnp.int32)
    counts = jnp.sum(e_idx[:, None] == jnp.arange(N_EXPERTS, dtype=jnp.int32)[None, :], axis=0).astype(jnp.int32)
    starts = jnp.cumsum(counts) - counts
    padded = ((counts + tm - 1) // tm) * tm
    pad_ends = jnp.cumsum(padded)
    pad_starts = pad_ends - padded
    a_pad = n_assign + N_EXPERTS * tm
    n_tiles = a_pad // tm

    tile_start = jnp.arange(n_tiles, dtype=jnp.int32) * tm
    tile_expert = jnp.minimum(jnp.sum(tile_start[:, None] >= pad_ends[None, :], axis=1), N_EXPERTS - 1).astype(jnp.int32)
    n_tiles_used = (pad_ends[-1] // tm).astype(jnp.int32).reshape(1)

    slot = jnp.arange(a_pad, dtype=jnp.int32)
    slot_e = jnp.repeat(tile_expert, tm)
    within = slot - pad_starts[slot_e]
    valid = within < counts[slot_e]
    src = order[jnp.clip(starts[slot_e] + within, 0, n_assign - 1)]
    src_tok = jnp.where(valid, src % t, 0).astype(jnp.int32)
    gate_w = jnp.where(valid, e_w[src], 0.0).astype(F32).reshape(a_pad, 1)

    ys = _expert_ffn(x_slabs, src_tok, gate_w, tile_expert, n_tiles_used, ffn_norm, wg, wu, wd, tm)
    dest = (pad_starts[e_idx] + inv - starts[e_idx]).astype(jnp.int32)
    return _combine_norm(x_slabs, ys, dest, final_norm)


def _l0_in_weight(w_in):
    hq = RET_HEADS * RET_DK
    hv = RET_HEADS * RET_DV
    hd = DIL_HEADS * DIL_DH
    rq, rk, rv, rg, dq, dk, dv = jnp.split(
        w_in, [hq, 2 * hq, 2 * hq + hv, 2 * hq + 2 * hv, 2 * hq + 2 * hv + hd, 2 * hq + 2 * hv + 2 * hd], axis=1)
    dup = lambda t: jnp.concatenate([t.reshape(-1, RET_HEADS, 1, RET_DK)] * 2, axis=2).reshape(-1, 2 * hq)
    w = jnp.concatenate([dup(rq) * RET_DK ** -0.5, dup(rk), rv, rg, dq * DIL_DH ** -0.5, dk, dv], axis=1)
    assert w.shape[1] == L0_COLS
    return w.astype(BF16)


def _prepare(l0_w_in, l0_ret_decay_f, l0_ret_decay_b, l0_w_out, l0_ffn_w_gate, l0_ffn_w_up, l0_ffn_w_down,
             l1_w_in, l1_mla_w_uq, l1_mla_w_ukv, l1_s5_glu_w, l1_w_out, l1_router,
             l1_exp_w_gate, l1_exp_w_up, l1_exp_w_down):
    w1, wq2, wkv2 = _l1_weights(l1_w_in, l1_mla_w_uq, l1_mla_w_ukv)
    router_pad = jnp.zeros((D_MODEL, LANES), F32).at[:, :N_EXPERTS].set(l1_router.astype(F32))
    return dict(
        l0_w_in=_l0_in_weight(l0_w_in),
        ret_tables=_retention_tables(l0_ret_decay_f, l0_ret_decay_b),
        l0_w_out=l0_w_out.astype(BF16),
        l0_wg=l0_ffn_w_gate.astype(BF16), l0_wu=l0_ffn_w_up.astype(BF16), l0_wd=l0_ffn_w_down.astype(BF16),
        l1_w1=w1, l1_wq2=wq2, l1_wkv2=wkv2,
        glu_w=l1_s5_glu_w.astype(BF16), l1_w_out=l1_w_out.astype(BF16), router_pad=router_pad,
        exp_wg=l1_exp_w_gate.astype(BF16), exp_wu=l1_exp_w_up.astype(BF16), exp_wd=l1_exp_w_down.astype(BF16),
    )


def _trunk(x, prep, p):
    b, s, d = x.shape
    t = b * s
    p0 = _norm_matmul(x.reshape(t, d), p['l0_attn_norm'], prep['l0_w_in']).reshape(b, s, L0_COLS)
    ret = _retention(p0, prep['ret_tables'], p['l0_ret_gn'])
    dil = _dilated_attention(p0)
    x = _l0_out_ffn(x.reshape(t, d), ret.reshape(t, -1), dil.reshape(t, -1), prep['l0_w_out'],
                    p['l0_ffn_norm'], prep['l0_wg'], prep['l0_wu'], prep['l0_wd']).reshape(b, s, d)
    nb = -(-b // SUBLANES) * SUBLANES
    ct, st = _rope_lane_tables(s)
    q, k, v, u_tm = _l1_proj(x, p['l1_attn_norm'], prep['l1_w1'], p['l1_mla_q_norm'], prep['l1_wq2'],
                             p['l1_mla_kv_norm'], prep['l1_wkv2'], ct, st, b)
    mla = _mla_attention(q, k, v)
    u3 = u_tm.reshape(s, b, S5_CH)
    if nb != b:
        u3 = jnp.pad(u3, ((0, 0), (0, nb - b), (0, 0)))
    sf = _s5_direction_params(p['l1_s5_lam_re_f'], p['l1_s5_lam_im_f'], p['l1_s5_log_step_f'],
                              p['l1_s5_b_re'], p['l1_s5_b_im'], p['l1_s5_c_re'], p['l1_s5_c_im'], nb)
    sb = _s5_direction_params(p['l1_s5_lam_re_b'], p['l1_s5_lam_im_b'], p['l1_s5_log_step_b'],
                              p['l1_s5_b_re'], p['l1_s5_b_im'], p['l1_s5_c_re'], p['l1_s5_c_im'], nb)
    yf = _s5_scan(u3, sf, reverse=False).reshape(s, nb * S5_CH)
    yb = _s5_scan(u3, sb, reverse=True).reshape(s, nb * S5_CH)
    x1s, route = _l1_out_router(x, mla, yf, yb, u_tm, p['l1_s5_d'], prep['glu_w'], p['l1_s5_glu_b'],
                                prep['l1_w_out'], p['l1_ffn_norm'], prep['router_pad'])
    out = _moe(x1s.reshape(t * SLAB_ROWS, LANES), route.reshape(t, LANES), p['l1_ffn_norm'],
               prep['exp_wg'], prep['exp_wu'], prep['exp_wd'], p['final_norm'])
    return out.reshape(b, s, d)


def kernel(x_prompt, x_sample, l0_attn_norm, l0_w_in, l0_ret_decay_f, l0_ret_decay_b, l0_ret_gn, l0_w_out,
           l0_ffn_norm, l0_ffn_w_gate, l0_ffn_w_up, l0_ffn_w_down, l1_attn_norm, l1_w_in, l1_mla_q_norm,
           l1_mla_w_uq, l1_mla_kv_norm, l1_mla_w_ukv, l1_s5_lam_re_f, l1_s5_lam_im_f, l1_s5_log_step_f,
           l1_s5_lam_re_b, l1_s5_lam_im_b, l1_s5_log_step_b, l1_s5_b_re, l1_s5_b_im, l1_s5_c_re, l1_s5_c_im,
           l1_s5_d, l1_s5_glu_w, l1_s5_glu_b, l1_w_out, l1_ffn_norm, l1_router, l1_exp_w_gate, l1_exp_w_up,
           l1_exp_w_down, final_norm):
    p = dict(locals())
    prep = _prepare(l0_w_in, l0_ret_decay_f, l0_ret_decay_b, l0_w_out, l0_ffn_w_gate, l0_ffn_w_up,
                    l0_ffn_w_down, l1_w_in, l1_mla_w_uq, l1_mla_w_ukv, l1_s5_glu_w, l1_w_out, l1_router,
                    l1_exp_w_gate, l1_exp_w_up, l1_exp_w_down)
    return (_trunk(x_prompt, prep, p), _trunk(x_sample, prep, p))
```

```python
import functools
import math

import jax
import jax.numpy as jnp
from jax import lax
from jax.experimental import pallas as pl
from jax.experimental.pallas import tpu as pltpu

F32 = jnp.float32
BF16 = jnp.bfloat16

D_MODEL = 1024
EPS = 1e-6
NEG_INF = -1e30
RET_HEADS = 4
RET_DK = 64
RET_DV = 128
RET_CHUNK = 128
RET_UNROLL = 4
DIL_HEADS = 8
DIL_DH = 64
DIL_PATTERNS = ((128, 1), (512, 4), (2048, 16))
DIL_QBLOCK = 128
DIL_UNROLL = 4
MLA_HEADS = 8
MLA_Q_RANK = 256
MLA_KV_RANK = 128
MLA_NOPE = 64
MLA_ROPE = 32
MLA_V = 64
ROPE_BASE = 10000.0
S5_GROUPS = 32
S5_GROUP_CH = 16
S5_STATE = 64
S5_CH = S5_GROUPS * S5_GROUP_CH
S5_NSTATE = S5_GROUPS * S5_STATE
N_EXPERTS = 8
TOP_K = 2

LANES = 128
SUBLANES = 8
VMEM_LIMIT = 56 * 1024 * 1024

L0_COLS = 3584
L0_BLOCKS = L0_COLS // LANES
L0_RQ, L0_RK, L0_RV, L0_RG, L0_DQ, L0_DK, L0_DV = 0, 4, 8, 12, 16, 20, 24


def _cparams(sem):
    return pltpu.CompilerParams(dimension_semantics=sem, vmem_limit_bytes=VMEM_LIMIT)


def _rms(x, g):
    return x * lax.rsqrt(jnp.mean(x * x, axis=-1, keepdims=True) + EPS) * g


SLAB_ROWS = D_MODEL // LANES


def _store_token_slabs(ref2d, x, n_tok):
    for s in range(SLAB_ROWS):
        ref2d[pl.ds(s, n_tok, stride=SLAB_ROWS), :] = x[:, s * LANES:(s + 1) * LANES]


def _load_token_slabs(ref2d, n_tok):
    return jnp.concatenate([ref2d[pl.ds(s, n_tok, stride=SLAB_ROWS), :] for s in range(SLAB_ROWS)], axis=1)


DIL_COLS = 3 * DIL_HEADS * DIL_DH
DIL_COL0 = L0_DQ * LANES
DIL_STRIDED = tuple(dil for _, dil in DIL_PATTERNS if dil > 1)


def _l0_in_proj_kernel(x_ref, g_ref, w_ref, o_ref, *rest):
    dil_refs, dsc = rest[:-1], rest[-1]
    tm = x_ref.shape[0]
    xn = _rms(x_ref[...], g_ref[...]).astype(BF16)
    res = jnp.dot(xn, w_ref[...], preferred_element_type=F32)
    o_ref[...] = res.astype(o_ref.dtype)
    for c in range(DIL_COLS // LANES):
        dsc[c] = res[:, DIL_COL0 + c * LANES:DIL_COL0 + (c + 1) * LANES]
    for ref, dil in zip(dil_refs, DIL_STRIDED):
        for r in range(dil):
            for c in range(DIL_COLS // LANES):
                lo = r * DIL_COLS + c * LANES
                ref[:, lo:lo + LANES] = dsc[c, pl.ds(r, tm // dil, stride=dil), :].astype(ref.dtype)


def _l0_in_proj(x2d, gain, w, tm=512):
    t, d = x2d.shape
    n = w.shape[1]
    dil_specs = [pl.BlockSpec((tm // dil, dil * DIL_COLS), lambda i: (i, 0)) for dil in DIL_STRIDED]
    dil_shapes = [jax.ShapeDtypeStruct((t // dil, dil * DIL_COLS), BF16) for dil in DIL_STRIDED]
    return pl.pallas_call(
        _l0_in_proj_kernel,
        grid=(t // tm,),
        in_specs=[
            pl.BlockSpec((tm, d), lambda i: (i, 0)),
            pl.BlockSpec((1, d), lambda i: (0, 0)),
            pl.BlockSpec((d, n), lambda i: (0, 0)),
        ],
        out_specs=[pl.BlockSpec((tm, n), lambda i: (i, 0))] + dil_specs,
        out_shape=[jax.ShapeDtypeStruct((t, n), BF16)] + dil_shapes,
        scratch_shapes=[pltpu.VMEM((DIL_COLS // LANES, tm, LANES), F32)],
        compiler_params=_cparams(("parallel",)),
        name="l0_norm_in_proj",
    )(x2d, gain.reshape(1, d), w)


def _retention_kernel(q_ref, k_ref, v_ref, g_ref, d_ref, qw_ref, kw_ref, cd_ref, gn_ref,
                      o_ref, kv_ref, p_ref, *, n_chunks):
    c = RET_CHUNK
    kw = kw_ref[0]
    qw = qw_ref[0]
    dmat = d_ref[0]
    gn = gn_ref[...]

    def kv_body(n, carry):
        sl = pl.ds(pl.multiple_of(n * c, c), c)
        kc = (k_ref[0, sl, :].astype(F32) * kw).T.astype(BF16)
        kv_ref[n] = jnp.dot(kc, v_ref[0, sl, :], preferred_element_type=F32)
        return carry

    lax.fori_loop(0, n_chunks, kv_body, 0, unroll=RET_UNROLL)

    half = c // 2
    dec_f = cd_ref[0, 0:half, :]
    dec_b = cd_ref[0, half:c, :]

    def fwd_body(n, s):
        p_ref[n, 0:half, :] = s.astype(BF16)
        return s * dec_f + kv_ref[n, 0:half, :]

    lax.fori_loop(0, n_chunks, fwd_body, jnp.zeros((half, RET_DV), F32))

    def bwd_body(i, s):
        n = n_chunks - 1 - i
        p_ref[n, half:c, :] = s.astype(BF16)
        return s * dec_b + kv_ref[n, half:c, :]

    lax.fori_loop(0, n_chunks, bwd_body, jnp.zeros((half, RET_DV), F32))

    def out_body(n, carry):
        sl = pl.ds(pl.multiple_of(n * c, c), c)
        qc = q_ref[0, sl, :]
        s = lax.dot_general(qc, k_ref[0, sl, :], (((1,), (1,)), ((), ())), preferred_element_type=F32)
        intra = jnp.dot((s * dmat).astype(BF16), v_ref[0, sl, :], preferred_element_type=F32)
        qq = (qc.astype(F32) * qw).astype(BF16)
        y = intra + jnp.dot(qq, p_ref[n], preferred_element_type=F32)
        mu = jnp.mean(y, axis=-1, keepdims=True)
        yc = y - mu
        var = jnp.mean(yc * yc, axis=-1, keepdims=True)
        yn = yc * lax.rsqrt(var + EPS) * gn
        gg = g_ref[0, sl, :].astype(F32)
        o_ref[0, sl, :] = (gg * jax.nn.sigmoid(gg) * yn).astype(o_ref.dtype)
        return carry

    lax.fori_loop(0, n_chunks, out_body, 0, unroll=RET_UNROLL)


def _retention_tables(decay_f, decay_b):
    c = RET_CHUNK
    lg_f = jax.nn.log_sigmoid(decay_f.astype(F32))[:, None, None]
    lg_b = jax.nn.log_sigmoid(decay_b.astype(F32))[:, None, None]
    j = jnp.arange(c, dtype=F32)
    diff = j[:, None] - j[None, :]
    dmat = 0.5 * jnp.where(diff >= 0, jnp.exp(lg_f * jnp.maximum(diff, 0.0)),
                           jnp.exp(lg_b * jnp.maximum(-diff, 0.0)))
    lane_f = (jnp.arange(LANES) < RET_DK)[None, None, :]
    jj = j[None, :, None]
    qw = jnp.where(lane_f, jnp.exp(lg_f * (jj + 1.0)), jnp.exp(lg_b * (c - jj)))
    kw = jnp.where(lane_f, jnp.exp(lg_f * (c - 1.0 - jj)), jnp.exp(lg_b * jj))
    row_f = (jnp.arange(c) < c // 2)[None, :, None]
    cd = jnp.where(row_f, jnp.exp(lg_f * c), jnp.exp(lg_b * c)) * jnp.ones((1, 1, RET_DV), F32)
    return dmat.astype(F32), qw.astype(F32), kw.astype(F32), cd.astype(F32)


def _retention(p0, tables, gn):
    b, s, _ = p0.shape
    dmat, qw, kw, cd = tables
    n_chunks = s // RET_CHUNK
    seq = lambda col: pl.BlockSpec((1, s, LANES), lambda i, h: (i, 0, col + h))
    tab = pl.BlockSpec((1, RET_CHUNK, LANES), lambda i, h: (h, 0, 0))
    return pl.pallas_call(
        functools.partial(_retention_kernel, n_chunks=n_chunks),
        grid=(b, RET_HEADS),
        in_specs=[seq(L0_RQ), seq(L0_RK), seq(L0_RV), seq(L0_RG), tab, tab, tab, tab,
                  pl.BlockSpec((1, LANES), lambda i, h: (0, h))],
        out_specs=pl.BlockSpec((1, s, LANES), lambda i, h: (i, 0, h)),
        out_shape=jax.ShapeDtypeStruct((b, s, RET_HEADS * RET_DV), BF16),
        scratch_shapes=[pltpu.VMEM((n_chunks, RET_CHUNK, RET_DV), F32),
                        pltpu.VMEM((n_chunks, RET_CHUNK, RET_DV), BF16)],
        compiler_params=_cparams(("parallel", "parallel")),
        name="l0_retention",
    )(p0, p0, p0, p0, dmat, qw, kw, cd, gn.reshape(1, -1))


def _dilated_kernel(*refs, seq_len, half, q_rows, n_hp, has_prev, final):
    q_ref, k_ref, v_ref, slope_ref = refs[:4]
    refs = refs[4:]
    ratio = DIL_CHAIN_RATIO
    if has_prev:
        prev_o, prev_l, refs = refs[:ratio], refs[ratio:2 * ratio], refs[2 * ratio:]
        po_ref, pl_ref = refs[-2:]
        refs = refs[:-2]
        for m in range(ratio):
            for hp in range(n_hp):
                ls = slice(hp * LANES, (hp + 1) * LANES)
                po_ref[hp, pl.ds(m, q_rows // ratio, stride=ratio), :] = prev_o[m][0, :, ls]
                pl_ref[hp, pl.ds(m, q_rows // ratio, stride=ratio), :] = prev_l[m][0, :, ls]
    o_ref = refs[0]
    lse_ref = None if final else refs[1]
    tq = DIL_QBLOCK
    kw = tq + 2 * half
    hp0 = pl.program_id(2) * n_hp
    base = pl.program_id(3) * q_rows
    lane = lax.broadcasted_iota(jnp.int32, (tq, LANES), 1)
    lo = lane < DIL_DH
    rq = lax.broadcasted_iota(jnp.int32, (tq, kw), 0)
    rk = lax.broadcasted_iota(jnp.int32, (tq, kw), 1)

    def blk(i, carry):
        off = pl.multiple_of(i * tq, tq)
        rows = pl.ds(off, tq)
        qs = base + off
        ks = pl.multiple_of(jnp.clip(qs - half, 0, seq_len - kw), half)
        dist = jnp.abs((qs + rq) - (ks + rk))
        valid = dist <= half
        distf = dist.astype(F32)
        for hp in range(n_hp):
            ls = slice(hp * LANES, (hp + 1) * LANES)
            q = q_ref[0, rows, ls]
            kwin = k_ref[0, pl.ds(ks, kw), ls]
            vwin = v_ref[0, pl.ds(ks, kw), ls]
            outs, lses = [], []
            for hh in range(2):
                qh = jnp.where(lo if hh == 0 else jnp.logical_not(lo), q, jnp.zeros_like(q))
                s = lax.dot_general(qh, kwin, (((1,), (1,)), ((), ())), preferred_element_type=F32)
                s = s - slope_ref[hp0 + hp, hh:hh + 1, 0:1] * distf
                s = jnp.where(valid, s, NEG_INF)
                m = jnp.max(s, axis=-1, keepdims=True)
                p = jnp.exp(s - m)
                den = jnp.sum(p, axis=-1, keepdims=True)
                outs.append(jnp.dot(p.astype(BF16), vwin, preferred_element_type=F32) / den)
                lses.append(m + jnp.log(den))
            o = jnp.where(lo, outs[0], outs[1])
            lse = jnp.where(lo, lses[0], lses[1])
            if has_prev:
                lse_p = pl_ref[hp, rows, :]
                top = jnp.maximum(lse_p, lse)
                wa = jnp.exp(lse_p - top)
                wb = jnp.exp(lse - top)
                den = wa + wb
                o = (wa * po_ref[hp, rows, :] + wb * o) / den
                lse = top + jnp.log(den)
            o_ref[0, rows, ls] = o.astype(o_ref.dtype)
            if not final:
                lse_ref[0, rows, ls] = lse
        return carry

    n_blk = q_rows // tq
    lax.fori_loop(0, n_blk, blk, 0, unroll=max(1, min(DIL_UNROLL // n_hp, n_blk)))


DIL_MAX_RESIDENT_ROWS = 2048


DIL_CHAIN_RATIO = 4
assert all(DIL_PATTERNS[n + 1][1] == DIL_CHAIN_RATIO * DIL_PATTERNS[n][1] for n in range(len(DIL_PATTERNS) - 1))


def _dilated_pattern(src, col0, batch, window, dil, prev, final):
    width = DIL_HEADS * DIL_DH
    seq_len = src.shape[1]
    group_blocks = src.shape[2] // dil // LANES
    half = window // (2 * dil)
    q_rows = min(seq_len, 1024)
    hpairs = DIL_HEADS // 2
    n_hp = hpairs if seq_len <= DIL_MAX_RESIDENT_ROWS else 1
    lanes = n_hp * LANES
    hp_blocks = hpairs // n_hp
    c0 = col0 // LANES
    slopes = jnp.exp2(-8.0 * jnp.arange(1, DIL_HEADS + 1, dtype=F32) / DIL_HEADS) * dil
    slope_tab = jnp.zeros((hpairs, SUBLANES, LANES), F32).at[:, 0:2, :].set(
        jnp.broadcast_to(slopes.reshape(hpairs, 2, 1), (hpairs, 2, LANES)))
    col = lambda part: (lambda i, r, hp, j: (i, 0, (r * group_blocks + c0 + part * hpairs) // n_hp + hp))
    qcol = lambda i, r, hp, j: (i, j, (r * group_blocks + c0) // n_hp + hp)
    oblk = pl.BlockSpec((1, q_rows, lanes), lambda i, r, hp, j: (i, j, r * hp_blocks + hp))
    in_specs = [pl.BlockSpec((1, q_rows, lanes), qcol),
                pl.BlockSpec((1, seq_len, lanes), col(1)),
                pl.BlockSpec((1, seq_len, lanes), col(2)),
                pl.BlockSpec((hpairs, SUBLANES, LANES), lambda i, r, hp, j: (0, 0, 0))]
    args = [src, src, src, slope_tab]
    scratch = []
    if prev is not None:
        ratio = DIL_CHAIN_RATIO
        pblk = lambda m: pl.BlockSpec((1, q_rows // ratio, lanes),
                                      lambda i, r, hp, j: (i, j, (m * dil + r) * hp_blocks + hp))
        for a in prev:
            in_specs += [pblk(m) for m in range(ratio)]
            args += [a] * ratio
        scratch = [pltpu.VMEM((n_hp, q_rows, LANES), F32)] * 2
    oshape = (batch, seq_len, dil * width)
    if final:
        out_specs, out_shape = oblk, jax.ShapeDtypeStruct(oshape, BF16)
    else:
        out_specs, out_shape = [oblk, oblk], [jax.ShapeDtypeStruct(oshape, F32)] * 2
    return pl.pallas_call(
        functools.partial(_dilated_kernel, seq_len=seq_len, half=half, q_rows=q_rows, n_hp=n_hp,
                          has_prev=prev is not None, final=final),
        grid=(batch, dil, hp_blocks, seq_len // q_rows),
        in_specs=in_specs,
        out_specs=out_specs,
        out_shape=out_shape,
        scratch_shapes=scratch,
        compiler_params=_cparams(("parallel", "parallel", "parallel", "arbitrary")),
        name=f"l0_dilated_d{dil}",
    )(*args)


def _dilated_attention(p0, regrouped):
    b, s, _ = p0.shape
    by_dil = dict(zip(DIL_STRIDED, regrouped))
    prev = None
    for n, (window, dil) in enumerate(reversed(DIL_PATTERNS)):
        if dil == 1:
            src, col0 = p0, DIL_COL0
        else:
            src, col0 = by_dil[dil].reshape(b, s // dil, dil * DIL_COLS), 0
        prev = _dilated_pattern(src, col0, b, window, dil, prev, final=n == len(DIL_PATTERNS) - 1)
    return prev


def _l0_out_ffn_kernel(x_ref, ret_ref, dil_ref, wo_ref, g_ref, wg_ref, wu_ref, wd_ref, o_ref,
                       x1_ref, hn_ref, acc_ref):
    j = pl.program_id(1)

    @pl.when(j == 0)
    def _():
        half = wo_ref.shape[0] // 2
        x1 = (x_ref[...]
              + jnp.dot(ret_ref[...], wo_ref[0:half, :], preferred_element_type=F32)
              + jnp.dot(dil_ref[...], wo_ref[half:, :], preferred_element_type=F32))
        x1_ref[...] = x1
        hn_ref[...] = _rms(x1, g_ref[...]).astype(BF16)
        acc_ref[...] = jnp.zeros_like(acc_ref)

    h = hn_ref[...]
    a = jnp.dot(h, wg_ref[...], preferred_element_type=F32)
    u = jnp.dot(h, wu_ref[...], preferred_element_type=F32)
    act = (a * jax.nn.sigmoid(a) * u).astype(BF16)
    acc_ref[...] += jnp.dot(act, wd_ref[...], preferred_element_type=F32)

    @pl.when(j == pl.num_programs(1) - 1)
    def _():
        o_ref[...] = x1_ref[...] + acc_ref[...]


def _l0_out_ffn(x2d, ret2d, dil2d, w_out, ffn_norm, w_gate, w_up, w_down, tm=512):
    t, d = x2d.shape
    f = w_gate.shape[1]
    tf = f // 2 if (f // 2) % LANES == 0 else f
    half = w_out.shape[0] // 2
    return pl.pallas_call(
        _l0_out_ffn_kernel,
        grid=(t // tm, f // tf),
        in_specs=[
            pl.BlockSpec((tm, d), lambda i, j: (i, 0)),
            pl.BlockSpec((tm, half), lambda i, j: (i, 0)),
            pl.BlockSpec((tm, half), lambda i, j: (i, 0)),
            pl.BlockSpec((2 * half, d), lambda i, j: (0, 0)),
            pl.BlockSpec((1, d), lambda i, j: (0, 0)),
            pl.BlockSpec((d, tf), lambda i, j: (0, j)),
            pl.BlockSpec((d, tf), lambda i, j: (0, j)),
            pl.BlockSpec((tf, d), lambda i, j: (j, 0)),
        ],
        out_specs=pl.BlockSpec((tm, d), lambda i, j: (i, 0)),
        out_shape=jax.ShapeDtypeStruct((t, d), F32),
        scratch_shapes=[pltpu.VMEM((tm, d), F32), pltpu.VMEM((tm, d), BF16), pltpu.VMEM((tm, d), F32)],
        compiler_params=_cparams(("parallel", "arbitrary")),
        name="l0_out_proj_ffn",
    )(x2d, ret2d, dil2d, w_out, ffn_norm.reshape(1, d), w_gate, w_up, w_down)


L1_W1_COLS = MLA_Q_RANK + MLA_KV_RANK + 2 * LANES + S5_CH
HEAD_BLOCK = LANES


def _l1_proj_kernel(x_ref, an_ref, w1_ref, qn_ref, wq_ref, kvn_ref, wkv_ref, ct_ref, st_ref,
                    q_out, k_out, v_out, u_out):
    xn = _rms(x_ref[0], an_ref[...]).astype(BF16)
    proj = jnp.dot(xn, w1_ref[...], preferred_element_type=F32)
    c0 = MLA_Q_RANK
    c1 = c0 + MLA_KV_RANK
    cq = proj[:, 0:c0]
    ckv = proj[:, c0:c1]
    ka = proj[:, c1:c1 + LANES]
    kb = proj[:, c1 + LANES:c1 + 2 * LANES]
    u_out[...] = proj[:, c1 + 2 * LANES:]

    ct = ct_ref[...]
    st = st_ref[...]
    ct8 = jnp.concatenate([ct] * MLA_HEADS, axis=1)
    st8 = jnp.concatenate([st] * MLA_HEADS, axis=1)
    width = MLA_HEADS * HEAD_BLOCK

    q2 = jnp.dot(_rms(cq, qn_ref[...]).astype(BF16), wq_ref[...], preferred_element_type=F32)
    scale = (MLA_NOPE + MLA_ROPE) ** -0.5 * math.log2(math.e)
    q_out[0] = ((q2[:, 0:width] * ct8 + q2[:, width:] * st8) * scale).astype(q_out.dtype)

    kv = jnp.dot(_rms(ckv, kvn_ref[...]).astype(BF16), wkv_ref[...], preferred_element_type=F32)
    krot = ka * ct + kb * st
    k_out[0] = (kv[:, 0:width] + jnp.concatenate([krot] * MLA_HEADS, axis=1)).astype(k_out.dtype)
    lane = lax.broadcasted_iota(jnp.int32, (1, width), 1)
    ones = jnp.where(lane % HEAD_BLOCK >= MLA_V, 1.0, 0.0)
    v_out[0] = (kv[:, width:] + ones).astype(v_out.dtype)


def _l1_weights(w_in, w_uq, w_ukv):
    c0 = MLA_Q_RANK
    c1 = c0 + MLA_KV_RANK
    c2 = c1 + MLA_ROPE
    hr = MLA_ROPE // 2
    kpe = w_in[:, c1:c2]
    zeros = lambda n: jnp.zeros((w_in.shape[0], n), w_in.dtype)
    ka = jnp.concatenate([zeros(MLA_NOPE), kpe, zeros(LANES - MLA_NOPE - MLA_ROPE)], axis=1)
    kb = jnp.concatenate([zeros(MLA_NOPE), kpe[:, hr:], kpe[:, :hr], zeros(LANES - MLA_NOPE - MLA_ROPE)], axis=1)
    w1 = jnp.concatenate([w_in[:, :c1], ka, kb, w_in[:, c2:]], axis=1).astype(BF16)

    qd = MLA_NOPE + MLA_ROPE
    wq = w_uq.reshape(MLA_Q_RANK, MLA_HEADS, qd)
    zq = lambda n: jnp.zeros((MLA_Q_RANK, MLA_HEADS, n), w_uq.dtype)
    qa = jnp.concatenate([wq, zq(HEAD_BLOCK - qd)], axis=2)
    qb = jnp.concatenate([zq(MLA_NOPE), wq[:, :, MLA_NOPE + hr:], wq[:, :, MLA_NOPE:MLA_NOPE + hr],
                          zq(HEAD_BLOCK - qd)], axis=2)
    wq2 = jnp.concatenate([qa.reshape(MLA_Q_RANK, -1), qb.reshape(MLA_Q_RANK, -1)], axis=1).astype(BF16)

    wkv = w_ukv.reshape(MLA_KV_RANK, MLA_HEADS, MLA_NOPE + MLA_V)
    kpart = jnp.concatenate([wkv[:, :, :MLA_NOPE],
                             jnp.zeros((MLA_KV_RANK, MLA_HEADS, HEAD_BLOCK - MLA_NOPE), w_ukv.dtype)], axis=2)
    vpart = jnp.concatenate([wkv[:, :, MLA_NOPE:],
                             jnp.zeros((MLA_KV_RANK, MLA_HEADS, HEAD_BLOCK - MLA_V), w_ukv.dtype)], axis=2)
    wkv2 = jnp.concatenate([kpart.reshape(MLA_KV_RANK, -1), vpart.reshape(MLA_KV_RANK, -1)], axis=1).astype(BF16)
    return w1, wq2, wkv2


def _rope_lane_tables(s):
    hr = MLA_ROPE // 2
    inv = ROPE_BASE ** (-jnp.arange(0, MLA_ROPE, 2, dtype=F32) / MLA_ROPE)
    ang = jnp.arange(s, dtype=F32)[:, None] * inv[None, :]
    cos, sin = jnp.cos(ang), jnp.sin(ang)
    pad = jnp.zeros((s, HEAD_BLOCK - MLA_NOPE - MLA_ROPE), F32)
    ct = jnp.concatenate([jnp.ones((s, MLA_NOPE), F32), cos, cos, pad], axis=1)
    st = jnp.concatenate([jnp.zeros((s, MLA_NOPE), F32), -sin, sin, pad], axis=1)
    assert ct.shape[1] == HEAD_BLOCK and hr * 2 == MLA_ROPE
    return ct, st


def _l1_proj(x, attn_norm, w1, q_norm, wq2, kv_norm, wkv2, ct, st, nb_pad, tm=512):
    b, s, d = x.shape
    width = MLA_HEADS * HEAD_BLOCK
    const = lambda shape: pl.BlockSpec(shape, lambda i, j: (0, 0))
    tok = lambda n: pl.BlockSpec((1, tm, n), lambda i, j: (i, j, 0))
    return pl.pallas_call(
        _l1_proj_kernel,
        grid=(b, s // tm),
        in_specs=[tok(d), const((1, d)), const(w1.shape), const((1, MLA_Q_RANK)), const(wq2.shape),
                  const((1, MLA_KV_RANK)), const(wkv2.shape),
                  pl.BlockSpec((tm, HEAD_BLOCK), lambda i, j: (j, 0)),
                  pl.BlockSpec((tm, HEAD_BLOCK), lambda i, j: (j, 0))],
        out_specs=[tok(width), tok(width), tok(width),
                   pl.BlockSpec((tm, S5_CH), lambda i, j: (j, i))],
        out_shape=[jax.ShapeDtypeStruct((b, s, width), BF16),
                   jax.ShapeDtypeStruct((b, s, width), BF16),
                   jax.ShapeDtypeStruct((b, s, width), BF16),
                   jax.ShapeDtypeStruct((s, nb_pad * S5_CH), F32)],
        compiler_params=_cparams(("parallel", "parallel")),
        name="l1_norm_in_proj",
    )(x, attn_norm.reshape(1, d), w1, q_norm.reshape(1, -1), wq2, kv_norm.reshape(1, -1), wkv2, ct, st)


def _mla_kernel(q_ref, k_ref, v_ref, o_ref, *, seq_len, tk, unroll):
    tq = q_ref.shape[1]
    qs = [q_ref[0, :, 0:HEAD_BLOCK], q_ref[0, :, HEAD_BLOCK:]]

    def body(j, carry):
        sl = pl.ds(pl.multiple_of(j * tk, tk), tk)
        new = []
        for hh in range(2):
            m, acc = carry[hh]
            hs = slice(hh * HEAD_BLOCK, (hh + 1) * HEAD_BLOCK)
            s = lax.dot_general(qs[hh], k_ref[0, sl, hs], (((1,), (1,)), ((), ())), preferred_element_type=F32)
            m_new = jnp.maximum(m, jnp.max(s, axis=-1, keepdims=True))
            p = jnp.exp2(s - m_new).astype(BF16)
            acc_new = jnp.exp2(m - m_new) * acc + jnp.dot(p, v_ref[0, sl, hs], preferred_element_type=F32)
            new.append((m_new, acc_new))
        return tuple(new)

    init = tuple((jnp.full((tq, 1), NEG_INF, F32), jnp.zeros((tq, HEAD_BLOCK), F32)) for _ in range(2))
    (_, a0), (_, a1) = lax.fori_loop(0, seq_len // tk, body, init, unroll=unroll)
    lane = lax.broadcasted_iota(jnp.int32, (tq, LANES), 1)
    o0 = a0 / pltpu.roll(a0, MLA_V, 1)
    o1 = a1 / pltpu.roll(a1, MLA_V, 1)
    o_ref[0] = jnp.where(lane < MLA_V, o0, pltpu.roll(o1, MLA_V, 1)).astype(o_ref.dtype)


def _mla_attention(q, k, v, tq=512, tk=1024, unroll=2):
    b, s, _ = q.shape
    hpairs = MLA_HEADS // 2
    pair = 2 * HEAD_BLOCK
    return pl.pallas_call(
        functools.partial(_mla_kernel, seq_len=s, tk=tk, unroll=unroll),
        grid=(b, hpairs, s // tq),
        in_specs=[pl.BlockSpec((1, tq, pair), lambda i, hp, j: (i, j, hp)),
                  pl.BlockSpec((1, s, pair), lambda i, hp, j: (i, 0, hp)),
                  pl.BlockSpec((1, s, pair), lambda i, hp, j: (i, 0, hp))],
        out_specs=pl.BlockSpec((1, tq, LANES), lambda i, hp, j: (i, j, hp)),
        out_shape=jax.ShapeDtypeStruct((b, s, MLA_HEADS * MLA_V), BF16),
        compiler_params=_cparams(("parallel", "parallel", "arbitrary")),
        name="l1_latent_attention",
    )(q, k, v)


S5_BLOCKS = S5_CH // LANES
S5_BLOCK_STATES = S5_NSTATE // S5_BLOCKS


def _s5_kernel(u_ref, bb_ref, are_ref, aim_ref, cb_ref, y_ref, bu_ref, x_ref, *, reverse):
    tc, nb, _ = u_ref.shape
    n = S5_NSTATE
    sb = S5_BLOCK_STATES

    @pl.when(pl.program_id(0) == 0)
    def _():
        x_ref[...] = jnp.zeros_like(x_ref)

    u2 = u_ref[...].reshape(tc * nb, S5_CH).astype(BF16)
    for k in range(S5_BLOCKS):
        r = jnp.dot(u2[:, k * LANES:(k + 1) * LANES], bb_ref[k], preferred_element_type=F32)
        bu_ref[:, :, k * sb:(k + 1) * sb] = r[:, 0:sb].reshape(tc, nb, sb)
        bu_ref[:, :, n + k * sb:n + (k + 1) * sb] = r[:, sb:].reshape(tc, nb, sb)

    def step(i, carry):
        t = tc - 1 - i if reverse else i
        xr, xi = carry
        a_re = are_ref[...]
        a_im = aim_ref[...]
        nr = a_re * xr - a_im * xi + bu_ref[t, :, 0:n]
        ni = a_re * xi + a_im * xr + bu_ref[t, :, n:]
        bu_ref[t, :, 0:n] = nr
        bu_ref[t, :, n:] = ni
        return nr, ni

    xr, xi = lax.fori_loop(0, tc, step, (x_ref[:, 0:n], x_ref[:, n:]), unroll=2)
    x_ref[:, 0:n] = xr
    x_ref[:, n:] = xi

    for k in range(S5_BLOCKS):
        xs = jnp.concatenate([bu_ref[:, :, k * sb:(k + 1) * sb], bu_ref[:, :, n + k * sb:n + (k + 1) * sb]], axis=2)
        yk = jnp.dot(xs.reshape(tc * nb, 2 * sb).astype(BF16), cb_ref[k], preferred_element_type=F32)
        y_ref[:, :, k * LANES:(k + 1) * LANES] = yk.reshape(tc, nb, LANES)


def _s5_direction_params(lam_re, lam_im, log_step, b_re, b_im, c_re, c_im, nb):
    lam = lax.complex(jnp.minimum(lam_re.astype(F32), -1e-4), lam_im.astype(F32))
    step = jnp.exp(log_step.astype(F32))[:, None]
    lam_bar = jnp.exp(lam * step)
    bmat = lax.complex(b_re.astype(F32), b_im.astype(F32))
    b_bar = ((lam_bar - 1.0) / lam)[:, :, None] * bmat
    gpb = S5_GROUPS // S5_BLOCKS
    eye = jnp.eye(gpb, dtype=F32)

    def in_map(t):
        t = t.reshape(S5_BLOCKS, gpb, S5_STATE, S5_GROUP_CH)
        return jnp.einsum('kgpc,gh->kgchp', t, eye).reshape(S5_BLOCKS, LANES, S5_BLOCK_STATES)

    def out_map(t):
        t = t.reshape(S5_BLOCKS, gpb, S5_GROUP_CH, S5_STATE)
        return jnp.einsum('kgcp,gh->kgphc', t, eye).reshape(S5_BLOCKS, S5_BLOCK_STATES, LANES)

    bblk = jnp.concatenate([in_map(jnp.real(b_bar)), in_map(jnp.imag(b_bar))], axis=2).astype(BF16)
    cblk = jnp.concatenate([out_map(c_re.astype(F32)), -out_map(c_im.astype(F32))], axis=1).astype(BF16)
    a_re = jnp.broadcast_to(jnp.real(lam_bar).reshape(1, S5_NSTATE), (nb, S5_NSTATE)).astype(F32)
    a_im = jnp.broadcast_to(jnp.imag(lam_bar).reshape(1, S5_NSTATE), (nb, S5_NSTATE)).astype(F32)
    return bblk, a_re, a_im, cblk


def _s5_scan(u_tm, params, reverse, tc=64):
    s, nb, _ = u_tm.shape
    bdense, a_re, a_im, cdense = params
    nchunks = s // tc
    tmap = (lambda i: (nchunks - 1 - i, 0, 0)) if reverse else (lambda i: (i, 0, 0))
    const = lambda shape: pl.BlockSpec(shape, lambda i: (0,) * len(shape))
    return pl.pallas_call(
        functools.partial(_s5_kernel, reverse=reverse),
        grid=(nchunks,),
        in_specs=[pl.BlockSpec((tc, nb, S5_CH), tmap), const(bdense.shape), const(a_re.shape),
                  const(a_im.shape), const(cdense.shape)],
        out_specs=pl.BlockSpec((tc, nb, S5_CH), tmap),
        out_shape=jax.ShapeDtypeStruct((s, nb, S5_CH), F32),
        scratch_shapes=[pltpu.VMEM((tc, nb, 2 * S5_NSTATE), F32), pltpu.VMEM((nb, 2 * S5_NSTATE), F32)],
        compiler_params=_cparams(("arbitrary",)),
        name="l1_s5_scan_bwd" if reverse else "l1_s5_scan_fwd",
    )(u_tm, bdense, a_re, a_im, cdense)


def _l1_out_router_kernel(x_ref, mla_ref, yf_ref, yb_ref, u_ref, dskip_ref, gw_ref, gb_ref, wo_ref,
                          fn_ref, rt_ref, x1_out, route_out):
    tm = x_ref.shape[1]
    u = u_ref[...]
    y = yf_ref[...] + yb_ref[...] + dskip_ref[...] * u
    z = jax.nn.gelu(y)
    gate = jax.nn.sigmoid(jnp.dot(z.astype(BF16), gw_ref[...], preferred_element_type=F32) + gb_ref[...])
    ssm = (z * gate).astype(BF16)
    half = wo_ref.shape[0] // 2
    x1 = (x_ref[0]
          + jnp.dot(mla_ref[0], wo_ref[0:half, :], preferred_element_type=F32)
          + jnp.dot(ssm, wo_ref[half:, :], preferred_element_type=F32))
    _store_token_slabs(x1_out.at[0], x1, tm)
    hn = _rms(x1, fn_ref[...])

    logits = jnp.dot(hn, rt_ref[...], preferred_element_type=F32, precision=lax.Precision.HIGHEST)
    lane = lax.broadcasted_iota(jnp.int32, logits.shape, 1)
    lg = jnp.where(lane < N_EXPERTS, logits, -jnp.inf)
    m1 = jnp.max(lg, axis=-1, keepdims=True)
    i1 = jnp.min(jnp.where(lg == m1, lane, LANES), axis=-1, keepdims=True)
    lg2 = jnp.where(lane == i1, -jnp.inf, lg)
    m2 = jnp.max(lg2, axis=-1, keepdims=True)
    i2 = jnp.min(jnp.where(lg2 == m2, lane, LANES), axis=-1, keepdims=True)
    e2 = jnp.exp(m2 - m1)
    w1 = 1.0 / (1.0 + e2)
    w2 = e2 / (1.0 + e2)
    route_out[0] = jnp.where(lane == 0, i1.astype(F32),
                             jnp.where(lane == 1, i2.astype(F32),
                                       jnp.where(lane == 2, w1, jnp.where(lane == 3, w2, 0.0))))


def _l1_out_router(x, mla, yf, yb, u_tm, d_skip, glu_w, glu_b, w_out, ffn_norm, router_pad, tm=512):
    b, s, d = x.shape
    tok = lambda n: pl.BlockSpec((1, tm, n), lambda i, j: (i, j, 0))
    tmaj = pl.BlockSpec((tm, S5_CH), lambda i, j: (j, i))
    const = lambda shape: pl.BlockSpec(shape, lambda i, j: (0, 0))
    return pl.pallas_call(
        _l1_out_router_kernel,
        grid=(b, s // tm),
        in_specs=[tok(d), tok(MLA_HEADS * MLA_V), tmaj, tmaj, tmaj, const((1, S5_CH)),
                  const(glu_w.shape), const((1, S5_CH)), const(w_out.shape), const((1, d)),
                  const(router_pad.shape)],
        out_specs=[pl.BlockSpec((1, tm * SLAB_ROWS, LANES), lambda i, j: (i, j, 0)), tok(LANES)],
        out_shape=[jax.ShapeDtypeStruct((b, s * SLAB_ROWS, LANES), F32),
                   jax.ShapeDtypeStruct((b, s, LANES), F32)],
        compiler_params=_cparams(("parallel", "parallel")),
        name="l1_out_proj_router",
    )(x, mla, yf, yb, u_tm, d_skip.reshape(1, -1), glu_w, glu_b.reshape(1, -1), w_out,
      ffn_norm.reshape(1, d), router_pad)


def _start_row_gather(idx_ref, src_hbm, dst, sem, n_rows):
    def issue(r, carry):
        src = pl.multiple_of(idx_ref[r] * SLAB_ROWS, SLAB_ROWS)
        pltpu.make_async_copy(src_hbm.at[pl.ds(src, SLAB_ROWS)],
                              dst.at[pl.ds(pl.multiple_of(r * SLAB_ROWS, SLAB_ROWS), SLAB_ROWS)], sem).start()
        return carry

    lax.fori_loop(0, n_rows, issue, 0, unroll=8)


def _wait_row_gather(src_hbm, dst, sem, n_rows):
    pltpu.make_async_copy(src_hbm.at[pl.ds(0, n_rows * SLAB_ROWS)], dst, sem).wait()


def _expert_ffn_kernel(te_ref, nt_ref, idx0_ref, idxn_ref, x_hbm, gw_ref, g_ref, wg_ref, wu_ref, wd_ref,
                       o_ref, xbuf, sem, hn_ref, acc_ref):
    i = pl.program_id(0)
    j = pl.program_id(1)
    last = pl.num_programs(1) - 1
    tm = hn_ref.shape[0]
    n_used = nt_ref[0]
    slot = i % 2

    @pl.when(jnp.logical_and(i == 0, j == 0))
    def _():
        _start_row_gather(idx0_ref, x_hbm, xbuf.at[0], sem.at[0], tm)

    @pl.when(jnp.logical_and(j == 0, i < n_used))
    def _():
        _wait_row_gather(x_hbm, xbuf.at[slot], sem.at[slot], tm)
        x = _load_token_slabs(xbuf.at[slot], tm)
        hn_ref[...] = _rms(x, g_ref[...]).astype(BF16)
        acc_ref[...] = jnp.zeros_like(acc_ref)

    @pl.when(jnp.logical_and(j == 0, i + 1 < n_used))
    def _():
        _start_row_gather(idxn_ref, x_hbm, xbuf.at[1 - slot], sem.at[1 - slot], tm)

    @pl.when(i < n_used)
    def _():
        h = hn_ref[...]
        a = jnp.dot(h, wg_ref[0], preferred_element_type=F32)
        u = jnp.dot(h, wu_ref[0], preferred_element_type=F32)
        act = (a * jax.nn.sigmoid(a) * u).astype(BF16)
        acc_ref[...] += jnp.dot(act, wd_ref[0], preferred_element_type=F32)

        @pl.when(j == last)
        def _():
            _store_token_slabs(o_ref, gw_ref[...] * acc_ref[...], tm)

    @pl.when(jnp.logical_and(i >= n_used, j == last))
    def _():
        o_ref[...] = jnp.zeros_like(o_ref)


def _expert_ffn(x_slabs, src_tok, gate_w, tile_expert, n_tiles_used, ffn_norm, wg, wu, wd, tm, tf=1792):
    a_pad = src_tok.shape[0]
    n_tiles = a_pad // tm
    d = D_MODEL
    f = wg.shape[2]
    grid_spec = pltpu.PrefetchScalarGridSpec(
        num_scalar_prefetch=2,
        grid=(n_tiles, f // tf),
        in_specs=[
            pl.BlockSpec((tm,), lambda i, j, te, nt: (0,), memory_space=pltpu.SMEM),
            pl.BlockSpec((tm,), lambda i, j, te, nt: (jnp.minimum(i + 1, n_tiles - 1),), memory_space=pltpu.SMEM),
            pl.BlockSpec(memory_space=pl.ANY),
            pl.BlockSpec((tm, 1), lambda i, j, te, nt: (i, 0)),
            pl.BlockSpec((1, d), lambda i, j, te, nt: (0, 0)),
            pl.BlockSpec((1, d, tf), lambda i, j, te, nt: (te[i], 0, j)),
            pl.BlockSpec((1, d, tf), lambda i, j, te, nt: (te[i], 0, j)),
            pl.BlockSpec((1, tf, d), lambda i, j, te, nt: (te[i], j, 0)),
        ],
        out_specs=pl.BlockSpec((tm * SLAB_ROWS, LANES), lambda i, j, te, nt: (i, 0)),
        scratch_shapes=[pltpu.VMEM((2, tm * SLAB_ROWS, LANES), F32), pltpu.SemaphoreType.DMA((2,)),
                        pltpu.VMEM((tm, d), BF16), pltpu.VMEM((tm, d), F32)],
    )
    return pl.pallas_call(
        _expert_ffn_kernel,
        grid_spec=grid_spec,
        out_shape=jax.ShapeDtypeStruct((a_pad * SLAB_ROWS, LANES), F32),
        compiler_params=_cparams(("arbitrary", "arbitrary")),
        name="l1_expert_ffn",
    )(tile_expert, n_tiles_used, src_tok, src_tok, x_slabs, gate_w, ffn_norm.reshape(1, d), wg, wu, wd)


def _combine_norm_kernel(ia_ref, ib_ref, x_ref, ys_hbm, g_ref, o_ref, buf, sem):
    tm = o_ref.shape[0]
    _start_row_gather(ia_ref, ys_hbm, buf.at[0], sem.at[0], tm)
    _start_row_gather(ib_ref, ys_hbm, buf.at[1], sem.at[1], tm)
    _wait_row_gather(ys_hbm, buf.at[0], sem.at[0], tm)
    _wait_row_gather(ys_hbm, buf.at[1], sem.at[1], tm)
    y = _load_token_slabs(x_ref, tm) + (_load_token_slabs(buf.at[0], tm) + _load_token_slabs(buf.at[1], tm))
    o_ref[...] = _rms(y, g_ref[...])


def _combine_norm(x_slabs, ys_slabs, dest, final_norm, tm=256):
    t = x_slabs.shape[0] // SLAB_ROWS
    nblk = t // tm
    slab = (tm * SLAB_ROWS, LANES)
    return pl.pallas_call(
        _combine_norm_kernel,
        grid=(nblk,),
        in_specs=[pl.BlockSpec((tm,), lambda i: (i,), memory_space=pltpu.SMEM),
                  pl.BlockSpec((tm,), lambda i: (i + nblk,), memory_space=pltpu.SMEM),
                  pl.BlockSpec(slab, lambda i: (i, 0)),
                  pl.BlockSpec(memory_space=pl.ANY),
                  pl.BlockSpec((1, D_MODEL), lambda i: (0, 0))],
        out_specs=pl.BlockSpec((tm, D_MODEL), lambda i: (i, 0)),
        out_shape=jax.ShapeDtypeStruct((t, D_MODEL), F32),
        scratch_shapes=[pltpu.VMEM((2,) + slab, F32), pltpu.SemaphoreType.DMA((2,))],
        compiler_params=_cparams(("arbitrary",)),
        name="l1_combine_final_norm",
    )(dest, dest, x_slabs, ys_slabs, final_norm.reshape(1, D_MODEL))


def _moe(x_slabs, route, ffn_norm, wg, wu, wd, final_norm, tm=512):
    t = x_slabs.shape[0] // SLAB_ROWS
    e_idx = jnp.concatenate([route[:, 0], route[:, 1]]).astype(jnp.int32)
    e_w = jnp.concatenate([route[:, 2], route[:, 3]])
    n_assign = TOP_K * t
    order = jnp.argsort(e_idx, stable=True).astype(jnp.int32)
    inv = jnp.argsort(order).astype(jnp.int32)
    counts = jnp.sum(e_idx[:, None] == jnp.arange(N_EXPERTS, dtype=jnp.int32)[None, :], axis=0).astype(jnp.int32)
    starts = jnp.cumsum(counts) - counts
    padded = ((counts + tm - 1) // tm) * tm
    pad_ends = jnp.cumsum(padded)
    pad_starts = pad_ends - padded
    a_pad = n_assign + N_EXPERTS * tm
    n_tiles = a_pad // tm

    tile_start = jnp.arange(n_tiles, dtype=jnp.int32) * tm
    tile_expert = jnp.minimum(jnp.sum(tile_start[:, None] >= pad_ends[None, :], axis=1), N_EXPERTS - 1).astype(jnp.int32)
    n_tiles_used = (pad_ends[-1] // tm).astype(jnp.int32).reshape(1)

    slot = jnp.arange(a_pad, dtype=jnp.int32)
    slot_e = jnp.repeat(tile_expert, tm)
    within = slot - pad_starts[slot_e]
    valid = within < counts[slot_e]
    src = order[jnp.clip(starts[slot_e] + within, 0, n_assign - 1)]
    src_tok = jnp.where(valid, src % t, 0).astype(jnp.int32)
    gate_w = jnp.where(valid, e_w[src], 0.0).astype(F32).reshape(a_pad, 1)

    ys = _expert_ffn(x_slabs, src_tok, gate_w, tile_expert, n_tiles_used, ffn_norm, wg, wu, wd, tm)
    dest = (pad_starts[e_idx] + inv - starts[e_idx]).astype(jnp.int32)
    return _combine_norm(x_slabs, ys, dest, final_norm)


def _l0_in_weight(w_in):
    hq = RET_HEADS * RET_DK
    hv = RET_HEADS * RET_DV
    hd = DIL_HEADS * DIL_DH
    rq, rk, rv, rg, dq, dk, dv = jnp.split(
        w_in, [hq, 2 * hq, 2 * hq + hv, 2 * hq + 2 * hv, 2 * hq + 2 * hv + hd, 2 * hq + 2 * hv + 2 * hd], axis=1)
    dup = lambda t: jnp.concatenate([t.reshape(-1, RET_HEADS, 1, RET_DK)] * 2, axis=2).reshape(-1, 2 * hq)
    w = jnp.concatenate([dup(rq) * RET_DK ** -0.5, dup(rk), rv, rg, dq * DIL_DH ** -0.5, dk, dv], axis=1)
    assert w.shape[1] == L0_COLS
    return w.astype(BF16)


def _prepare(l0_w_in, l0_ret_decay_f, l0_ret_decay_b, l0_w_out, l0_ffn_w_gate, l0_ffn_w_up, l0_ffn_w_down,
             l1_w_in, l1_mla_w_uq, l1_mla_w_ukv, l1_s5_glu_w, l1_w_out, l1_router,
             l1_exp_w_gate, l1_exp_w_up, l1_exp_w_down):
    w1, wq2, wkv2 = _l1_weights(l1_w_in, l1_mla_w_uq, l1_mla_w_ukv)
    router_pad = jnp.zeros((D_MODEL, LANES), F32).at[:, :N_EXPERTS].set(l1_router.astype(F32))
    return dict(
        l0_w_in=_l0_in_weight(l0_w_in),
        ret_tables=_retention_tables(l0_ret_decay_f, l0_ret_decay_b),
        l0_w_out=l0_w_out.astype(BF16),
        l0_wg=l0_ffn_w_gate.astype(BF16), l0_wu=l0_ffn_w_up.astype(BF16), l0_wd=l0_ffn_w_down.astype(BF16),
        l1_w1=w1, l1_wq2=wq2, l1_wkv2=wkv2,
        glu_w=l1_s5_glu_w.astype(BF16), l1_w_out=l1_w_out.astype(BF16), router_pad=router_pad,
        exp_wg=l1_exp_w_gate.astype(BF16), exp_wu=l1_exp_w_up.astype(BF16), exp_wd=l1_exp_w_down.astype(BF16),
    )


def _trunk(x, prep, p):
    b, s, d = x.shape
    t = b * s
    p0, *regrouped = _l0_in_proj(x.reshape(t, d), p['l0_attn_norm'], prep['l0_w_in'])
    p0 = p0.reshape(b, s, L0_COLS)
    ret = _retention(p0, prep['ret_tables'], p['l0_ret_gn'])
    dil = _dilated_attention(p0, regrouped)
    x = _l0_out_ffn(x.reshape(t, d), ret.reshape(t, -1), dil.reshape(t, -1), prep['l0_w_out'],
                    p['l0_ffn_norm'], prep['l0_wg'], prep['l0_wu'], prep['l0_wd']).reshape(b, s, d)
    nb = -(-b // SUBLANES) * SUBLANES
    ct, st = _rope_lane_tables(s)
    q, k, v, u_tm = _l1_proj(x, p['l1_attn_norm'], prep['l1_w1'], p['l1_mla_q_norm'], prep['l1_wq2'],
                             p['l1_mla_kv_norm'], prep['l1_wkv2'], ct, st, b)
    mla = _mla_attention(q, k, v)
    u3 = u_tm.reshape(s, b, S5_CH)
    if nb != b:
        u3 = jnp.pad(u3, ((0, 0), (0, nb - b), (0, 0)))
    sf = _s5_direction_params(p['l1_s5_lam_re_f'], p['l1_s5_lam_im_f'], p['l1_s5_log_step_f'],
                              p['l1_s5_b_re'], p['l1_s5_b_im'], p['l1_s5_c_re'], p['l1_s5_c_im'], nb)
    sb = _s5_direction_params(p['l1_s5_lam_re_b'], p['l1_s5_lam_im_b'], p['l1_s5_log_step_b'],
                              p['l1_s5_b_re'], p['l1_s5_b_im'], p['l1_s5_c_re'], p['l1_s5_c_im'], nb)
    yf = _s5_scan(u3, sf, reverse=False).reshape(s, nb * S5_CH)
    yb = _s5_scan(u3, sb, reverse=True).reshape(s, nb * S5_CH)
    x1s, route = _l1_out_router(x, mla, yf, yb, u_tm, p['l1_s5_d'], prep['glu_w'], p['l1_s5_glu_b'],
                                prep['l1_w_out'], p['l1_ffn_norm'], prep['router_pad'])
    out = _moe(x1s.reshape(t * SLAB_ROWS, LANES), route.reshape(t, LANES), p['l1_ffn_norm'],
               prep['exp_wg'], prep['exp_wu'], prep['exp_wd'], p['final_norm'])
    return out.reshape(b, s, d)


def kernel(x_prompt, x_sample, l0_attn_norm, l0_w_in, l0_ret_decay_f, l0_ret_decay_b, l0_ret_gn, l0_w_out,
           l0_ffn_norm, l0_ffn_w_gate, l0_ffn_w_up, l0_ffn_w_down, l1_attn_norm, l1_w_in, l1_mla_q_norm,
           l1_mla_w_uq, l1_mla_kv_norm, l1_mla_w_ukv, l1_s5_lam_re_f, l1_s5_lam_im_f, l1_s5_log_step_f,
           l1_s5_lam_re_b, l1_s5_lam_im_b, l1_s5_log_step_b, l1_s5_b_re, l1_s5_b_im, l1_s5_c_re, l1_s5_c_im,
           l1_s5_d, l1_s5_glu_w, l1_s5_glu_b, l1_w_out, l1_ffn_norm, l1_router, l1_exp_w_gate, l1_exp_w_up,
           l1_exp_w_down, final_norm):
    p = dict(locals())
    prep = _prepare(l0_w_in, l0_ret_decay_f, l0_ret_decay_b, l0_w_out, l0_ffn_w_gate, l0_ffn_w_up,
                    l0_ffn_w_down, l1_w_in, l1_mla_w_uq, l1_mla_w_ukv, l1_s5_glu_w, l1_w_out, l1_router,
                    l1_exp_w_gate, l1_exp_w_up, l1_exp_w_down)
    return (_trunk(x_prompt, prep, p), _trunk(x_sample, prep, p))
```

```python
import functools
import math

import jax
import jax.numpy as jnp
from jax import lax
from jax.experimental import pallas as pl
from jax.experimental.pallas import tpu as pltpu

F32 = jnp.float32
BF16 = jnp.bfloat16

D_MODEL = 1024
EPS = 1e-6
NEG_INF = -1e30
RET_HEADS = 4
RET_DK = 64
RET_DV = 128
RET_CHUNK = 128
RET_UNROLL = 4
DIL_HEADS = 8
DIL_DH = 64
DIL_PATTERNS = ((128, 1), (512, 4), (2048, 16))
DIL_QBLOCK = 128
DIL_UNROLL = 4
MLA_HEADS = 8
MLA_Q_RANK = 256
MLA_KV_RANK = 128
MLA_NOPE = 64
MLA_ROPE = 32
MLA_V = 64
ROPE_BASE = 10000.0
S5_GROUPS = 32
S5_GROUP_CH = 16
S5_STATE = 64
S5_CH = S5_GROUPS * S5_GROUP_CH
S5_NSTATE = S5_GROUPS * S5_STATE
N_EXPERTS = 8
TOP_K = 2

LANES = 128
SUBLANES = 8
VMEM_LIMIT = 56 * 1024 * 1024

L0_COLS = 3584
L0_BLOCKS = L0_COLS // LANES
L0_RQ, L0_RK, L0_RV, L0_RG, L0_DQ, L0_DK, L0_DV = 0, 4, 8, 12, 16, 20, 24


def _cparams(sem):
    return pltpu.CompilerParams(dimension_semantics=sem, vmem_limit_bytes=VMEM_LIMIT)


def _rms(x, g):
    return x * lax.rsqrt(jnp.mean(x * x, axis=-1, keepdims=True) + EPS) * g


SLAB_ROWS = D_MODEL // LANES


def _store_token_slabs(ref2d, x, n_tok):
    for s in range(SLAB_ROWS):
        ref2d[pl.ds(s, n_tok, stride=SLAB_ROWS), :] = x[:, s * LANES:(s + 1) * LANES]


def _load_token_slabs(ref2d, n_tok):
    return jnp.concatenate([ref2d[pl.ds(s, n_tok, stride=SLAB_ROWS), :] for s in range(SLAB_ROWS)], axis=1)


DIL_COLS = 3 * DIL_HEADS * DIL_DH
DIL_COL0 = L0_DQ * LANES
DIL_STRIDED = tuple(dil for _, dil in DIL_PATTERNS if dil > 1)


def _l0_in_proj_kernel(x_ref, g_ref, w_ref, o_ref, *rest):
    dil_refs, dsc = rest[:-1], rest[-1]
    tm = x_ref.shape[0]
    xn = _rms(x_ref[...], g_ref[...]).astype(BF16)
    res = jnp.dot(xn, w_ref[...], preferred_element_type=F32)
    o_ref[...] = res.astype(o_ref.dtype)
    for c in range(DIL_COLS // LANES):
        dsc[c] = res[:, DIL_COL0 + c * LANES:DIL_COL0 + (c + 1) * LANES]
    for ref, dil in zip(dil_refs, DIL_STRIDED):
        for r in range(dil):
            for c in range(DIL_COLS // LANES):
                lo = r * DIL_COLS + c * LANES
                ref[:, lo:lo + LANES] = dsc[c, pl.ds(r, tm // dil, stride=dil), :].astype(ref.dtype)


def _l0_in_proj(x2d, gain, w, tm=512):
    t, d = x2d.shape
    n = w.shape[1]
    dil_specs = [pl.BlockSpec((tm // dil, dil * DIL_COLS), lambda i: (i, 0)) for dil in DIL_STRIDED]
    dil_shapes = [jax.ShapeDtypeStruct((t // dil, dil * DIL_COLS), BF16) for dil in DIL_STRIDED]
    return pl.pallas_call(
        _l0_in_proj_kernel,
        grid=(t // tm,),
        in_specs=[
            pl.BlockSpec((tm, d), lambda i: (i, 0)),
            pl.BlockSpec((1, d), lambda i: (0, 0)),
            pl.BlockSpec((d, n), lambda i: (0, 0)),
        ],
        out_specs=[pl.BlockSpec((tm, n), lambda i: (i, 0))] + dil_specs,
        out_shape=[jax.ShapeDtypeStruct((t, n), BF16)] + dil_shapes,
        scratch_shapes=[pltpu.VMEM((DIL_COLS // LANES, tm, LANES), F32)],
        compiler_params=_cparams(("parallel",)),
        name="l0_norm_in_proj",
    )(x2d, gain.reshape(1, d), w)


def _retention_kernel(q_ref, k_ref, v_ref, g_ref, d_ref, qw_ref, kw_ref, cd_ref, gn_ref,
                      o_ref, kv_ref, p_ref, *, n_chunks):
    c = RET_CHUNK
    kw = kw_ref[0]
    qw = qw_ref[0]
    dmat = d_ref[0]
    gn = gn_ref[...]

    def kv_body(n, carry):
        sl = pl.ds(pl.multiple_of(n * c, c), c)
        kc = (k_ref[0, sl, :].astype(F32) * kw).T.astype(BF16)
        kv_ref[n] = jnp.dot(kc, v_ref[0, sl, :], preferred_element_type=F32)
        return carry

    lax.fori_loop(0, n_chunks, kv_body, 0, unroll=RET_UNROLL)

    half = c // 2
    dec_f = cd_ref[0, 0:half, :]
    dec_b = cd_ref[0, half:c, :]

    def fwd_body(n, s):
        p_ref[n, 0:half, :] = s.astype(BF16)
        return s * dec_f + kv_ref[n, 0:half, :]

    lax.fori_loop(0, n_chunks, fwd_body, jnp.zeros((half, RET_DV), F32))

    def bwd_body(i, s):
        n = n_chunks - 1 - i
        p_ref[n, half:c, :] = s.astype(BF16)
        return s * dec_b + kv_ref[n, half:c, :]

    lax.fori_loop(0, n_chunks, bwd_body, jnp.zeros((half, RET_DV), F32))

    def out_body(n, carry):
        sl = pl.ds(pl.multiple_of(n * c, c), c)
        qc = q_ref[0, sl, :]
        s = lax.dot_general(qc, k_ref[0, sl, :], (((1,), (1,)), ((), ())), preferred_element_type=F32)
        intra = jnp.dot((s * dmat).astype(BF16), v_ref[0, sl, :], preferred_element_type=F32)
        qq = (qc.astype(F32) * qw).astype(BF16)
        y = intra + jnp.dot(qq, p_ref[n], preferred_element_type=F32)
        mu = jnp.mean(y, axis=-1, keepdims=True)
        yc = y - mu
        var = jnp.mean(yc * yc, axis=-1, keepdims=True)
        yn = yc * lax.rsqrt(var + EPS) * gn
        gg = g_ref[0, sl, :].astype(F32)
        o_ref[0, sl, :] = (gg * jax.nn.sigmoid(gg) * yn).astype(o_ref.dtype)
        return carry

    lax.fori_loop(0, n_chunks, out_body, 0, unroll=RET_UNROLL)


def _retention_tables(decay_f, decay_b):
    c = RET_CHUNK
    lg_f = jax.nn.log_sigmoid(decay_f.astype(F32))[:, None, None]
    lg_b = jax.nn.log_sigmoid(decay_b.astype(F32))[:, None, None]
    j = jnp.arange(c, dtype=F32)
    diff = j[:, None] - j[None, :]
    dmat = 0.5 * jnp.where(diff >= 0, jnp.exp(lg_f * jnp.maximum(diff, 0.0)),
                           jnp.exp(lg_b * jnp.maximum(-diff, 0.0)))
    lane_f = (jnp.arange(LANES) < RET_DK)[None, None, :]
    jj = j[None, :, None]
    qw = jnp.where(lane_f, jnp.exp(lg_f * (jj + 1.0)), jnp.exp(lg_b * (c - jj)))
    kw = jnp.where(lane_f, jnp.exp(lg_f * (c - 1.0 - jj)), jnp.exp(lg_b * jj))
    row_f = (jnp.arange(c) < c // 2)[None, :, None]
    cd = jnp.where(row_f, jnp.exp(lg_f * c), jnp.exp(lg_b * c)) * jnp.ones((1, 1, RET_DV), F32)
    return dmat.astype(F32), qw.astype(F32), kw.astype(F32), cd.astype(F32)


def _retention(p0, tables, gn):
    b, s, _ = p0.shape
    dmat, qw, kw, cd = tables
    n_chunks = s // RET_CHUNK
    seq = lambda col: pl.BlockSpec((1, s, LANES), lambda i, h: (i, 0, col + h))
    tab = pl.BlockSpec((1, RET_CHUNK, LANES), lambda i, h: (h, 0, 0))
    return pl.pallas_call(
        functools.partial(_retention_kernel, n_chunks=n_chunks),
        grid=(b, RET_HEADS),
        in_specs=[seq(L0_RQ), seq(L0_RK), seq(L0_RV), seq(L0_RG), tab, tab, tab, tab,
                  pl.BlockSpec((1, LANES), lambda i, h: (0, h))],
        out_specs=pl.BlockSpec((1, s, LANES), lambda i, h: (i, 0, h)),
        out_shape=jax.ShapeDtypeStruct((b, s, RET_HEADS * RET_DV), BF16),
        scratch_shapes=[pltpu.VMEM((n_chunks, RET_CHUNK, RET_DV), F32),
                        pltpu.VMEM((n_chunks, RET_CHUNK, RET_DV), BF16)],
        compiler_params=_cparams(("parallel", "parallel")),
        name="l0_retention",
    )(p0, p0, p0, p0, dmat, qw, kw, cd, gn.reshape(1, -1))


def _dilated_kernel(*refs, seq_len, half, q_rows, n_hp, has_prev, final):
    q_ref, k_ref, v_ref, bias_ref = refs[:4]
    refs = refs[4:]
    ratio = DIL_CHAIN_RATIO
    if has_prev:
        prev_o, prev_l, refs = refs[:ratio], refs[ratio:2 * ratio], refs[2 * ratio:]
        po_ref, pl_ref = refs[-2:]
        refs = refs[:-2]
        for m in range(ratio):
            for hp in range(n_hp):
                ls = slice(hp * LANES, (hp + 1) * LANES)
                po_ref[hp, pl.ds(m, q_rows // ratio, stride=ratio), :] = prev_o[m][0, :, ls]
                pl_ref[hp, pl.ds(m, q_rows // ratio, stride=ratio), :] = prev_l[m][0, :, ls]
    o_ref = refs[0]
    lse_ref = None if final else refs[1]
    tq = DIL_QBLOCK
    kw = tq + 2 * half
    base = pl.program_id(3) * q_rows
    lane = lax.broadcasted_iota(jnp.int32, (tq, LANES), 1)
    lo = lane < DIL_DH

    def blk(i, carry):
        off = pl.multiple_of(i * tq, tq)
        rows = pl.ds(off, tq)
        qs = base + off
        ks = pl.multiple_of(jnp.clip(qs - half, 0, seq_len - kw), half)
        shift = (qs - ks) // half
        for hp in range(n_hp):
            ls = slice(hp * LANES, (hp + 1) * LANES)
            q = q_ref[0, rows, ls]
            kwin = k_ref[0, pl.ds(ks, kw), ls]
            vwin = v_ref[0, pl.ds(ks, kw), ls]
            outs, lses = [], []
            for hh in range(2):
                qh = jnp.where(lo if hh == 0 else jnp.logical_not(lo), q, jnp.zeros_like(q))
                s = lax.dot_general(qh, kwin, (((1,), (1,)), ((), ())), preferred_element_type=F32)
                s = s + bias_ref[hp, hh, shift]
                m = jnp.max(s, axis=-1, keepdims=True)
                p = jnp.exp(s - m)
                den = jnp.sum(p, axis=-1, keepdims=True)
                outs.append(jnp.dot(p.astype(BF16), vwin, preferred_element_type=F32) / den)
                lses.append(m + jnp.log(den))
            o = jnp.where(lo, outs[0], outs[1])
            lse = jnp.where(lo, lses[0], lses[1])
            if has_prev:
                lse_p = pl_ref[hp, rows, :]
                top = jnp.maximum(lse_p, lse)
                wa = jnp.exp(lse_p - top)
                wb = jnp.exp(lse - top)
                den = wa + wb
                o = (wa * po_ref[hp, rows, :] + wb * o) / den
                lse = top + jnp.log(den)
            o_ref[0, rows, ls] = o.astype(o_ref.dtype)
            if not final:
                lse_ref[0, rows, ls] = lse
        return carry

    n_blk = q_rows // tq
    lax.fori_loop(0, n_blk, blk, 0, unroll=max(1, min(DIL_UNROLL // n_hp, n_blk)))


DIL_MAX_RESIDENT_ROWS = 2048


DIL_CHAIN_RATIO = 4
assert all(DIL_PATTERNS[n + 1][1] == DIL_CHAIN_RATIO * DIL_PATTERNS[n][1] for n in range(len(DIL_PATTERNS) - 1))


def _dilated_pattern(src, col0, batch, window, dil, prev, final):
    width = DIL_HEADS * DIL_DH
    seq_len = src.shape[1]
    group_blocks = src.shape[2] // dil // LANES
    half = window // (2 * dil)
    q_rows = min(seq_len, 1024)
    hpairs = DIL_HEADS // 2
    n_hp = hpairs if seq_len <= DIL_MAX_RESIDENT_ROWS else 1
    lanes = n_hp * LANES
    hp_blocks = hpairs // n_hp
    c0 = col0 // LANES
    slopes = (jnp.exp2(-8.0 * jnp.arange(1, DIL_HEADS + 1, dtype=F32) / DIL_HEADS) * dil).reshape(hpairs, 2, 1, 1, 1)
    kw = DIL_QBLOCK + 2 * half
    shift = jnp.arange(3, dtype=jnp.int32)[:, None, None] * half
    dist = jnp.abs(jnp.arange(DIL_QBLOCK, dtype=jnp.int32)[None, :, None] + shift
                   - jnp.arange(kw, dtype=jnp.int32)[None, None, :])
    bias = jnp.where(dist <= half, -(slopes * dist.astype(F32)), NEG_INF).astype(F32)
    col = lambda part: (lambda i, r, hp, j: (i, 0, (r * group_blocks + c0 + part * hpairs) // n_hp + hp))
    qcol = lambda i, r, hp, j: (i, j, (r * group_blocks + c0) // n_hp + hp)
    oblk = pl.BlockSpec((1, q_rows, lanes), lambda i, r, hp, j: (i, j, r * hp_blocks + hp))
    in_specs = [pl.BlockSpec((1, q_rows, lanes), qcol),
                pl.BlockSpec((1, seq_len, lanes), col(1)),
                pl.BlockSpec((1, seq_len, lanes), col(2)),
                pl.BlockSpec((n_hp, 2, 3, DIL_QBLOCK, kw), lambda i, r, hp, j: (hp, 0, 0, 0, 0))]
    args = [src, src, src, bias]
    scratch = []
    if prev is not None:
        ratio = DIL_CHAIN_RATIO
        pblk = lambda m: pl.BlockSpec((1, q_rows // ratio, lanes),
                                      lambda i, r, hp, j: (i, j, (m * dil + r) * hp_blocks + hp))
        for a in prev:
            in_specs += [pblk(m) for m in range(ratio)]
            args += [a] * ratio
        scratch = [pltpu.VMEM((n_hp, q_rows, LANES), F32)] * 2
    oshape = (batch, seq_len, dil * width)
    if final:
        out_specs, out_shape = oblk, jax.ShapeDtypeStruct(oshape, BF16)
    else:
        out_specs, out_shape = [oblk, oblk], [jax.ShapeDtypeStruct(oshape, F32)] * 2
    return pl.pallas_call(
        functools.partial(_dilated_kernel, seq_len=seq_len, half=half, q_rows=q_rows, n_hp=n_hp,
                          has_prev=prev is not None, final=final),
        grid=(batch, dil, hp_blocks, seq_len // q_rows),
        in_specs=in_specs,
        out_specs=out_specs,
        out_shape=out_shape,
        scratch_shapes=scratch,
        compiler_params=_cparams(("parallel", "parallel", "parallel", "arbitrary")),
        name=f"l0_dilated_d{dil}",
    )(*args)


def _dilated_attention(p0, regrouped):
    b, s, _ = p0.shape
    by_dil = dict(zip(DIL_STRIDED, regrouped))
    prev = None
    for n, (window, dil) in enumerate(reversed(DIL_PATTERNS)):
        if dil == 1:
            src, col0 = p0, DIL_COL0
        else:
            src, col0 = by_dil[dil].reshape(b, s // dil, dil * DIL_COLS), 0
        prev = _dilated_pattern(src, col0, b, window, dil, prev, final=n == len(DIL_PATTERNS) - 1)
    return prev


def _l0_out_ffn_kernel(x_ref, ret_ref, dil_ref, wo_ref, g_ref, wg_ref, wu_ref, wd_ref, o_ref,
                       x1_ref, hn_ref, acc_ref):
    j = pl.program_id(1)

    @pl.when(j == 0)
    def _():
        half = wo_ref.shape[0] // 2
        x1 = (x_ref[...]
              + jnp.dot(ret_ref[...], wo_ref[0:half, :], preferred_element_type=F32)
              + jnp.dot(dil_ref[...], wo_ref[half:, :], preferred_element_type=F32))
        x1_ref[...] = x1
        hn_ref[...] = _rms(x1, g_ref[...]).astype(BF16)
        acc_ref[...] = jnp.zeros_like(acc_ref)

    h = hn_ref[...]
    a = jnp.dot(h, wg_ref[...], preferred_element_type=F32)
    u = jnp.dot(h, wu_ref[...], preferred_element_type=F32)
    act = (a * jax.nn.sigmoid(a) * u).astype(BF16)
    acc_ref[...] += jnp.dot(act, wd_ref[...], preferred_element_type=F32)

    @pl.when(j == pl.num_programs(1) - 1)
    def _():
        o_ref[...] = x1_ref[...] + acc_ref[...]


def _l0_out_ffn(x2d, ret2d, dil2d, w_out, ffn_norm, w_gate, w_up, w_down, tm=512):
    t, d = x2d.shape
    f = w_gate.shape[1]
    tf = f // 2 if (f // 2) % LANES == 0 else f
    half = w_out.shape[0] // 2
    return pl.pallas_call(
        _l0_out_ffn_kernel,
        grid=(t // tm, f // tf),
        in_specs=[
            pl.BlockSpec((tm, d), lambda i, j: (i, 0)),
            pl.BlockSpec((tm, half), lambda i, j: (i, 0)),
            pl.BlockSpec((tm, half), lambda i, j: (i, 0)),
            pl.BlockSpec((2 * half, d), lambda i, j: (0, 0)),
            pl.BlockSpec((1, d), lambda i, j: (0, 0)),
            pl.BlockSpec((d, tf), lambda i, j: (0, j)),
            pl.BlockSpec((d, tf), lambda i, j: (0, j)),
            pl.BlockSpec((tf, d), lambda i, j: (j, 0)),
        ],
        out_specs=pl.BlockSpec((tm, d), lambda i, j: (i, 0)),
        out_shape=jax.ShapeDtypeStruct((t, d), F32),
        scratch_shapes=[pltpu.VMEM((tm, d), F32), pltpu.VMEM((tm, d), BF16), pltpu.VMEM((tm, d), F32)],
        compiler_params=_cparams(("parallel", "arbitrary")),
        name="l0_out_proj_ffn",
    )(x2d, ret2d, dil2d, w_out, ffn_norm.reshape(1, d), w_gate, w_up, w_down)


L1_W1_COLS = MLA_Q_RANK + MLA_KV_RANK + 2 * LANES + S5_CH
HEAD_BLOCK = LANES


def _l1_proj_kernel(x_ref, an_ref, w1_ref, qn_ref, wq_ref, kvn_ref, wkv_ref, ct_ref, st_ref,
                    q_out, k_out, v_out, u_out):
    xn = _rms(x_ref[0], an_ref[...]).astype(BF16)
    proj = jnp.dot(xn, w1_ref[...], preferred_element_type=F32)
    c0 = MLA_Q_RANK
    c1 = c0 + MLA_KV_RANK
    cq = proj[:, 0:c0]
    ckv = proj[:, c0:c1]
    ka = proj[:, c1:c1 + LANES]
    kb = proj[:, c1 + LANES:c1 + 2 * LANES]
    u_out[...] = proj[:, c1 + 2 * LANES:]

    ct = ct_ref[...]
    st = st_ref[...]
    ct8 = jnp.concatenate([ct] * MLA_HEADS, axis=1)
    st8 = jnp.concatenate([st] * MLA_HEADS, axis=1)
    width = MLA_HEADS * HEAD_BLOCK

    q2 = jnp.dot(_rms(cq, qn_ref[...]).astype(BF16), wq_ref[...], preferred_element_type=F32)
    scale = (MLA_NOPE + MLA_ROPE) ** -0.5 * math.log2(math.e)
    q_out[0] = ((q2[:, 0:width] * ct8 + q2[:, width:] * st8) * scale).astype(q_out.dtype)

    kv = jnp.dot(_rms(ckv, kvn_ref[...]).astype(BF16), wkv_ref[...], preferred_element_type=F32)
    krot = ka * ct + kb * st
    k_out[0] = (kv[:, 0:width] + jnp.concatenate([krot] * MLA_HEADS, axis=1)).astype(k_out.dtype)
    lane = lax.broadcasted_iota(jnp.int32, (1, width), 1)
    ones = jnp.where(lane % HEAD_BLOCK >= MLA_V, 1.0, 0.0)
    v_out[0] = (kv[:, width:] + ones).astype(v_out.dtype)


def _l1_weights(w_in, w_uq, w_ukv):
    c0 = MLA_Q_RANK
    c1 = c0 + MLA_KV_RANK
    c2 = c1 + MLA_ROPE
    hr = MLA_ROPE // 2
    kpe = w_in[:, c1:c2]
    zeros = lambda n: jnp.zeros((w_in.shape[0], n), w_in.dtype)
    ka = jnp.concatenate([zeros(MLA_NOPE), kpe, zeros(LANES - MLA_NOPE - MLA_ROPE)], axis=1)
    kb = jnp.concatenate([zeros(MLA_NOPE), kpe[:, hr:], kpe[:, :hr], zeros(LANES - MLA_NOPE - MLA_ROPE)], axis=1)
    w1 = jnp.concatenate([w_in[:, :c1], ka, kb, w_in[:, c2:]], axis=1).astype(BF16)

    qd = MLA_NOPE + MLA_ROPE
    wq = w_uq.reshape(MLA_Q_RANK, MLA_HEADS, qd)
    zq = lambda n: jnp.zeros((MLA_Q_RANK, MLA_HEADS, n), w_uq.dtype)
    qa = jnp.concatenate([wq, zq(HEAD_BLOCK - qd)], axis=2)
    qb = jnp.concatenate([zq(MLA_NOPE), wq[:, :, MLA_NOPE + hr:], wq[:, :, MLA_NOPE:MLA_NOPE + hr],
                          zq(HEAD_BLOCK - qd)], axis=2)
    wq2 = jnp.concatenate([qa.reshape(MLA_Q_RANK, -1), qb.reshape(MLA_Q_RANK, -1)], axis=1).astype(BF16)

    wkv = w_ukv.reshape(MLA_KV_RANK, MLA_HEADS, MLA_NOPE + MLA_V)
    kpart = jnp.concatenate([wkv[:, :, :MLA_NOPE],
                             jnp.zeros((MLA_KV_RANK, MLA_HEADS, HEAD_BLOCK - MLA_NOPE), w_ukv.dtype)], axis=2)
    vpart = jnp.concatenate([wkv[:, :, MLA_NOPE:],
                             jnp.zeros((MLA_KV_RANK, MLA_HEADS, HEAD_BLOCK - MLA_V), w_ukv.dtype)], axis=2)
    wkv2 = jnp.concatenate([kpart.reshape(MLA_KV_RANK, -1), vpart.reshape(MLA_KV_RANK, -1)], axis=1).astype(BF16)
    return w1, wq2, wkv2


def _rope_lane_tables(s):
    hr = MLA_ROPE // 2
    inv = ROPE_BASE ** (-jnp.arange(0, MLA_ROPE, 2, dtype=F32) / MLA_ROPE)
    ang = jnp.arange(s, dtype=F32)[:, None] * inv[None, :]
    cos, sin = jnp.cos(ang), jnp.sin(ang)
    pad = jnp.zeros((s, HEAD_BLOCK - MLA_NOPE - MLA_ROPE), F32)
    ct = jnp.concatenate([jnp.ones((s, MLA_NOPE), F32), cos, cos, pad], axis=1)
    st = jnp.concatenate([jnp.zeros((s, MLA_NOPE), F32), -sin, sin, pad], axis=1)
    assert ct.shape[1] == HEAD_BLOCK and hr * 2 == MLA_ROPE
    return ct, st


def _l1_proj(x, attn_norm, w1, q_norm, wq2, kv_norm, wkv2, ct, st, nb_pad, tm=512):
    b, s, d = x.shape
    width = MLA_HEADS * HEAD_BLOCK
    const = lambda shape: pl.BlockSpec(shape, lambda i, j: (0, 0))
    tok = lambda n: pl.BlockSpec((1, tm, n), lambda i, j: (i, j, 0))
    return pl.pallas_call(
        _l1_proj_kernel,
        grid=(b, s // tm),
        in_specs=[tok(d), const((1, d)), const(w1.shape), const((1, MLA_Q_RANK)), const(wq2.shape),
                  const((1, MLA_KV_RANK)), const(wkv2.shape),
                  pl.BlockSpec((tm, HEAD_BLOCK), lambda i, j: (j, 0)),
                  pl.BlockSpec((tm, HEAD_BLOCK), lambda i, j: (j, 0))],
        out_specs=[tok(width), tok(width), tok(width),
                   pl.BlockSpec((tm, S5_CH), lambda i, j: (j, i))],
        out_shape=[jax.ShapeDtypeStruct((b, s, width), BF16),
                   jax.ShapeDtypeStruct((b, s, width), BF16),
                   jax.ShapeDtypeStruct((b, s, width), BF16),
                   jax.ShapeDtypeStruct((s, nb_pad * S5_CH), F32)],
        compiler_params=_cparams(("parallel", "parallel")),
        name="l1_norm_in_proj",
    )(x, attn_norm.reshape(1, d), w1, q_norm.reshape(1, -1), wq2, kv_norm.reshape(1, -1), wkv2, ct, st)


def _mla_kernel(q_ref, k_ref, v_ref, o_ref, *, seq_len, tk, unroll):
    tq = q_ref.shape[1]
    qs = [q_ref[0, :, 0:HEAD_BLOCK], q_ref[0, :, HEAD_BLOCK:]]

    def body(j, carry):
        sl = pl.ds(pl.multiple_of(j * tk, tk), tk)
        new = []
        for hh in range(2):
            m, acc = carry[hh]
            hs = slice(hh * HEAD_BLOCK, (hh + 1) * HEAD_BLOCK)
            s = lax.dot_general(qs[hh], k_ref[0, sl, hs], (((1,), (1,)), ((), ())), preferred_element_type=F32)
            m_new = jnp.maximum(m, jnp.max(s, axis=-1, keepdims=True))
            p = jnp.exp2(s - m_new).astype(BF16)
            acc_new = jnp.exp2(m - m_new) * acc + jnp.dot(p, v_ref[0, sl, hs], preferred_element_type=F32)
            new.append((m_new, acc_new))
        return tuple(new)

    init = tuple((jnp.full((tq, 1), NEG_INF, F32), jnp.zeros((tq, HEAD_BLOCK), F32)) for _ in range(2))
    (_, a0), (_, a1) = lax.fori_loop(0, seq_len // tk, body, init, unroll=unroll)
    lane = lax.broadcasted_iota(jnp.int32, (tq, LANES), 1)
    o0 = a0 / pltpu.roll(a0, MLA_V, 1)
    o1 = a1 / pltpu.roll(a1, MLA_V, 1)
    o_ref[0] = jnp.where(lane < MLA_V, o0, pltpu.roll(o1, MLA_V, 1)).astype(o_ref.dtype)


def _mla_attention(q, k, v, tq=512, tk=2048, unroll=2):
    b, s, _ = q.shape
    hpairs = MLA_HEADS // 2
    pair = 2 * HEAD_BLOCK
    return pl.pallas_call(
        functools.partial(_mla_kernel, seq_len=s, tk=tk, unroll=unroll),
        grid=(b, hpairs, s // tq),
        in_specs=[pl.BlockSpec((1, tq, pair), lambda i, hp, j: (i, j, hp)),
                  pl.BlockSpec((1, s, pair), lambda i, hp, j: (i, 0, hp)),
                  pl.BlockSpec((1, s, pair), lambda i, hp, j: (i, 0, hp))],
        out_specs=pl.BlockSpec((1, tq, LANES), lambda i, hp, j: (i, j, hp)),
        out_shape=jax.ShapeDtypeStruct((b, s, MLA_HEADS * MLA_V), BF16),
        compiler_params=_cparams(("parallel", "parallel", "arbitrary")),
        name="l1_latent_attention",
    )(q, k, v)


S5_BLOCKS = S5_CH // LANES
S5_BLOCK_STATES = S5_NSTATE // S5_BLOCKS


def _s5_kernel(u_ref, bb_ref, are_ref, aim_ref, cb_ref, y_ref, bu_ref, x_ref, *, reverse, paired):
    tc, nb, _ = u_ref.shape
    n = S5_NSTATE
    sb = S5_BLOCK_STATES

    @pl.when(pl.program_id(0) == 0)
    def _():
        x_ref[...] = jnp.zeros_like(x_ref)

    u2 = u_ref[...].reshape(tc * nb, S5_CH)
    if paired:
        row = lax.broadcasted_iota(jnp.int32, (tc * nb, LANES), 0)
        is_fwd = (row % nb) < nb // 2
    for k in range(S5_BLOCKS):
        uk = u2[:, k * LANES:(k + 1) * LANES]
        if paired:
            zero = jnp.zeros_like(uk)
            uk = jnp.concatenate([jnp.where(is_fwd, uk, zero), jnp.where(is_fwd, zero, uk)], axis=1)
        r = jnp.dot(uk.astype(BF16), bb_ref[k], preferred_element_type=F32)
        bu_ref[:, :, k * sb:(k + 1) * sb] = r[:, 0:sb].reshape(tc, nb, sb)
        bu_ref[:, :, n + k * sb:n + (k + 1) * sb] = r[:, sb:].reshape(tc, nb, sb)

    def step(i, carry):
        t = tc - 1 - i if reverse else i
        xr, xi = carry
        a_re = are_ref[...]
        a_im = aim_ref[...]
        nr = a_re * xr - a_im * xi + bu_ref[t, :, 0:n]
        ni = a_re * xi + a_im * xr + bu_ref[t, :, n:]
        bu_ref[t, :, 0:n] = nr
        bu_ref[t, :, n:] = ni
        return nr, ni

    xr, xi = lax.fori_loop(0, tc, step, (x_ref[:, 0:n], x_ref[:, n:]), unroll=2)
    x_ref[:, 0:n] = xr
    x_ref[:, n:] = xi

    for k in range(S5_BLOCKS):
        xs = jnp.concatenate([bu_ref[:, :, k * sb:(k + 1) * sb], bu_ref[:, :, n + k * sb:n + (k + 1) * sb]], axis=2)
        yk = jnp.dot(xs.reshape(tc * nb, 2 * sb).astype(BF16), cb_ref[k], preferred_element_type=F32)
        if paired:
            yk = jnp.where(is_fwd, yk[:, 0:LANES], yk[:, LANES:])
        y_ref[:, :, k * LANES:(k + 1) * LANES] = yk.reshape(tc, nb, LANES)


def _s5_direction_params(lam_re, lam_im, log_step, b_re, b_im, c_re, c_im, nb):
    lam = lax.complex(jnp.minimum(lam_re.astype(F32), -1e-4), lam_im.astype(F32))
    step = jnp.exp(log_step.astype(F32))[:, None]
    lam_bar = jnp.exp(lam * step)
    bmat = lax.complex(b_re.astype(F32), b_im.astype(F32))
    b_bar = ((lam_bar - 1.0) / lam)[:, :, None] * bmat
    gpb = S5_GROUPS // S5_BLOCKS
    eye = jnp.eye(gpb, dtype=F32)

    def in_map(t):
        t = t.reshape(S5_BLOCKS, gpb, S5_STATE, S5_GROUP_CH)
        return jnp.einsum('kgpc,gh->kgchp', t, eye).reshape(S5_BLOCKS, LANES, S5_BLOCK_STATES)

    def out_map(t):
        t = t.reshape(S5_BLOCKS, gpb, S5_GROUP_CH, S5_STATE)
        return jnp.einsum('kgcp,gh->kgphc', t, eye).reshape(S5_BLOCKS, S5_BLOCK_STATES, LANES)

    bblk = jnp.concatenate([in_map(jnp.real(b_bar)), in_map(jnp.imag(b_bar))], axis=2).astype(BF16)
    cblk = jnp.concatenate([out_map(c_re.astype(F32)), -out_map(c_im.astype(F32))], axis=1).astype(BF16)
    a_re = jnp.broadcast_to(jnp.real(lam_bar).reshape(1, S5_NSTATE), (nb, S5_NSTATE)).astype(F32)
    a_im = jnp.broadcast_to(jnp.imag(lam_bar).reshape(1, S5_NSTATE), (nb, S5_NSTATE)).astype(F32)
    return bblk, a_re, a_im, cblk


def _s5_pair_params(fwd, bwd):
    half = fwd[1].shape[0] // 2
    return (jnp.concatenate([fwd[0], bwd[0]], axis=1),
            jnp.concatenate([fwd[1][:half], bwd[1][:half]], axis=0),
            jnp.concatenate([fwd[2][:half], bwd[2][:half]], axis=0),
            jnp.concatenate([fwd[3], bwd[3]], axis=2))


def _s5_scan(u_tm, params, reverse, paired=False, tc=64):
    s, nb, _ = u_tm.shape
    bdense, a_re, a_im, cdense = params
    nchunks = s // tc
    tmap = (lambda i: (nchunks - 1 - i, 0, 0)) if reverse else (lambda i: (i, 0, 0))
    const = lambda shape: pl.BlockSpec(shape, lambda i: (0,) * len(shape))
    return pl.pallas_call(
        functools.partial(_s5_kernel, reverse=reverse, paired=paired),
        grid=(nchunks,),
        in_specs=[pl.BlockSpec((tc, nb, S5_CH), tmap), const(bdense.shape), const(a_re.shape),
                  const(a_im.shape), const(cdense.shape)],
        out_specs=pl.BlockSpec((tc, nb, S5_CH), tmap),
        out_shape=jax.ShapeDtypeStruct((s, nb, S5_CH), F32),
        scratch_shapes=[pltpu.VMEM((tc, nb, 2 * S5_NSTATE), F32), pltpu.VMEM((nb, 2 * S5_NSTATE), F32)],
        compiler_params=_cparams(("arbitrary",)),
        name="l1_s5_scan_pair" if paired else ("l1_s5_scan_bwd" if reverse else "l1_s5_scan_fwd"),
    )(u_tm, bdense, a_re, a_im, cdense)


def _l1_out_router_kernel(x_ref, mla_ref, yf_ref, yb_ref, u_ref, dskip_ref, gw_ref, gb_ref, wo_ref,
                          fn_ref, rt_ref, x1_out, route_out):
    tm = x_ref.shape[1]
    u = u_ref[...]
    y = yf_ref[...] + yb_ref[...] + dskip_ref[...] * u
    z = jax.nn.gelu(y)
    gate = jax.nn.sigmoid(jnp.dot(z.astype(BF16), gw_ref[...], preferred_element_type=F32) + gb_ref[...])
    ssm = (z * gate).astype(BF16)
    half = wo_ref.shape[0] // 2
    x1 = (x_ref[0]
          + jnp.dot(mla_ref[0], wo_ref[0:half, :], preferred_element_type=F32)
          + jnp.dot(ssm, wo_ref[half:, :], preferred_element_type=F32))
    _store_token_slabs(x1_out.at[0], x1, tm)
    hn = _rms(x1, fn_ref[...])

    h_hi = hn.astype(BF16)
    h_lo = (hn - h_hi.astype(F32)).astype(BF16)
    logits = (jnp.dot(h_hi, rt_ref[0], preferred_element_type=F32)
              + jnp.dot(h_hi, rt_ref[1], preferred_element_type=F32)
              + jnp.dot(h_lo, rt_ref[0], preferred_element_type=F32))
    lane = lax.broadcasted_iota(jnp.int32, logits.shape, 1)
    lg = jnp.where(lane < N_EXPERTS, logits, -jnp.inf)
    m1 = jnp.max(lg, axis=-1, keepdims=True)
    i1 = jnp.min(jnp.where(lg == m1, lane, LANES), axis=-1, keepdims=True)
    lg2 = jnp.where(lane == i1, -jnp.inf, lg)
    m2 = jnp.max(lg2, axis=-1, keepdims=True)
    i2 = jnp.min(jnp.where(lg2 == m2, lane, LANES), axis=-1, keepdims=True)
    e2 = jnp.exp(m2 - m1)
    w1 = 1.0 / (1.0 + e2)
    w2 = e2 / (1.0 + e2)
    route_out[0] = jnp.where(lane == 0, i1.astype(F32),
                             jnp.where(lane == 1, i2.astype(F32),
                                       jnp.where(lane == 2, w1, jnp.where(lane == 3, w2, 0.0))))


def _l1_out_router(x, mla, yf, yb, u_tm, d_skip, glu_w, glu_b, w_out, ffn_norm, router_pad, tm=512):
    b, s, d = x.shape
    tok = lambda n: pl.BlockSpec((1, tm, n), lambda i, j: (i, j, 0))
    tmaj = pl.BlockSpec((tm, S5_CH), lambda i, j: (j, i))
    const = lambda shape: pl.BlockSpec(shape, lambda i, j: (0,) * len(shape))
    return pl.pallas_call(
        _l1_out_router_kernel,
        grid=(b, s // tm),
        in_specs=[tok(d), tok(MLA_HEADS * MLA_V), tmaj, tmaj, tmaj, const((1, S5_CH)),
                  const(glu_w.shape), const((1, S5_CH)), const(w_out.shape), const((1, d)),
                  const(router_pad.shape)],
        out_specs=[pl.BlockSpec((1, tm * SLAB_ROWS, LANES), lambda i, j: (i, j, 0)), tok(LANES)],
        out_shape=[jax.ShapeDtypeStruct((b, s * SLAB_ROWS, LANES), F32),
                   jax.ShapeDtypeStruct((b, s, LANES), F32)],
        compiler_params=_cparams(("parallel", "parallel")),
        name="l1_out_proj_router",
    )(x, mla, yf, yb, u_tm, d_skip.reshape(1, -1), glu_w, glu_b.reshape(1, -1), w_out,
      ffn_norm.reshape(1, d), router_pad)


def _start_row_gather(idx_ref, src_hbm, dst, sem, n_rows):
    def issue(r, carry):
        src = pl.multiple_of(idx_ref[r] * SLAB_ROWS, SLAB_ROWS)
        pltpu.make_async_copy(src_hbm.at[pl.ds(src, SLAB_ROWS)],
                              dst.at[pl.ds(pl.multiple_of(r * SLAB_ROWS, SLAB_ROWS), SLAB_ROWS)], sem).start()
        return carry

    lax.fori_loop(0, n_rows, issue, 0, unroll=8)


def _wait_row_gather(src_hbm, dst, sem, n_rows):
    pltpu.make_async_copy(src_hbm.at[pl.ds(0, n_rows * SLAB_ROWS)], dst, sem).wait()


def _expert_ffn_kernel(te_ref, nt_ref, idx0_ref, idxn_ref, x_hbm, gw_ref, g_ref, wg_ref, wu_ref, wd_ref,
                       o_ref, xbuf, sem, hn_ref, acc_ref):
    i = pl.program_id(0)
    j = pl.program_id(1)
    last = pl.num_programs(1) - 1
    tm = hn_ref.shape[0]
    n_used = nt_ref[0]
    slot = i % 2

    @pl.when(jnp.logical_and(i == 0, j == 0))
    def _():
        _start_row_gather(idx0_ref, x_hbm, xbuf.at[0], sem.at[0], tm)

    @pl.when(jnp.logical_and(j == 0, i < n_used))
    def _():
        _wait_row_gather(x_hbm, xbuf.at[slot], sem.at[slot], tm)
        x = _load_token_slabs(xbuf.at[slot], tm)
        hn_ref[...] = _rms(x, g_ref[...]).astype(BF16)
        acc_ref[...] = jnp.zeros_like(acc_ref)

    @pl.when(jnp.logical_and(j == 0, i + 1 < n_used))
    def _():
        _start_row_gather(idxn_ref, x_hbm, xbuf.at[1 - slot], sem.at[1 - slot], tm)

    @pl.when(i < n_used)
    def _():
        h = hn_ref[...]
        a = jnp.dot(h, wg_ref[0], preferred_element_type=F32)
        u = jnp.dot(h, wu_ref[0], preferred_element_type=F32)
        act = (a * jax.nn.sigmoid(a) * u).astype(BF16)
        acc_ref[...] += jnp.dot(act, wd_ref[0], preferred_element_type=F32)

        @pl.when(j == last)
        def _():
            _store_token_slabs(o_ref, gw_ref[...] * acc_ref[...], tm)

    @pl.when(jnp.logical_and(i >= n_used, j == last))
    def _():
        o_ref[...] = jnp.zeros_like(o_ref)


def _expert_ffn(x_slabs, src_tok, gate_w, tile_expert, n_tiles_used, ffn_norm, wg, wu, wd, tm, tf=1792):
    a_pad = src_tok.shape[0]
    n_tiles = a_pad // tm
    d = D_MODEL
    f = wg.shape[2]
    grid_spec = pltpu.PrefetchScalarGridSpec(
        num_scalar_prefetch=2,
        grid=(n_tiles, f // tf),
        in_specs=[
            pl.BlockSpec((tm,), lambda i, j, te, nt: (0,), memory_space=pltpu.SMEM),
            pl.BlockSpec((tm,), lambda i, j, te, nt: (jnp.minimum(i + 1, n_tiles - 1),), memory_space=pltpu.SMEM),
            pl.BlockSpec(memory_space=pl.ANY),
            pl.BlockSpec((tm, 1), lambda i, j, te, nt: (i, 0)),
            pl.BlockSpec((1, d), lambda i, j, te, nt: (0, 0)),
            pl.BlockSpec((1, d, tf), lambda i, j, te, nt: (te[i], 0, j)),
            pl.BlockSpec((1, d, tf), lambda i, j, te, nt: (te[i], 0, j)),
            pl.BlockSpec((1, tf, d), lambda i, j, te, nt: (te[i], j, 0)),
        ],
        out_specs=pl.BlockSpec((tm * SLAB_ROWS, LANES), lambda i, j, te, nt: (i, 0)),
        scratch_shapes=[pltpu.VMEM((2, tm * SLAB_ROWS, LANES), F32), pltpu.SemaphoreType.DMA((2,)),
                        pltpu.VMEM((tm, d), BF16), pltpu.VMEM((tm, d), F32)],
    )
    return pl.pallas_call(
        _expert_ffn_kernel,
        grid_spec=grid_spec,
        out_shape=jax.ShapeDtypeStruct((a_pad * SLAB_ROWS, LANES), F32),
        compiler_params=_cparams(("arbitrary", "arbitrary")),
        name="l1_expert_ffn",
    )(tile_expert, n_tiles_used, src_tok, src_tok, x_slabs, gate_w, ffn_norm.reshape(1, d), wg, wu, wd)


def _combine_norm_kernel(ia0_ref, ib0_ref, ian_ref, ibn_ref, x_ref, ys_hbm, g_ref, o_ref, buf, sem):
    tm = o_ref.shape[0]
    i = pl.program_id(0)
    slot = i % 2

    @pl.when(i == 0)
    def _():
        _start_row_gather(ia0_ref, ys_hbm, buf.at[0, 0], sem.at[0, 0], tm)
        _start_row_gather(ib0_ref, ys_hbm, buf.at[0, 1], sem.at[0, 1], tm)

    @pl.when(i + 1 < pl.num_programs(0))
    def _():
        _start_row_gather(ian_ref, ys_hbm, buf.at[1 - slot, 0], sem.at[1 - slot, 0], tm)
        _start_row_gather(ibn_ref, ys_hbm, buf.at[1 - slot, 1], sem.at[1 - slot, 1], tm)

    _wait_row_gather(ys_hbm, buf.at[slot, 0], sem.at[slot, 0], tm)
    _wait_row_gather(ys_hbm, buf.at[slot, 1], sem.at[slot, 1], tm)
    y = _load_token_slabs(x_ref, tm) + (_load_token_slabs(buf.at[slot, 0], tm)
                                        + _load_token_slabs(buf.at[slot, 1], tm))
    o_ref[...] = _rms(y, g_ref[...])


def _combine_norm(x_slabs, ys_slabs, dest, final_norm, tm=256):
    t = x_slabs.shape[0] // SLAB_ROWS
    nblk = t // tm
    slab = (tm * SLAB_ROWS, LANES)
    return pl.pallas_call(
        _combine_norm_kernel,
        grid=(nblk,),
        in_specs=[pl.BlockSpec((tm,), lambda i: (0,), memory_space=pltpu.SMEM),
                  pl.BlockSpec((tm,), lambda i: (nblk,), memory_space=pltpu.SMEM),
                  pl.BlockSpec((tm,), lambda i: (jnp.minimum(i + 1, nblk - 1),), memory_space=pltpu.SMEM),
                  pl.BlockSpec((tm,), lambda i: (jnp.minimum(i + 1, nblk - 1) + nblk,), memory_space=pltpu.SMEM),
                  pl.BlockSpec(slab, lambda i: (i, 0)),
                  pl.BlockSpec(memory_space=pl.ANY),
                  pl.BlockSpec((1, D_MODEL), lambda i: (0, 0))],
        out_specs=pl.BlockSpec((tm, D_MODEL), lambda i: (i, 0)),
        out_shape=jax.ShapeDtypeStruct((t, D_MODEL), F32),
        scratch_shapes=[pltpu.VMEM((2, 2) + slab, F32), pltpu.SemaphoreType.DMA((2, 2))],
        compiler_params=_cparams(("arbitrary",)),
        name="l1_combine_final_norm",
    )(dest, dest, dest, dest, x_slabs, ys_slabs, final_norm.reshape(1, D_MODEL))


def _moe(x_slabs, route, ffn_norm, wg, wu, wd, final_norm, tm=512):
    t = x_slabs.shape[0] // SLAB_ROWS
    e_idx = jnp.concatenate([route[:, 0], route[:, 1]]).astype(jnp.int32)
    e_w = jnp.concatenate([route[:, 2], route[:, 3]])
    n_assign = TOP_K * t
    order = jnp.argsort(e_idx, stable=True).astype(jnp.int32)
    inv = jnp.argsort(order).astype(jnp.int32)
    counts = jnp.sum(e_idx[:, None] == jnp.arange(N_EXPERTS, dtype=jnp.int32)[None, :], axis=0).astype(jnp.int32)
    starts = jnp.cumsum(counts) - counts
    padded = ((counts + tm - 1) // tm) * tm
    pad_ends = jnp.cumsum(padded)
    pad_starts = pad_ends - padded
    a_pad = n_assign + N_EXPERTS * tm
    n_tiles = a_pad // tm

    tile_start = jnp.arange(n_tiles, dtype=jnp.int32) * tm
    tile_expert = jnp.minimum(jnp.sum(tile_start[:, None] >= pad_ends[None, :], axis=1), N_EXPERTS - 1).astype(jnp.int32)
    n_tiles_used = (pad_ends[-1] // tm).astype(jnp.int32).reshape(1)

    slot = jnp.arange(a_pad, dtype=jnp.int32)
    slot_e = jnp.repeat(tile_expert, tm)
    within = slot - pad_starts[slot_e]
    valid = within < counts[slot_e]
    src = order[jnp.clip(starts[slot_e] + within, 0, n_assign - 1)]
    src_tok = jnp.where(valid, src % t, 0).astype(jnp.int32)
    gate_w = jnp.where(valid, e_w[src], 0.0).astype(F32).reshape(a_pad, 1)

    ys = _expert_ffn(x_slabs, src_tok, gate_w, tile_expert, n_tiles_used, ffn_norm, wg, wu, wd, tm)
    dest = (pad_starts[e_idx] + inv - starts[e_idx]).astype(jnp.int32)
    return _combine_norm(x_slabs, ys, dest, final_norm)


def _l0_in_weight(w_in):
    hq = RET_HEADS * RET_DK
    hv = RET_HEADS * RET_DV
    hd = DIL_HEADS * DIL_DH
    rq, rk, rv, rg, dq, dk, dv = jnp.split(
        w_in, [hq, 2 * hq, 2 * hq + hv, 2 * hq + 2 * hv, 2 * hq + 2 * hv + hd, 2 * hq + 2 * hv + 2 * hd], axis=1)
    dup = lambda t: jnp.concatenate([t.reshape(-1, RET_HEADS, 1, RET_DK)] * 2, axis=2).reshape(-1, 2 * hq)
    w = jnp.concatenate([dup(rq) * RET_DK ** -0.5, dup(rk), rv, rg, dq * DIL_DH ** -0.5, dk, dv], axis=1)
    assert w.shape[1] == L0_COLS
    return w.astype(BF16)


def _prepare(l0_w_in, l0_ret_decay_f, l0_ret_decay_b, l0_w_out, l0_ffn_w_gate, l0_ffn_w_up, l0_ffn_w_down,
             l1_w_in, l1_mla_w_uq, l1_mla_w_ukv, l1_s5_glu_w, l1_w_out, l1_router,
             l1_exp_w_gate, l1_exp_w_up, l1_exp_w_down):
    w1, wq2, wkv2 = _l1_weights(l1_w_in, l1_mla_w_uq, l1_mla_w_ukv)
    router_f32 = jnp.zeros((D_MODEL, LANES), F32).at[:, :N_EXPERTS].set(l1_router.astype(F32))
    router_hi = router_f32.astype(BF16)
    router_pad = jnp.stack([router_hi, (router_f32 - router_hi.astype(F32)).astype(BF16)])
    return dict(
        l0_w_in=_l0_in_weight(l0_w_in),
        ret_tables=_retention_tables(l0_ret_decay_f, l0_ret_decay_b),
        l0_w_out=l0_w_out.astype(BF16),
        l0_wg=l0_ffn_w_gate.astype(BF16), l0_wu=l0_ffn_w_up.astype(BF16), l0_wd=l0_ffn_w_down.astype(BF16),
        l1_w1=w1, l1_wq2=wq2, l1_wkv2=wkv2,
        glu_w=l1_s5_glu_w.astype(BF16), l1_w_out=l1_w_out.astype(BF16), router_pad=router_pad,
        exp_wg=l1_exp_w_gate.astype(BF16), exp_wu=l1_exp_w_up.astype(BF16), exp_wd=l1_exp_w_down.astype(BF16),
    )


def _trunk(x, prep, p):
    b, s, d = x.shape
    t = b * s
    p0, *regrouped = _l0_in_proj(x.reshape(t, d), p['l0_attn_norm'], prep['l0_w_in'])
    p0 = p0.reshape(b, s, L0_COLS)
    ret = _retention(p0, prep['ret_tables'], p['l0_ret_gn'])
    dil = _dilated_attention(p0, regrouped)
    x = _l0_out_ffn(x.reshape(t, d), ret.reshape(t, -1), dil.reshape(t, -1), prep['l0_w_out'],
                    p['l0_ffn_norm'], prep['l0_wg'], prep['l0_wu'], prep['l0_wd']).reshape(b, s, d)
    nb = -(-b // SUBLANES) * SUBLANES
    ct, st = _rope_lane_tables(s)
    q, k, v, u_tm = _l1_proj(x, p['l1_attn_norm'], prep['l1_w1'], p['l1_mla_q_norm'], prep['l1_wq2'],
                             p['l1_mla_kv_norm'], prep['l1_wkv2'], ct, st, b)
    mla = _mla_attention(q, k, v)
    u3 = u_tm.reshape(s, b, S5_CH)
    sf = _s5_direction_params(p['l1_s5_lam_re_f'], p['l1_s5_lam_im_f'], p['l1_s5_log_step_f'],
                              p['l1_s5_b_re'], p['l1_s5_b_im'], p['l1_s5_c_re'], p['l1_s5_c_im'], nb)
    sb = _s5_direction_params(p['l1_s5_lam_re_b'], p['l1_s5_lam_im_b'], p['l1_s5_log_step_b'],
                              p['l1_s5_b_re'], p['l1_s5_b_im'], p['l1_s5_c_re'], p['l1_s5_c_im'], nb)
    if 2 * b == SUBLANES:
        y = _s5_scan(jnp.concatenate([u3, jnp.flip(u3, axis=0)], axis=1), _s5_pair_params(sf, sb),
                     reverse=False, paired=True)
        yf = y[:, :b].reshape(s, b * S5_CH)
        yb = jnp.flip(y[:, b:], axis=0).reshape(s, b * S5_CH)
    else:
        if nb != b:
            u3 = jnp.pad(u3, ((0, 0), (0, nb - b), (0, 0)))
        yf = _s5_scan(u3, sf, reverse=False).reshape(s, nb * S5_CH)
        yb = _s5_scan(u3, sb, reverse=True).reshape(s, nb * S5_CH)
    x1s, route = _l1_out_router(x, mla, yf, yb, u_tm, p['l1_s5_d'], prep['glu_w'], p['l1_s5_glu_b'],
                                prep['l1_w_out'], p['l1_ffn_norm'], prep['router_pad'])
    out = _moe(x1s.reshape(t * SLAB_ROWS, LANES), route.reshape(t, LANES), p['l1_ffn_norm'],
               prep['exp_wg'], prep['exp_wu'], prep['exp_wd'], p['final_norm'])
    return out.reshape(b, s, d)


def kernel(x_prompt, x_sample, l0_attn_norm, l0_w_in, l0_ret_decay_f, l0_ret_decay_b, l0_ret_gn, l0_w_out,
           l0_ffn_norm, l0_ffn_w_gate, l0_ffn_w_up, l0_ffn_w_down, l1_attn_norm, l1_w_in, l1_mla_q_norm,
           l1_mla_w_uq, l1_mla_kv_norm, l1_mla_w_ukv, l1_s5_lam_re_f, l1_s5_lam_im_f, l1_s5_log_step_f,
           l1_s5_lam_re_b, l1_s5_lam_im_b, l1_s5_log_step_b, l1_s5_b_re, l1_s5_b_im, l1_s5_c_re, l1_s5_c_im,
           l1_s5_d, l1_s5_glu_w, l1_s5_glu_b, l1_w_out, l1_ffn_norm, l1_router, l1_exp_w_gate, l1_exp_w_up,
           l1_exp_w_down, final_norm):
    p = dict(locals())
    prep = _prepare(l0_w_in, l0_ret_decay_f, l0_ret_decay_b, l0_w_out, l0_ffn_w_gate, l0_ffn_w_up,
                    l0_ffn_w_down, l1_w_in, l1_mla_w_uq, l1_mla_w_ukv, l1_s5_glu_w, l1_w_out, l1_router,
                    l1_exp_w_gate, l1_exp_w_up, l1_exp_w_down)
    return (_trunk(x_prompt, prep, p), _trunk(x_sample, prep, p))
```

```python
import functools
import math

import jax
import jax.numpy as jnp
from jax import lax
from jax.experimental import pallas as pl
from jax.experimental.pallas import tpu as pltpu

F32 = jnp.float32
BF16 = jnp.bfloat16

D_MODEL = 1024
EPS = 1e-6
NEG_INF = -1e30
RET_HEADS = 4
RET_DK = 64
RET_DV = 128
RET_CHUNK = 128
RET_UNROLL = 16
DIL_HEADS = 8
DIL_DH = 64
DIL_PATTERNS = ((128, 1), (512, 4), (2048, 16))
DIL_QBLOCK = 128
DIL_UNROLL = 16
MLA_HEADS = 8
MLA_Q_RANK = 256
MLA_KV_RANK = 128
MLA_NOPE = 64
MLA_ROPE = 32
MLA_V = 64
ROPE_BASE = 10000.0
S5_GROUPS = 32
S5_GROUP_CH = 16
S5_STATE = 64
S5_CH = S5_GROUPS * S5_GROUP_CH
S5_NSTATE = S5_GROUPS * S5_STATE
N_EXPERTS = 8
TOP_K = 2

LANES = 128
SUBLANES = 8
VMEM_LIMIT = 56 * 1024 * 1024

L0_COLS = 3584
L0_BLOCKS = L0_COLS // LANES
L0_RQ, L0_RK, L0_RV, L0_RG, L0_DQ, L0_DK, L0_DV = 0, 4, 8, 12, 16, 20, 24


def _cparams(sem):
    return pltpu.CompilerParams(dimension_semantics=sem, vmem_limit_bytes=VMEM_LIMIT)


def _rms(x, g):
    return x * lax.rsqrt(jnp.mean(x * x, axis=-1, keepdims=True) + EPS) * g


SLAB_ROWS = D_MODEL // LANES


def _store_token_slabs(ref2d, x, n_tok):
    for s in range(SLAB_ROWS):
        ref2d[pl.ds(s, n_tok, stride=SLAB_ROWS), :] = x[:, s * LANES:(s + 1) * LANES]


def _load_token_slabs(ref2d, n_tok):
    return jnp.concatenate([ref2d[pl.ds(s, n_tok, stride=SLAB_ROWS), :] for s in range(SLAB_ROWS)], axis=1)


DIL_COLS = 3 * DIL_HEADS * DIL_DH
DIL_COL0 = L0_DQ * LANES
DIL_STRIDED = tuple(dil for _, dil in DIL_PATTERNS if dil > 1)


def _l0_in_proj_kernel(x_ref, g_ref, w_ref, o_ref, *rest):
    dil_refs, dsc = rest[:-1], rest[-1]
    tm = x_ref.shape[0]
    xn = _rms(x_ref[...], g_ref[...]).astype(BF16)
    res = jnp.dot(xn, w_ref[...], preferred_element_type=F32)
    o_ref[...] = res.astype(o_ref.dtype)
    for c in range(DIL_COLS // LANES):
        dsc[c] = res[:, DIL_COL0 + c * LANES:DIL_COL0 + (c + 1) * LANES]
    for ref, dil in zip(dil_refs, DIL_STRIDED):
        for r in range(dil):
            for c in range(DIL_COLS // LANES):
                lo = r * DIL_COLS + c * LANES
                ref[:, lo:lo + LANES] = dsc[c, pl.ds(r, tm // dil, stride=dil), :].astype(ref.dtype)


def _l0_in_proj(x2d, gain, w, tm=512):
    t, d = x2d.shape
    n = w.shape[1]
    dil_specs = [pl.BlockSpec((tm // dil, dil * DIL_COLS), lambda i: (i, 0)) for dil in DIL_STRIDED]
    dil_shapes = [jax.ShapeDtypeStruct((t // dil, dil * DIL_COLS), BF16) for dil in DIL_STRIDED]
    return pl.pallas_call(
        _l0_in_proj_kernel,
        grid=(t // tm,),
        in_specs=[
            pl.BlockSpec((tm, d), lambda i: (i, 0)),
            pl.BlockSpec((1, d), lambda i: (0, 0)),
            pl.BlockSpec((d, n), lambda i: (0, 0)),
        ],
        out_specs=[pl.BlockSpec((tm, n), lambda i: (i, 0))] + dil_specs,
        out_shape=[jax.ShapeDtypeStruct((t, n), BF16)] + dil_shapes,
        scratch_shapes=[pltpu.VMEM((DIL_COLS // LANES, tm, LANES), F32)],
        compiler_params=_cparams(("parallel",)),
        name="l0_norm_in_proj",
    )(x2d, gain.reshape(1, d), w)


def _retention_kernel(q_ref, k_ref, v_ref, g_ref, d_ref, qw_ref, kw_ref, cd_ref, gn_ref,
                      o_ref, kv_ref, p_ref, *, n_chunks):
    c = RET_CHUNK
    kw = kw_ref[0]
    qw = qw_ref[0]
    dmat = d_ref[0]
    gn = gn_ref[...]

    def kv_body(n, carry):
        sl = pl.ds(pl.multiple_of(n * c, c), c)
        kc = (k_ref[0, sl, :].astype(F32) * kw).T.astype(BF16)
        kv_ref[n] = jnp.dot(kc, v_ref[0, sl, :], preferred_element_type=F32)
        return carry

    lax.fori_loop(0, n_chunks, kv_body, 0, unroll=RET_UNROLL)

    half = c // 2
    dec_f = cd_ref[0, 0:half, :]
    dec_b = cd_ref[0, half:c, :]

    def fwd_body(n, s):
        p_ref[n, 0:half, :] = s.astype(BF16)
        return s * dec_f + kv_ref[n, 0:half, :]

    lax.fori_loop(0, n_chunks, fwd_body, jnp.zeros((half, RET_DV), F32))

    def bwd_body(i, s):
        n = n_chunks - 1 - i
        p_ref[n, half:c, :] = s.astype(BF16)
        return s * dec_b + kv_ref[n, half:c, :]

    lax.fori_loop(0, n_chunks, bwd_body, jnp.zeros((half, RET_DV), F32))

    def out_body(n, carry):
        sl = pl.ds(pl.multiple_of(n * c, c), c)
        qc = q_ref[0, sl, :]
        s = lax.dot_general(qc, k_ref[0, sl, :], (((1,), (1,)), ((), ())), preferred_element_type=F32)
        intra = jnp.dot((s * dmat).astype(BF16), v_ref[0, sl, :], preferred_element_type=F32)
        qq = (qc.astype(F32) * qw).astype(BF16)
        y = intra + jnp.dot(qq, p_ref[n], preferred_element_type=F32)
        mu = jnp.mean(y, axis=-1, keepdims=True)
        yc = y - mu
        var = jnp.mean(yc * yc, axis=-1, keepdims=True)
        yn = yc * lax.rsqrt(var + EPS) * gn
        gg = g_ref[0, sl, :].astype(F32)
        o_ref[0, sl, :] = (gg * jax.nn.sigmoid(gg) * yn).astype(o_ref.dtype)
        return carry

    lax.fori_loop(0, n_chunks, out_body, 0, unroll=RET_UNROLL)


def _retention_tables(decay_f, decay_b):
    c = RET_CHUNK
    lg_f = jax.nn.log_sigmoid(decay_f.astype(F32))[:, None, None]
    lg_b = jax.nn.log_sigmoid(decay_b.astype(F32))[:, None, None]
    j = jnp.arange(c, dtype=F32)
    diff = j[:, None] - j[None, :]
    dmat = 0.5 * jnp.where(diff >= 0, jnp.exp(lg_f * jnp.maximum(diff, 0.0)),
                           jnp.exp(lg_b * jnp.maximum(-diff, 0.0)))
    lane_f = (jnp.arange(LANES) < RET_DK)[None, None, :]
    jj = j[None, :, None]
    qw = jnp.where(lane_f, jnp.exp(lg_f * (jj + 1.0)), jnp.exp(lg_b * (c - jj)))
    kw = jnp.where(lane_f, jnp.exp(lg_f * (c - 1.0 - jj)), jnp.exp(lg_b * jj))
    row_f = (jnp.arange(c) < c // 2)[None, :, None]
    cd = jnp.where(row_f, jnp.exp(lg_f * c), jnp.exp(lg_b * c)) * jnp.ones((1, 1, RET_DV), F32)
    return dmat.astype(F32), qw.astype(F32), kw.astype(F32), cd.astype(F32)


def _retention(p0, tables, gn):
    b, s, _ = p0.shape
    dmat, qw, kw, cd = tables
    n_chunks = s // RET_CHUNK
    seq = lambda col: pl.BlockSpec((1, s, LANES), lambda i, h: (i, 0, col + h))
    tab = pl.BlockSpec((1, RET_CHUNK, LANES), lambda i, h: (h, 0, 0))
    return pl.pallas_call(
        functools.partial(_retention_kernel, n_chunks=n_chunks),
        grid=(b, RET_HEADS),
        in_specs=[seq(L0_RQ), seq(L0_RK), seq(L0_RV), seq(L0_RG), tab, tab, tab, tab,
                  pl.BlockSpec((1, LANES), lambda i, h: (0, h))],
        out_specs=pl.BlockSpec((1, s, LANES), lambda i, h: (i, 0, h)),
        out_shape=jax.ShapeDtypeStruct((b, s, RET_HEADS * RET_DV), BF16),
        scratch_shapes=[pltpu.VMEM((n_chunks, RET_CHUNK, RET_DV), F32),
                        pltpu.VMEM((n_chunks, RET_CHUNK, RET_DV), BF16)],
        compiler_params=_cparams(("parallel", "parallel")),
        name="l0_retention",
    )(p0, p0, p0, p0, dmat, qw, kw, cd, gn.reshape(1, -1))


def _dilated_kernel(*refs, seq_len, half, q_rows, n_hp, has_prev, final):
    q_ref, k_ref, v_ref, bias_ref = refs[:4]
    refs = refs[4:]
    ratio = DIL_CHAIN_RATIO
    if has_prev:
        prev_o, prev_l, refs = refs[:ratio], refs[ratio:2 * ratio], refs[2 * ratio:]
        po_ref, pl_ref = refs[-2:]
        refs = refs[:-2]
        for m in range(ratio):
            for hp in range(n_hp):
                ls = slice(hp * LANES, (hp + 1) * LANES)
                po_ref[hp, pl.ds(m, q_rows // ratio, stride=ratio), :] = prev_o[m][0, :, ls]
                pl_ref[hp, pl.ds(m, q_rows // ratio, stride=ratio), :] = prev_l[m][0, :, ls]
    o_ref = refs[0]
    lse_ref = None if final else refs[1]
    tq = DIL_QBLOCK
    kw = tq + 2 * half
    base = pl.program_id(3) * q_rows
    lane = lax.broadcasted_iota(jnp.int32, (tq, LANES), 1)
    lo = lane < DIL_DH

    def blk(i, carry):
        off = pl.multiple_of(i * tq, tq)
        rows = pl.ds(off, tq)
        qs = base + off
        ks = pl.multiple_of(jnp.clip(qs - half, 0, seq_len - kw), half)
        shift = (qs - ks) // half
        for hp in range(n_hp):
            ls = slice(hp * LANES, (hp + 1) * LANES)
            q = q_ref[0, rows, ls]
            kwin = k_ref[0, pl.ds(ks, kw), ls]
            vwin = v_ref[0, pl.ds(ks, kw), ls]
            zero = jnp.zeros_like(q)
            q2 = jnp.concatenate([jnp.where(lo, q, zero), jnp.where(lo, zero, q)], axis=0)
            s = lax.dot_general(q2, kwin, (((1,), (1,)), ((), ())), preferred_element_type=F32)
            s = s + bias_ref[hp, shift]
            m = jnp.max(s, axis=-1, keepdims=True)
            p = jnp.exp(s - m)
            den = jnp.sum(p, axis=-1, keepdims=True)
            pv = jnp.dot(p.astype(BF16), vwin, preferred_element_type=F32) / den
            lse2 = m + jnp.log(den)
            o = jnp.where(lo, pv[0:tq], pv[tq:])
            lse = jnp.where(lo, lse2[0:tq], lse2[tq:])
            if has_prev:
                lse_p = pl_ref[hp, rows, :]
                top = jnp.maximum(lse_p, lse)
                wa = jnp.exp(lse_p - top)
                wb = jnp.exp(lse - top)
                den = wa + wb
                o = (wa * po_ref[hp, rows, :] + wb * o) / den
                lse = top + jnp.log(den)
            o_ref[0, rows, ls] = o.astype(o_ref.dtype)
            if not final:
                lse_ref[0, rows, ls] = lse
        return carry

    n_blk = q_rows // tq
    lax.fori_loop(0, n_blk, blk, 0, unroll=max(1, min(DIL_UNROLL // n_hp, n_blk)))


DIL_MAX_RESIDENT_ROWS = 2048


DIL_CHAIN_RATIO = 4
assert all(DIL_PATTERNS[n + 1][1] == DIL_CHAIN_RATIO * DIL_PATTERNS[n][1] for n in range(len(DIL_PATTERNS) - 1))


def _dilated_pattern(src, col0, batch, window, dil, prev, final):
    width = DIL_HEADS * DIL_DH
    seq_len = src.shape[1]
    group_blocks = src.shape[2] // dil // LANES
    half = window // (2 * dil)
    q_rows = min(seq_len, 1024)
    hpairs = DIL_HEADS // 2
    n_hp = hpairs if seq_len <= DIL_MAX_RESIDENT_ROWS else 1
    lanes = n_hp * LANES
    hp_blocks = hpairs // n_hp
    c0 = col0 // LANES
    slopes = (jnp.exp2(-8.0 * jnp.arange(1, DIL_HEADS + 1, dtype=F32) / DIL_HEADS) * dil).reshape(hpairs, 2, 1, 1, 1)
    kw = DIL_QBLOCK + 2 * half
    shift = jnp.arange(3, dtype=jnp.int32)[:, None, None] * half
    dist = jnp.abs(jnp.arange(DIL_QBLOCK, dtype=jnp.int32)[None, :, None] + shift
                   - jnp.arange(kw, dtype=jnp.int32)[None, None, :])
    bias = jnp.where(dist <= half, -(slopes * dist.astype(F32)), NEG_INF).astype(F32)
    bias = bias.transpose(0, 2, 1, 3, 4).reshape(hpairs, 3, 2 * DIL_QBLOCK, kw)
    col = lambda part: (lambda i, r, hp, j: (i, 0, (r * group_blocks + c0 + part * hpairs) // n_hp + hp))
    qcol = lambda i, r, hp, j: (i, j, (r * group_blocks + c0) // n_hp + hp)
    oblk = pl.BlockSpec((1, q_rows, lanes), lambda i, r, hp, j: (i, j, r * hp_blocks + hp))
    in_specs = [pl.BlockSpec((1, q_rows, lanes), qcol),
                pl.BlockSpec((1, seq_len, lanes), col(1)),
                pl.BlockSpec((1, seq_len, lanes), col(2)),
                pl.BlockSpec((n_hp, 3, 2 * DIL_QBLOCK, kw), lambda i, r, hp, j: (hp, 0, 0, 0))]
    args = [src, src, src, bias]
    scratch = []
    if prev is not None:
        ratio = DIL_CHAIN_RATIO
        pblk = lambda m: pl.BlockSpec((1, q_rows // ratio, lanes),
                                      lambda i, r, hp, j: (i, j, (m * dil + r) * hp_blocks + hp))
        for a in prev:
            in_specs += [pblk(m) for m in range(ratio)]
            args += [a] * ratio
        scratch = [pltpu.VMEM((n_hp, q_rows, LANES), F32)] * 2
    oshape = (batch, seq_len, dil * width)
    if final:
        out_specs, out_shape = oblk, jax.ShapeDtypeStruct(oshape, BF16)
    else:
        out_specs, out_shape = [oblk, oblk], [jax.ShapeDtypeStruct(oshape, F32)] * 2
    return pl.pallas_call(
        functools.partial(_dilated_kernel, seq_len=seq_len, half=half, q_rows=q_rows, n_hp=n_hp,
                          has_prev=prev is not None, final=final),
        grid=(batch, dil, hp_blocks, seq_len // q_rows),
        in_specs=in_specs,
        out_specs=out_specs,
        out_shape=out_shape,
        scratch_shapes=scratch,
        compiler_params=_cparams(("parallel", "parallel", "parallel", "arbitrary")),
        name=f"l0_dilated_d{dil}",
    )(*args)


def _dilated_attention(p0, regrouped):
    b, s, _ = p0.shape
    by_dil = dict(zip(DIL_STRIDED, regrouped))
    prev = None
    for n, (window, dil) in enumerate(reversed(DIL_PATTERNS)):
        if dil == 1:
            src, col0 = p0, DIL_COL0
        else:
            src, col0 = by_dil[dil].reshape(b, s // dil, dil * DIL_COLS), 0
        prev = _dilated_pattern(src, col0, b, window, dil, prev, final=n == len(DIL_PATTERNS) - 1)
    return prev


def _l0_out_ffn_kernel(x_ref, ret_ref, dil_ref, wo_ref, g_ref, wg_ref, wu_ref, wd_ref, o_ref,
                       x1_ref, hn_ref, acc_ref):
    j = pl.program_id(1)

    @pl.when(j == 0)
    def _():
        half = wo_ref.shape[0] // 2
        x1 = (x_ref[...]
              + jnp.dot(ret_ref[...], wo_ref[0:half, :], preferred_element_type=F32)
              + jnp.dot(dil_ref[...], wo_ref[half:, :], preferred_element_type=F32))
        x1_ref[...] = x1
        hn_ref[...] = _rms(x1, g_ref[...]).astype(BF16)
        acc_ref[...] = jnp.zeros_like(acc_ref)

    h = hn_ref[...]
    a = jnp.dot(h, wg_ref[...], preferred_element_type=F32)
    u = jnp.dot(h, wu_ref[...], preferred_element_type=F32)
    act = (a * jax.nn.sigmoid(a) * u).astype(BF16)
    acc_ref[...] += jnp.dot(act, wd_ref[...], preferred_element_type=F32)

    @pl.when(j == pl.num_programs(1) - 1)
    def _():
        o_ref[...] = x1_ref[...] + acc_ref[...]


def _l0_out_ffn(x2d, ret2d, dil2d, w_out, ffn_norm, w_gate, w_up, w_down, tm=512):
    t, d = x2d.shape
    f = w_gate.shape[1]
    tf = f // 2 if (f // 2) % LANES == 0 else f
    half = w_out.shape[0] // 2
    return pl.pallas_call(
        _l0_out_ffn_kernel,
        grid=(t // tm, f // tf),
        in_specs=[
            pl.BlockSpec((tm, d), lambda i, j: (i, 0)),
            pl.BlockSpec((tm, half), lambda i, j: (i, 0)),
            pl.BlockSpec((tm, half), lambda i, j: (i, 0)),
            pl.BlockSpec((2 * half, d), lambda i, j: (0, 0)),
            pl.BlockSpec((1, d), lambda i, j: (0, 0)),
            pl.BlockSpec((d, tf), lambda i, j: (0, j)),
            pl.BlockSpec((d, tf), lambda i, j: (0, j)),
            pl.BlockSpec((tf, d), lambda i, j: (j, 0)),
        ],
        out_specs=pl.BlockSpec((tm, d), lambda i, j: (i, 0)),
        out_shape=jax.ShapeDtypeStruct((t, d), F32),
        scratch_shapes=[pltpu.VMEM((tm, d), F32), pltpu.VMEM((tm, d), BF16), pltpu.VMEM((tm, d), F32)],
        compiler_params=_cparams(("parallel", "arbitrary")),
        name="l0_out_proj_ffn",
    )(x2d, ret2d, dil2d, w_out, ffn_norm.reshape(1, d), w_gate, w_up, w_down)


L1_W1_COLS = MLA_Q_RANK + MLA_KV_RANK + 2 * LANES + S5_CH
HEAD_BLOCK = LANES


def _l1_proj_kernel(x_ref, an_ref, w1_ref, qn_ref, wq_ref, kvn_ref, wkv_ref, ct_ref, st_ref,
                    q_out, k_out, v_out, u_out):
    xn = _rms(x_ref[0], an_ref[...]).astype(BF16)
    proj = jnp.dot(xn, w1_ref[...], preferred_element_type=F32)
    c0 = MLA_Q_RANK
    c1 = c0 + MLA_KV_RANK
    cq = proj[:, 0:c0]
    ckv = proj[:, c0:c1]
    ka = proj[:, c1:c1 + LANES]
    kb = proj[:, c1 + LANES:c1 + 2 * LANES]
    u_out[...] = proj[:, c1 + 2 * LANES:]

    ct = ct_ref[...]
    st = st_ref[...]
    ct8 = jnp.concatenate([ct] * MLA_HEADS, axis=1)
    st8 = jnp.concatenate([st] * MLA_HEADS, axis=1)
    width = MLA_HEADS * HEAD_BLOCK

    q2 = jnp.dot(_rms(cq, qn_ref[...]).astype(BF16), wq_ref[...], preferred_element_type=F32)
    scale = (MLA_NOPE + MLA_ROPE) ** -0.5 * math.log2(math.e)
    q_out[0] = ((q2[:, 0:width] * ct8 + q2[:, width:] * st8) * scale).astype(q_out.dtype)

    kv = jnp.dot(_rms(ckv, kvn_ref[...]).astype(BF16), wkv_ref[...], preferred_element_type=F32)
    krot = ka * ct + kb * st
    k_out[0] = (kv[:, 0:width] + jnp.concatenate([krot] * MLA_HEADS, axis=1)).astype(k_out.dtype)
    lane = lax.broadcasted_iota(jnp.int32, (1, width), 1)
    ones = jnp.where(lane % HEAD_BLOCK >= MLA_V, 1.0, 0.0)
    v_out[0] = (kv[:, width:] + ones).astype(v_out.dtype)


def _l1_weights(w_in, w_uq, w_ukv):
    c0 = MLA_Q_RANK
    c1 = c0 + MLA_KV_RANK
    c2 = c1 + MLA_ROPE
    hr = MLA_ROPE // 2
    kpe = w_in[:, c1:c2]
    zeros = lambda n: jnp.zeros((w_in.shape[0], n), w_in.dtype)
    ka = jnp.concatenate([zeros(MLA_NOPE), kpe, zeros(LANES - MLA_NOPE - MLA_ROPE)], axis=1)
    kb = jnp.concatenate([zeros(MLA_NOPE), kpe[:, hr:], kpe[:, :hr], zeros(LANES - MLA_NOPE - MLA_ROPE)], axis=1)
    w1 = jnp.concatenate([w_in[:, :c1], ka, kb, w_in[:, c2:]], axis=1).astype(BF16)

    qd = MLA_NOPE + MLA_ROPE
    wq = w_uq.reshape(MLA_Q_RANK, MLA_HEADS, qd)
    zq = lambda n: jnp.zeros((MLA_Q_RANK, MLA_HEADS, n), w_uq.dtype)
    qa = jnp.concatenate([wq, zq(HEAD_BLOCK - qd)], axis=2)
    qb = jnp.concatenate([zq(MLA_NOPE), wq[:, :, MLA_NOPE + hr:], wq[:, :, MLA_NOPE:MLA_NOPE + hr],
                          zq(HEAD_BLOCK - qd)], axis=2)
    wq2 = jnp.concatenate([qa.reshape(MLA_Q_RANK, -1), qb.reshape(MLA_Q_RANK, -1)], axis=1).astype(BF16)

    wkv = w_ukv.reshape(MLA_KV_RANK, MLA_HEADS, MLA_NOPE + MLA_V)
    kpart = jnp.concatenate([wkv[:, :, :MLA_NOPE],
                             jnp.zeros((MLA_KV_RANK, MLA_HEADS, HEAD_BLOCK - MLA_NOPE), w_ukv.dtype)], axis=2)
    vpart = jnp.concatenate([wkv[:, :, MLA_NOPE:],
                             jnp.zeros((MLA_KV_RANK, MLA_HEADS, HEAD_BLOCK - MLA_V), w_ukv.dtype)], axis=2)
    wkv2 = jnp.concatenate([kpart.reshape(MLA_KV_RANK, -1), vpart.reshape(MLA_KV_RANK, -1)], axis=1).astype(BF16)
    return w1, wq2, wkv2


def _rope_lane_tables(s):
    hr = MLA_ROPE // 2
    inv = ROPE_BASE ** (-jnp.arange(0, MLA_ROPE, 2, dtype=F32) / MLA_ROPE)
    ang = jnp.arange(s, dtype=F32)[:, None] * inv[None, :]
    cos, sin = jnp.cos(ang), jnp.sin(ang)
    pad = jnp.zeros((s, HEAD_BLOCK - MLA_NOPE - MLA_ROPE), F32)
    ct = jnp.concatenate([jnp.ones((s, MLA_NOPE), F32), cos, cos, pad], axis=1)
    st = jnp.concatenate([jnp.zeros((s, MLA_NOPE), F32), -sin, sin, pad], axis=1)
    assert ct.shape[1] == HEAD_BLOCK and hr * 2 == MLA_ROPE
    return ct, st


def _l1_proj(x, attn_norm, w1, q_norm, wq2, kv_norm, wkv2, ct, st, nb_pad, tm=512):
    b, s, d = x.shape
    width = MLA_HEADS * HEAD_BLOCK
    const = lambda shape: pl.BlockSpec(shape, lambda i, j: (0, 0))
    tok = lambda n: pl.BlockSpec((1, tm, n), lambda i, j: (i, j, 0))
    return pl.pallas_call(
        _l1_proj_kernel,
        grid=(b, s // tm),
        in_specs=[tok(d), const((1, d)), const(w1.shape), const((1, MLA_Q_RANK)), const(wq2.shape),
                  const((1, MLA_KV_RANK)), const(wkv2.shape),
                  pl.BlockSpec((tm, HEAD_BLOCK), lambda i, j: (j, 0)),
                  pl.BlockSpec((tm, HEAD_BLOCK), lambda i, j: (j, 0))],
        out_specs=[tok(width), tok(width), tok(width),
                   pl.BlockSpec((tm, S5_CH), lambda i, j: (j, i))],
        out_shape=[jax.ShapeDtypeStruct((b, s, width), BF16),
                   jax.ShapeDtypeStruct((b, s, width), BF16),
                   jax.ShapeDtypeStruct((b, s, width), BF16),
                   jax.ShapeDtypeStruct((s, nb_pad * S5_CH), F32)],
        compiler_params=_cparams(("parallel", "parallel")),
        name="l1_norm_in_proj",
    )(x, attn_norm.reshape(1, d), w1, q_norm.reshape(1, -1), wq2, kv_norm.reshape(1, -1), wkv2, ct, st)


def _mla_kernel(q_ref, k_ref, v_ref, o_ref, *, seq_len, tk, unroll):
    tq = q_ref.shape[1]
    qs = [q_ref[0, :, 0:HEAD_BLOCK], q_ref[0, :, HEAD_BLOCK:]]

    def body(j, carry):
        sl = pl.ds(pl.multiple_of(j * tk, tk), tk)
        new = []
        for hh in range(2):
            m, acc = carry[hh]
            hs = slice(hh * HEAD_BLOCK, (hh + 1) * HEAD_BLOCK)
            s = lax.dot_general(qs[hh], k_ref[0, sl, hs], (((1,), (1,)), ((), ())), preferred_element_type=F32)
            m_new = jnp.maximum(m, jnp.max(s, axis=-1, keepdims=True))
            p = jnp.exp2(s - m_new).astype(BF16)
            acc_new = jnp.exp2(m - m_new) * acc + jnp.dot(p, v_ref[0, sl, hs], preferred_element_type=F32)
            new.append((m_new, acc_new))
        return tuple(new)

    init = tuple((jnp.full((tq, 1), NEG_INF, F32), jnp.zeros((tq, HEAD_BLOCK), F32)) for _ in range(2))
    (_, a0), (_, a1) = lax.fori_loop(0, seq_len // tk, body, init, unroll=unroll)
    lane = lax.broadcasted_iota(jnp.int32, (tq, LANES), 1)
    o0 = a0 / pltpu.roll(a0, MLA_V, 1)
    o1 = a1 / pltpu.roll(a1, MLA_V, 1)
    o_ref[0] = jnp.where(lane < MLA_V, o0, pltpu.roll(o1, MLA_V, 1)).astype(o_ref.dtype)


def _mla_attention(q, k, v, tq=512, tk=2048, unroll=2):
    b, s, _ = q.shape
    hpairs = MLA_HEADS // 2
    pair = 2 * HEAD_BLOCK
    return pl.pallas_call(
        functools.partial(_mla_kernel, seq_len=s, tk=tk, unroll=unroll),
        grid=(b, hpairs, s // tq),
        in_specs=[pl.BlockSpec((1, tq, pair), lambda i, hp, j: (i, j, hp)),
                  pl.BlockSpec((1, s, pair), lambda i, hp, j: (i, 0, hp)),
                  pl.BlockSpec((1, s, pair), lambda i, hp, j: (i, 0, hp))],
        out_specs=pl.BlockSpec((1, tq, LANES), lambda i, hp, j: (i, j, hp)),
        out_shape=jax.ShapeDtypeStruct((b, s, MLA_HEADS * MLA_V), BF16),
        compiler_params=_cparams(("parallel", "parallel", "arbitrary")),
        name="l1_latent_attention",
    )(q, k, v)


S5_BLOCKS = S5_CH // LANES
S5_BLOCK_STATES = S5_NSTATE // S5_BLOCKS
S5_UNROLL = 4


def _s5_kernel(u_ref, bb_ref, are_ref, aim_ref, cb_ref, y_ref, bu_ref, x_ref, *, reverse, paired):
    tc, nb, _ = u_ref.shape
    n = S5_NSTATE
    sb = S5_BLOCK_STATES

    @pl.when(pl.program_id(0) == 0)
    def _():
        x_ref[...] = jnp.zeros_like(x_ref)

    u2 = u_ref[...].reshape(tc * nb, S5_CH)
    if paired:
        row = lax.broadcasted_iota(jnp.int32, (tc * nb, LANES), 0)
        is_fwd = (row % nb) < nb // 2
    for k in range(S5_BLOCKS):
        uk = u2[:, k * LANES:(k + 1) * LANES]
        if paired:
            zero = jnp.zeros_like(uk)
            uk = jnp.concatenate([jnp.where(is_fwd, uk, zero), jnp.where(is_fwd, zero, uk)], axis=1)
        r = jnp.dot(uk.astype(BF16), bb_ref[k], preferred_element_type=F32)
        bu_ref[:, :, k * sb:(k + 1) * sb] = r[:, 0:sb].reshape(tc, nb, sb)
        bu_ref[:, :, n + k * sb:n + (k + 1) * sb] = r[:, sb:].reshape(tc, nb, sb)

    def step(i, carry):
        t = tc - 1 - i if reverse else i
        xr, xi = carry
        a_re = are_ref[...]
        a_im = aim_ref[...]
        nr = a_re * xr - a_im * xi + bu_ref[t, :, 0:n]
        ni = a_re * xi + a_im * xr + bu_ref[t, :, n:]
        bu_ref[t, :, 0:n] = nr
        bu_ref[t, :, n:] = ni
        return nr, ni

    xr, xi = lax.fori_loop(0, tc, step, (x_ref[:, 0:n], x_ref[:, n:]), unroll=S5_UNROLL)
    x_ref[:, 0:n] = xr
    x_ref[:, n:] = xi

    for k in range(S5_BLOCKS):
        xs = jnp.concatenate([bu_ref[:, :, k * sb:(k + 1) * sb], bu_ref[:, :, n + k * sb:n + (k + 1) * sb]], axis=2)
        yk = jnp.dot(xs.reshape(tc * nb, 2 * sb).astype(BF16), cb_ref[k], preferred_element_type=F32)
        if paired:
            yk = jnp.where(is_fwd, yk[:, 0:LANES], yk[:, LANES:])
        y_ref[:, :, k * LANES:(k + 1) * LANES] = yk.reshape(tc, nb, LANES)


def _s5_direction_params(lam_re, lam_im, log_step, b_re, b_im, c_re, c_im, nb):
    lam = lax.complex(jnp.minimum(lam_re.astype(F32), -1e-4), lam_im.astype(F32))
    step = jnp.exp(log_step.astype(F32))[:, None]
    lam_bar = jnp.exp(lam * step)
    bmat = lax.complex(b_re.astype(F32), b_im.astype(F32))
    b_bar = ((lam_bar - 1.0) / lam)[:, :, None] * bmat
    gpb = S5_GROUPS // S5_BLOCKS
    eye = jnp.eye(gpb, dtype=F32)

    def in_map(t):
        t = t.reshape(S5_BLOCKS, gpb, S5_STATE, S5_GROUP_CH)
        return jnp.einsum('kgpc,gh->kgchp', t, eye).reshape(S5_BLOCKS, LANES, S5_BLOCK_STATES)

    def out_map(t):
        t = t.reshape(S5_BLOCKS, gpb, S5_GROUP_CH, S5_STATE)
        return jnp.einsum('kgcp,gh->kgphc', t, eye).reshape(S5_BLOCKS, S5_BLOCK_STATES, LANES)

    bblk = jnp.concatenate([in_map(jnp.real(b_bar)), in_map(jnp.imag(b_bar))], axis=2).astype(BF16)
    cblk = jnp.concatenate([out_map(c_re.astype(F32)), -out_map(c_im.astype(F32))], axis=1).astype(BF16)
    a_re = jnp.broadcast_to(jnp.real(lam_bar).reshape(1, S5_NSTATE), (nb, S5_NSTATE)).astype(F32)
    a_im = jnp.broadcast_to(jnp.imag(lam_bar).reshape(1, S5_NSTATE), (nb, S5_NSTATE)).astype(F32)
    return bblk, a_re, a_im, cblk


def _s5_pair_params(fwd, bwd):
    half = fwd[1].shape[0] // 2
    return (jnp.concatenate([fwd[0], bwd[0]], axis=1),
            jnp.concatenate([fwd[1][:half], bwd[1][:half]], axis=0),
            jnp.concatenate([fwd[2][:half], bwd[2][:half]], axis=0),
            jnp.concatenate([fwd[3], bwd[3]], axis=2))


def _s5_scan(u_tm, params, reverse, paired=False, tc=64):
    s, nb, _ = u_tm.shape
    bdense, a_re, a_im, cdense = params
    nchunks = s // tc
    tmap = (lambda i: (nchunks - 1 - i, 0, 0)) if reverse else (lambda i: (i, 0, 0))
    const = lambda shape: pl.BlockSpec(shape, lambda i: (0,) * len(shape))
    return pl.pallas_call(
        functools.partial(_s5_kernel, reverse=reverse, paired=paired),
        grid=(nchunks,),
        in_specs=[pl.BlockSpec((tc, nb, S5_CH), tmap), const(bdense.shape), const(a_re.shape),
                  const(a_im.shape), const(cdense.shape)],
        out_specs=pl.BlockSpec((tc, nb, S5_CH), tmap),
        out_shape=jax.ShapeDtypeStruct((s, nb, S5_CH), F32),
        scratch_shapes=[pltpu.VMEM((tc, nb, 2 * S5_NSTATE), F32), pltpu.VMEM((nb, 2 * S5_NSTATE), F32)],
        compiler_params=_cparams(("arbitrary",)),
        name="l1_s5_scan_pair" if paired else ("l1_s5_scan_bwd" if reverse else "l1_s5_scan_fwd"),
    )(u_tm, bdense, a_re, a_im, cdense)


def _l1_out_router_kernel(x_ref, mla_ref, yf_ref, yb_ref, u_ref, dskip_ref, gw_ref, gb_ref, wo_ref,
                          fn_ref, rt_ref, x1_out, route_out):
    tm = x_ref.shape[1]
    u = u_ref[...]
    y = yf_ref[...] + yb_ref[...] + dskip_ref[...] * u
    z = jax.nn.gelu(y)
    gate = jax.nn.sigmoid(jnp.dot(z.astype(BF16), gw_ref[...], preferred_element_type=F32) + gb_ref[...])
    ssm = (z * gate).astype(BF16)
    half = wo_ref.shape[0] // 2
    x1 = (x_ref[0]
          + jnp.dot(mla_ref[0], wo_ref[0:half, :], preferred_element_type=F32)
          + jnp.dot(ssm, wo_ref[half:, :], preferred_element_type=F32))
    _store_token_slabs(x1_out.at[0], x1, tm)
    hn = _rms(x1, fn_ref[...])

    h_hi = hn.astype(BF16)
    h_lo = (hn - h_hi.astype(F32)).astype(BF16)
    logits = (jnp.dot(h_hi, rt_ref[0], preferred_element_type=F32)
              + jnp.dot(h_hi, rt_ref[1], preferred_element_type=F32)
              + jnp.dot(h_lo, rt_ref[0], preferred_element_type=F32))
    lane = lax.broadcasted_iota(jnp.int32, logits.shape, 1)
    lg = jnp.where(lane < N_EXPERTS, logits, -jnp.inf)
    m1 = jnp.max(lg, axis=-1, keepdims=True)
    i1 = jnp.min(jnp.where(lg == m1, lane, LANES), axis=-1, keepdims=True)
    lg2 = jnp.where(lane == i1, -jnp.inf, lg)
    m2 = jnp.max(lg2, axis=-1, keepdims=True)
    i2 = jnp.min(jnp.where(lg2 == m2, lane, LANES), axis=-1, keepdims=True)
    e2 = jnp.exp(m2 - m1)
    w1 = 1.0 / (1.0 + e2)
    w2 = e2 / (1.0 + e2)
    route_out[0] = jnp.where(lane == 0, i1.astype(F32),
                             jnp.where(lane == 1, i2.astype(F32),
                                       jnp.where(lane == 2, w1, jnp.where(lane == 3, w2, 0.0))))


def _l1_out_router(x, mla, yf, yb, u_tm, d_skip, glu_w, glu_b, w_out, ffn_norm, router_pad, tm=512):
    b, s, d = x.shape
    tok = lambda n: pl.BlockSpec((1, tm, n), lambda i, j: (i, j, 0))
    tmaj = pl.BlockSpec((tm, S5_CH), lambda i, j: (j, i))
    const = lambda shape: pl.BlockSpec(shape, lambda i, j: (0,) * len(shape))
    return pl.pallas_call(
        _l1_out_router_kernel,
        grid=(b, s // tm),
        in_specs=[tok(d), tok(MLA_HEADS * MLA_V), tmaj, tmaj, tmaj, const((1, S5_CH)),
                  const(glu_w.shape), const((1, S5_CH)), const(w_out.shape), const((1, d)),
                  const(router_pad.shape)],
        out_specs=[pl.BlockSpec((1, tm * SLAB_ROWS, LANES), lambda i, j: (i, j, 0)), tok(LANES)],
        out_shape=[jax.ShapeDtypeStruct((b, s * SLAB_ROWS, LANES), F32),
                   jax.ShapeDtypeStruct((b, s, LANES), F32)],
        compiler_params=_cparams(("parallel", "parallel")),
        name="l1_out_proj_router",
    )(x, mla, yf, yb, u_tm, d_skip.reshape(1, -1), glu_w, glu_b.reshape(1, -1), w_out,
      ffn_norm.reshape(1, d), router_pad)


def _start_row_gather(idx_ref, src_hbm, dst, sem, n_rows):
    def issue(r, carry):
        src = pl.multiple_of(idx_ref[r] * SLAB_ROWS, SLAB_ROWS)
        pltpu.make_async_copy(src_hbm.at[pl.ds(src, SLAB_ROWS)],
                              dst.at[pl.ds(pl.multiple_of(r * SLAB_ROWS, SLAB_ROWS), SLAB_ROWS)], sem).start()
        return carry

    lax.fori_loop(0, n_rows, issue, 0, unroll=8)


def _wait_row_gather(src_hbm, dst, sem, n_rows):
    pltpu.make_async_copy(src_hbm.at[pl.ds(0, n_rows * SLAB_ROWS)], dst, sem).wait()


def _expert_ffn_kernel(te_ref, nt_ref, idx0_ref, idxn_ref, x_hbm, gw_ref, g_ref, wg_ref, wu_ref, wd_ref,
                       o_ref, xbuf, sem, hn_ref, acc_ref):
    i = pl.program_id(0)
    j = pl.program_id(1)
    last = pl.num_programs(1) - 1
    tm = hn_ref.shape[0]
    n_used = nt_ref[0]
    slot = i % 2

    @pl.when(jnp.logical_and(i == 0, j == 0))
    def _():
        _start_row_gather(idx0_ref, x_hbm, xbuf.at[0], sem.at[0], tm)

    @pl.when(jnp.logical_and(j == 0, i < n_used))
    def _():
        _wait_row_gather(x_hbm, xbuf.at[slot], sem.at[slot], tm)
        x = _load_token_slabs(xbuf.at[slot], tm)
        hn_ref[...] = _rms(x, g_ref[...]).astype(BF16)
        acc_ref[...] = jnp.zeros_like(acc_ref)

    @pl.when(jnp.logical_and(j == 0, i + 1 < n_used))
    def _():
        _start_row_gather(idxn_ref, x_hbm, xbuf.at[1 - slot], sem.at[1 - slot], tm)

    @pl.when(i < n_used)
    def _():
        h = hn_ref[...]
        a = jnp.dot(h, wg_ref[0], preferred_element_type=F32)
        u = jnp.dot(h, wu_ref[0], preferred_element_type=F32)
        act = (a * jax.nn.sigmoid(a) * u).astype(BF16)
        acc_ref[...] += jnp.dot(act, wd_ref[0], preferred_element_type=F32)

        @pl.when(j == last)
        def _():
            _store_token_slabs(o_ref, gw_ref[...] * acc_ref[...], tm)

    @pl.when(jnp.logical_and(i >= n_used, j == last))
    def _():
        o_ref[...] = jnp.zeros_like(o_ref)


def _expert_ffn(x_slabs, src_tok, gate_w, tile_expert, n_tiles_used, ffn_norm, wg, wu, wd, tm, tf=1792):
    a_pad = src_tok.shape[0]
    n_tiles = a_pad // tm
    d = D_MODEL
    f = wg.shape[2]
    grid_spec = pltpu.PrefetchScalarGridSpec(
        num_scalar_prefetch=2,
        grid=(n_tiles, f // tf),
        in_specs=[
            pl.BlockSpec((tm,), lambda i, j, te, nt: (0,), memory_space=pltpu.SMEM),
            pl.BlockSpec((tm,), lambda i, j, te, nt: (jnp.minimum(i + 1, n_tiles - 1),), memory_space=pltpu.SMEM),
            pl.BlockSpec(memory_space=pl.ANY),
            pl.BlockSpec((tm, 1), lambda i, j, te, nt: (i, 0)),
            pl.BlockSpec((1, d), lambda i, j, te, nt: (0, 0)),
            pl.BlockSpec((1, d, tf), lambda i, j, te, nt: (te[i], 0, j)),
            pl.BlockSpec((1, d, tf), lambda i, j, te, nt: (te[i], 0, j)),
            pl.BlockSpec((1, tf, d), lambda i, j, te, nt: (te[i], j, 0)),
        ],
        out_specs=pl.BlockSpec((tm * SLAB_ROWS, LANES), lambda i, j, te, nt: (i, 0)),
        scratch_shapes=[pltpu.VMEM((2, tm * SLAB_ROWS, LANES), F32), pltpu.SemaphoreType.DMA((2,)),
                        pltpu.VMEM((tm, d), BF16), pltpu.VMEM((tm, d), F32)],
    )
    return pl.pallas_call(
        _expert_ffn_kernel,
        grid_spec=grid_spec,
        out_shape=jax.ShapeDtypeStruct((a_pad * SLAB_ROWS, LANES), F32),
        compiler_params=_cparams(("arbitrary", "arbitrary")),
        name="l1_expert_ffn",
    )(tile_expert, n_tiles_used, src_tok, src_tok, x_slabs, gate_w, ffn_norm.reshape(1, d), wg, wu, wd)


def _combine_norm_kernel(ia0_ref, ib0_ref, ian_ref, ibn_ref, x_ref, ys_hbm, g_ref, o_ref, buf, sem):
    tm = o_ref.shape[0]
    i = pl.program_id(0)
    slot = i % 2

    @pl.when(i == 0)
    def _():
        _start_row_gather(ia0_ref, ys_hbm, buf.at[0, 0], sem.at[0, 0], tm)
        _start_row_gather(ib0_ref, ys_hbm, buf.at[0, 1], sem.at[0, 1], tm)

    @pl.when(i + 1 < pl.num_programs(0))
    def _():
        _start_row_gather(ian_ref, ys_hbm, buf.at[1 - slot, 0], sem.at[1 - slot, 0], tm)
        _start_row_gather(ibn_ref, ys_hbm, buf.at[1 - slot, 1], sem.at[1 - slot, 1], tm)

    _wait_row_gather(ys_hbm, buf.at[slot, 0], sem.at[slot, 0], tm)
    _wait_row_gather(ys_hbm, buf.at[slot, 1], sem.at[slot, 1], tm)
    y = _load_token_slabs(x_ref, tm) + (_load_token_slabs(buf.at[slot, 0], tm)
                                        + _load_token_slabs(buf.at[slot, 1], tm))
    o_ref[...] = _rms(y, g_ref[...])


def _combine_norm(x_slabs, ys_slabs, dest, final_norm, tm=256):
    t = x_slabs.shape[0] // SLAB_ROWS
    nblk = t // tm
    slab = (tm * SLAB_ROWS, LANES)
    return pl.pallas_call(
        _combine_norm_kernel,
        grid=(nblk,),
        in_specs=[pl.BlockSpec((tm,), lambda i: (0,), memory_space=pltpu.SMEM),
                  pl.BlockSpec((tm,), lambda i: (nblk,), memory_space=pltpu.SMEM),
                  pl.BlockSpec((tm,), lambda i: (jnp.minimum(i + 1, nblk - 1),), memory_space=pltpu.SMEM),
                  pl.BlockSpec((tm,), lambda i: (jnp.minimum(i + 1, nblk - 1) + nblk,), memory_space=pltpu.SMEM),
                  pl.BlockSpec(slab, lambda i: (i, 0)),
                  pl.BlockSpec(memory_space=pl.ANY),
                  pl.BlockSpec((1, D_MODEL), lambda i: (0, 0))],
        out_specs=pl.BlockSpec((tm, D_MODEL), lambda i: (i, 0)),
        out_shape=jax.ShapeDtypeStruct((t, D_MODEL), F32),
        scratch_shapes=[pltpu.VMEM((2, 2) + slab, F32), pltpu.SemaphoreType.DMA((2, 2))],
        compiler_params=_cparams(("arbitrary",)),
        name="l1_combine_final_norm",
    )(dest, dest, dest, dest, x_slabs, ys_slabs, final_norm.reshape(1, D_MODEL))


def _moe(x_slabs, route, ffn_norm, wg, wu, wd, final_norm, tm=512):
    t = x_slabs.shape[0] // SLAB_ROWS
    e_idx = jnp.concatenate([route[:, 0], route[:, 1]]).astype(jnp.int32)
    e_w = jnp.concatenate([route[:, 2], route[:, 3]])
    n_assign = TOP_K * t
    order = jnp.argsort(e_idx, stable=True).astype(jnp.int32)
    inv = jnp.argsort(order).astype(jnp.int32)
    counts = jnp.sum(e_idx[:, None] == jnp.arange(N_EXPERTS, dtype=jnp.int32)[None, :], axis=0).astype(jnp.int32)
    starts = jnp.cumsum(counts) - counts
    padded = ((counts + tm - 1) // tm) * tm
    pad_ends = jnp.cumsum(padded)
    pad_starts = pad_ends - padded
    a_pad = n_assign + N_EXPERTS * tm
    n_tiles = a_pad // tm

    tile_start = jnp.arange(n_tiles, dtype=jnp.int32) * tm
    tile_expert = jnp.minimum(jnp.sum(tile_start[:, None] >= pad_ends[None, :], axis=1), N_EXPERTS - 1).astype(jnp.int32)
    n_tiles_used = (pad_ends[-1] // tm).astype(jnp.int32).reshape(1)

    slot = jnp.arange(a_pad, dtype=jnp.int32)
    slot_e = jnp.repeat(tile_expert, tm)
    within = slot - pad_starts[slot_e]
    valid = within < counts[slot_e]
    src = order[jnp.clip(starts[slot_e] + within, 0, n_assign - 1)]
    src_tok = jnp.where(valid, src % t, 0).astype(jnp.int32)
    gate_w = jnp.where(valid, e_w[src], 0.0).astype(F32).reshape(a_pad, 1)

    ys = _expert_ffn(x_slabs, src_tok, gate_w, tile_expert, n_tiles_used, ffn_norm, wg, wu, wd, tm)
    dest = (pad_starts[e_idx] + inv - starts[e_idx]).astype(jnp.int32)
    return _combine_norm(x_slabs, ys, dest, final_norm)


def _l0_in_weight(w_in):
    hq = RET_HEADS * RET_DK
    hv = RET_HEADS * RET_DV
    hd = DIL_HEADS * DIL_DH
    rq, rk, rv, rg, dq, dk, dv = jnp.split(
        w_in, [hq, 2 * hq, 2 * hq + hv, 2 * hq + 2 * hv, 2 * hq + 2 * hv + hd, 2 * hq + 2 * hv + 2 * hd], axis=1)
    dup = lambda t: jnp.concatenate([t.reshape(-1, RET_HEADS, 1, RET_DK)] * 2, axis=2).reshape(-1, 2 * hq)
    w = jnp.concatenate([dup(rq) * RET_DK ** -0.5, dup(rk), rv, rg, dq * DIL_DH ** -0.5, dk, dv], axis=1)
    assert w.shape[1] == L0_COLS
    return w.astype(BF16)


def _prepare(l0_w_in, l0_ret_decay_f, l0_ret_decay_b, l0_w_out, l0_ffn_w_gate, l0_ffn_w_up, l0_ffn_w_down,
             l1_w_in, l1_mla_w_uq, l1_mla_w_ukv, l1_s5_glu_w, l1_w_out, l1_router,
             l1_exp_w_gate, l1_exp_w_up, l1_exp_w_down):
    w1, wq2, wkv2 = _l1_weights(l1_w_in, l1_mla_w_uq, l1_mla_w_ukv)
    router_f32 = jnp.zeros((D_MODEL, LANES), F32).at[:, :N_EXPERTS].set(l1_router.astype(F32))
    router_hi = router_f32.astype(BF16)
    router_pad = jnp.stack([router_hi, (router_f32 - router_hi.astype(F32)).astype(BF16)])
    return dict(
        l0_w_in=_l0_in_weight(l0_w_in),
        ret_tables=_retention_tables(l0_ret_decay_f, l0_ret_decay_b),
        l0_w_out=l0_w_out.astype(BF16),
        l0_wg=l0_ffn_w_gate.astype(BF16), l0_wu=l0_ffn_w_up.astype(BF16), l0_wd=l0_ffn_w_down.astype(BF16),
        l1_w1=w1, l1_wq2=wq2, l1_wkv2=wkv2,
        glu_w=l1_s5_glu_w.astype(BF16), l1_w_out=l1_w_out.astype(BF16), router_pad=router_pad,
        exp_wg=l1_exp_w_gate.astype(BF16), exp_wu=l1_exp_w_up.astype(BF16), exp_wd=l1_exp_w_down.astype(BF16),
    )


def _trunk(x, prep, p):
    b, s, d = x.shape
    t = b * s
    p0, *regrouped = _l0_in_proj(x.reshape(t, d), p['l0_attn_norm'], prep['l0_w_in'])
    p0 = p0.reshape(b, s, L0_COLS)
    ret = _retention(p0, prep['ret_tables'], p['l0_ret_gn'])
    dil = _dilated_attention(p0, regrouped)
    x = _l0_out_ffn(x.reshape(t, d), ret.reshape(t, -1), dil.reshape(t, -1), prep['l0_w_out'],
                    p['l0_ffn_norm'], prep['l0_wg'], prep['l0_wu'], prep['l0_wd']).reshape(b, s, d)
    nb = -(-b // SUBLANES) * SUBLANES
    ct, st = _rope_lane_tables(s)
    q, k, v, u_tm = _l1_proj(x, p['l1_attn_norm'], prep['l1_w1'], p['l1_mla_q_norm'], prep['l1_wq2'],
                             p['l1_mla_kv_norm'], prep['l1_wkv2'], ct, st, b)
    mla = _mla_attention(q, k, v)
    u3 = u_tm.reshape(s, b, S5_CH)
    sf = _s5_direction_params(p['l1_s5_lam_re_f'], p['l1_s5_lam_im_f'], p['l1_s5_log_step_f'],
                              p['l1_s5_b_re'], p['l1_s5_b_im'], p['l1_s5_c_re'], p['l1_s5_c_im'], nb)
    sb = _s5_direction_params(p['l1_s5_lam_re_b'], p['l1_s5_lam_im_b'], p['l1_s5_log_step_b'],
                              p['l1_s5_b_re'], p['l1_s5_b_im'], p['l1_s5_c_re'], p['l1_s5_c_im'], nb)
    if 2 * b == SUBLANES:
        y = _s5_scan(jnp.concatenate([u3, jnp.flip(u3, axis=0)], axis=1), _s5_pair_params(sf, sb),
                     reverse=False, paired=True)
        yf = y[:, :b].reshape(s, b * S5_CH)
        yb = jnp.flip(y[:, b:], axis=0).reshape(s, b * S5_CH)
    else:
        if nb != b:
            u3 = jnp.pad(u3, ((0, 0), (0, nb - b), (0, 0)))
        yf = _s5_scan(u3, sf, reverse=False).reshape(s, nb * S5_CH)
        yb = _s5_scan(u3, sb, reverse=True).reshape(s, nb * S5_CH)
    x1s, route = _l1_out_router(x, mla, yf, yb, u_tm, p['l1_s5_d'], prep['glu_w'], p['l1_s5_glu_b'],
                                prep['l1_w_out'], p['l1_ffn_norm'], prep['router_pad'])
    out = _moe(x1s.reshape(t * SLAB_ROWS, LANES), route.reshape(t, LANES), p['l1_ffn_norm'],
               prep['exp_wg'], prep['exp_wu'], prep['exp_wd'], p['final_norm'])
    return out.reshape(b, s, d)


def kernel(x_prompt, x_sample, l0_attn_norm, l0_w_in, l0_ret_decay_f, l0_ret_decay_b, l0_ret_gn, l0_w_out,
           l0_ffn_norm, l0_ffn_w_gate, l0_ffn_w_up, l0_ffn_w_down, l1_attn_norm, l1_w_in, l1_mla_q_norm,
           l1_mla_w_uq, l1_mla_kv_norm, l1_mla_w_ukv, l1_s5_lam_re_f, l1_s5_lam_im_f, l1_s5_log_step_f,
           l1_s5_lam_re_b, l1_s5_lam_im_b, l1_s5_log_step_b, l1_s5_b_re, l1_s5_b_im, l1_s5_c_re, l1_s5_c_im,
           l1_s5_d, l1_s5_glu_w, l1_s5_glu_b, l1_w_out, l1_ffn_norm, l1_router, l1_exp_w_gate, l1_exp_w_up,
           l1_exp_w_down, final_norm):
    p = dict(locals())
    prep = _prepare(l0_w_in, l0_ret_decay_f, l0_ret_decay_b, l0_w_out, l0_ffn_w_gate, l0_ffn_w_up,
                    l0_ffn_w_down, l1_w_in, l1_mla_w_uq, l1_mla_w_ukv, l1_s5_glu_w, l1_w_out, l1_router,
                    l1_exp_w_gate, l1_exp_w_up, l1_exp_w_down)
    return (_trunk(x_prompt, prep, p), _trunk(x_sample, prep, p))
```

```python
import functools
import math

import jax
import jax.numpy as jnp
from jax import lax
from jax.experimental import pallas as pl
from jax.experimental.pallas import tpu as pltpu

F32 = jnp.float32
BF16 = jnp.bfloat16

D_MODEL = 1024
EPS = 1e-6
NEG_INF = -1e30
RET_HEADS = 4
RET_DK = 64
RET_DV = 128
RET_CHUNK = 128
RET_UNROLL = 16
DIL_HEADS = 8
DIL_DH = 64
DIL_PATTERNS = ((128, 1), (512, 4), (2048, 16))
DIL_QBLOCK = 128
DIL_UNROLL = 16
MLA_HEADS = 8
MLA_Q_RANK = 256
MLA_KV_RANK = 128
MLA_NOPE = 64
MLA_ROPE = 32
MLA_V = 64
ROPE_BASE = 10000.0
S5_GROUPS = 32
S5_GROUP_CH = 16
S5_STATE = 64
S5_CH = S5_GROUPS * S5_GROUP_CH
S5_NSTATE = S5_GROUPS * S5_STATE
N_EXPERTS = 8
TOP_K = 2

LANES = 128
SUBLANES = 8
VMEM_LIMIT = 56 * 1024 * 1024

L0_COLS = 3584
L0_BLOCKS = L0_COLS // LANES
L0_RQ, L0_RK, L0_RV, L0_RG, L0_DQ, L0_DK, L0_DV = 0, 4, 8, 12, 16, 20, 24


def _cparams(sem):
    return pltpu.CompilerParams(dimension_semantics=sem, vmem_limit_bytes=VMEM_LIMIT)


def _rms(x, g):
    return x * lax.rsqrt(jnp.mean(x * x, axis=-1, keepdims=True) + EPS) * g


SLAB_ROWS = D_MODEL // LANES


def _store_token_slabs(ref2d, x, n_tok):
    for s in range(SLAB_ROWS):
        ref2d[pl.ds(s, n_tok, stride=SLAB_ROWS), :] = x[:, s * LANES:(s + 1) * LANES]


def _load_token_slabs(ref2d, n_tok):
    return jnp.concatenate([ref2d[pl.ds(s, n_tok, stride=SLAB_ROWS), :] for s in range(SLAB_ROWS)], axis=1)


DIL_COLS = 3 * DIL_HEADS * DIL_DH
DIL_COL0 = L0_DQ * LANES
DIL_STRIDED = tuple(dil for _, dil in DIL_PATTERNS if dil > 1)


def _l0_in_proj_kernel(x_ref, g_ref, w_ref, o_ref, *rest):
    dil_refs, dsc = rest[:-1], rest[-1]
    tm = x_ref.shape[0]
    xn = _rms(x_ref[...], g_ref[...]).astype(BF16)
    res = jnp.dot(xn, w_ref[...], preferred_element_type=F32)
    o_ref[...] = res.astype(o_ref.dtype)
    for c in range(DIL_COLS // LANES):
        dsc[c] = res[:, DIL_COL0 + c * LANES:DIL_COL0 + (c + 1) * LANES]
    for ref, dil in zip(dil_refs, DIL_STRIDED):
        for r in range(dil):
            for c in range(DIL_COLS // LANES):
                lo = r * DIL_COLS + c * LANES
                ref[:, lo:lo + LANES] = dsc[c, pl.ds(r, tm // dil, stride=dil), :].astype(ref.dtype)


def _l0_in_proj(x2d, gain, w, tm=512):
    t, d = x2d.shape
    n = w.shape[1]
    dil_specs = [pl.BlockSpec((tm // dil, dil * DIL_COLS), lambda i: (i, 0)) for dil in DIL_STRIDED]
    dil_shapes = [jax.ShapeDtypeStruct((t // dil, dil * DIL_COLS), BF16) for dil in DIL_STRIDED]
    return pl.pallas_call(
        _l0_in_proj_kernel,
        grid=(t // tm,),
        in_specs=[
            pl.BlockSpec((tm, d), lambda i: (i, 0)),
            pl.BlockSpec((1, d), lambda i: (0, 0)),
            pl.BlockSpec((d, n), lambda i: (0, 0)),
        ],
        out_specs=[pl.BlockSpec((tm, n), lambda i: (i, 0))] + dil_specs,
        out_shape=[jax.ShapeDtypeStruct((t, n), BF16)] + dil_shapes,
        scratch_shapes=[pltpu.VMEM((DIL_COLS // LANES, tm, LANES), F32)],
        compiler_params=_cparams(("parallel",)),
        name="l0_norm_in_proj",
    )(x2d, gain.reshape(1, d), w)


def _retention_kernel(q_ref, k_ref, v_ref, g_ref, d_ref, qw_ref, kw_ref, cd_ref, gn_ref,
                      o_ref, kv_ref, p_ref, *, n_chunks):
    c = RET_CHUNK
    kw = kw_ref[0]
    qw = qw_ref[0]
    dmat = d_ref[0]
    gn = gn_ref[...]

    def kv_body(n, carry):
        sl = pl.ds(pl.multiple_of(n * c, c), c)
        kc = (k_ref[0, sl, :].astype(F32) * kw).T.astype(BF16)
        kv_ref[n] = jnp.dot(kc, v_ref[0, sl, :], preferred_element_type=F32)
        return carry

    lax.fori_loop(0, n_chunks, kv_body, 0, unroll=RET_UNROLL)

    half = c // 2
    dec_f = cd_ref[0, 0:half, :]
    dec_b = cd_ref[0, half:c, :]

    def fwd_body(n, s):
        p_ref[n, 0:half, :] = s.astype(BF16)
        return s * dec_f + kv_ref[n, 0:half, :]

    lax.fori_loop(0, n_chunks, fwd_body, jnp.zeros((half, RET_DV), F32))

    def bwd_body(i, s):
        n = n_chunks - 1 - i
        p_ref[n, half:c, :] = s.astype(BF16)
        return s * dec_b + kv_ref[n, half:c, :]

    lax.fori_loop(0, n_chunks, bwd_body, jnp.zeros((half, RET_DV), F32))

    def out_body(n, carry):
        sl = pl.ds(pl.multiple_of(n * c, c), c)
        qc = q_ref[0, sl, :]
        s = lax.dot_general(qc, k_ref[0, sl, :], (((1,), (1,)), ((), ())), preferred_element_type=F32)
        intra = jnp.dot((s * dmat).astype(BF16), v_ref[0, sl, :], preferred_element_type=F32)
        qq = (qc.astype(F32) * qw).astype(BF16)
        y = intra + jnp.dot(qq, p_ref[n], preferred_element_type=F32)
        mu = jnp.mean(y, axis=-1, keepdims=True)
        yc = y - mu
        var = jnp.mean(yc * yc, axis=-1, keepdims=True)
        yn = yc * lax.rsqrt(var + EPS) * gn
        gg = g_ref[0, sl, :].astype(F32)
        o_ref[0, sl, :] = (gg * jax.nn.sigmoid(gg) * yn).astype(o_ref.dtype)
        return carry

    lax.fori_loop(0, n_chunks, out_body, 0, unroll=RET_UNROLL)


def _retention_tables(decay_f, decay_b):
    c = RET_CHUNK
    lg_f = jax.nn.log_sigmoid(decay_f.astype(F32))[:, None, None]
    lg_b = jax.nn.log_sigmoid(decay_b.astype(F32))[:, None, None]
    j = jnp.arange(c, dtype=F32)
    diff = j[:, None] - j[None, :]
    dmat = 0.5 * jnp.where(diff >= 0, jnp.exp(lg_f * jnp.maximum(diff, 0.0)),
                           jnp.exp(lg_b * jnp.maximum(-diff, 0.0)))
    lane_f = (jnp.arange(LANES) < RET_DK)[None, None, :]
    jj = j[None, :, None]
    qw = jnp.where(lane_f, jnp.exp(lg_f * (jj + 1.0)), jnp.exp(lg_b * (c - jj)))
    kw = jnp.where(lane_f, jnp.exp(lg_f * (c - 1.0 - jj)), jnp.exp(lg_b * jj))
    row_f = (jnp.arange(c) < c // 2)[None, :, None]
    cd = jnp.where(row_f, jnp.exp(lg_f * c), jnp.exp(lg_b * c)) * jnp.ones((1, 1, RET_DV), F32)
    return dmat.astype(F32), qw.astype(F32), kw.astype(F32), cd.astype(F32)


def _retention(p0, tables, gn):
    b, s, _ = p0.shape
    dmat, qw, kw, cd = tables
    n_chunks = s // RET_CHUNK
    seq = lambda col: pl.BlockSpec((1, s, LANES), lambda i, h: (i, 0, col + h))
    tab = pl.BlockSpec((1, RET_CHUNK, LANES), lambda i, h: (h, 0, 0))
    return pl.pallas_call(
        functools.partial(_retention_kernel, n_chunks=n_chunks),
        grid=(b, RET_HEADS),
        in_specs=[seq(L0_RQ), seq(L0_RK), seq(L0_RV), seq(L0_RG), tab, tab, tab, tab,
                  pl.BlockSpec((1, LANES), lambda i, h: (0, h))],
        out_specs=pl.BlockSpec((1, s, LANES), lambda i, h: (i, 0, h)),
        out_shape=jax.ShapeDtypeStruct((b, s, RET_HEADS * RET_DV), BF16),
        scratch_shapes=[pltpu.VMEM((n_chunks, RET_CHUNK, RET_DV), F32),
                        pltpu.VMEM((n_chunks, RET_CHUNK, RET_DV), BF16)],
        compiler_params=_cparams(("parallel", "parallel")),
        name="l0_retention",
    )(p0, p0, p0, p0, dmat, qw, kw, cd, gn.reshape(1, -1))


def _dilated_kernel(*refs, seq_len, half, q_rows, n_hp, has_prev, final):
    q_ref, k_ref, v_ref, bias_ref = refs[:4]
    refs = refs[4:]
    ratio = DIL_CHAIN_RATIO
    if has_prev:
        prev_o, prev_l, refs = refs[:ratio], refs[ratio:2 * ratio], refs[2 * ratio:]
        po_ref, pl_ref = refs[-2:]
        refs = refs[:-2]
        for m in range(ratio):
            for hp in range(n_hp):
                ls = slice(hp * LANES, (hp + 1) * LANES)
                po_ref[hp, pl.ds(m, q_rows // ratio, stride=ratio), :] = prev_o[m][0, :, ls]
                pl_ref[hp, pl.ds(m, q_rows // ratio, stride=ratio), :] = prev_l[m][0, :, ls]
    o_ref = refs[0]
    lse_ref = None if final else refs[1]
    tq = DIL_QBLOCK
    kw = tq + 2 * half
    base = pl.program_id(3) * q_rows
    lane = lax.broadcasted_iota(jnp.int32, (tq, LANES), 1)
    lo = lane < DIL_DH

    def blk(i, carry):
        off = pl.multiple_of(i * tq, tq)
        rows = pl.ds(off, tq)
        qs = base + off
        ks = pl.multiple_of(jnp.clip(qs - half, 0, seq_len - kw), half)
        shift = (qs - ks) // half
        for hp in range(n_hp):
            ls = slice(hp * LANES, (hp + 1) * LANES)
            q = q_ref[0, rows, ls]
            kwin = k_ref[0, pl.ds(ks, kw), ls]
            vwin = v_ref[0, pl.ds(ks, kw), ls]
            zero = jnp.zeros_like(q)
            q2 = jnp.concatenate([jnp.where(lo, q, zero), jnp.where(lo, zero, q)], axis=0)
            s = lax.dot_general(q2, kwin, (((1,), (1,)), ((), ())), preferred_element_type=F32)
            s = s + bias_ref[hp, shift]
            m = jnp.max(s, axis=-1, keepdims=True)
            p = jnp.exp(s - m)
            den = jnp.sum(p, axis=-1, keepdims=True)
            pv = jnp.dot(p.astype(BF16), vwin, preferred_element_type=F32) / den
            lse2 = m + jnp.log(den)
            o = jnp.where(lo, pv[0:tq], pv[tq:])
            lse = jnp.where(lo, lse2[0:tq], lse2[tq:])
            if has_prev:
                lse_p = pl_ref[hp, rows, :]
                top = jnp.maximum(lse_p, lse)
                wa = jnp.exp(lse_p - top)
                wb = jnp.exp(lse - top)
                den = wa + wb
                o = (wa * po_ref[hp, rows, :] + wb * o) / den
                lse = top + jnp.log(den)
            o_ref[0, rows, ls] = o.astype(o_ref.dtype)
            if not final:
                lse_ref[0, rows, ls] = lse
        return carry

    n_blk = q_rows // tq
    lax.fori_loop(0, n_blk, blk, 0, unroll=max(1, min(DIL_UNROLL // n_hp, n_blk)))


DIL_MAX_RESIDENT_ROWS = 2048


DIL_CHAIN_RATIO = 4
assert all(DIL_PATTERNS[n + 1][1] == DIL_CHAIN_RATIO * DIL_PATTERNS[n][1] for n in range(len(DIL_PATTERNS) - 1))


def _dilated_pattern(src, col0, batch, window, dil, prev, final):
    width = DIL_HEADS * DIL_DH
    seq_len = src.shape[1]
    group_blocks = src.shape[2] // dil // LANES
    half = window // (2 * dil)
    hpairs = DIL_HEADS // 2
    n_hp = hpairs if seq_len <= DIL_MAX_RESIDENT_ROWS else 1
    q_rows = min(seq_len, max(1024, DIL_QBLOCK * DIL_UNROLL // n_hp))
    lanes = n_hp * LANES
    hp_blocks = hpairs // n_hp
    c0 = col0 // LANES
    slopes = (jnp.exp2(-8.0 * jnp.arange(1, DIL_HEADS + 1, dtype=F32) / DIL_HEADS) * dil).reshape(hpairs, 2, 1, 1, 1)
    kw = DIL_QBLOCK + 2 * half
    shift = jnp.arange(3, dtype=jnp.int32)[:, None, None] * half
    dist = jnp.abs(jnp.arange(DIL_QBLOCK, dtype=jnp.int32)[None, :, None] + shift
                   - jnp.arange(kw, dtype=jnp.int32)[None, None, :])
    bias = jnp.where(dist <= half, -(slopes * dist.astype(F32)), NEG_INF).astype(F32)
    bias = bias.transpose(0, 2, 1, 3, 4).reshape(hpairs, 3, 2 * DIL_QBLOCK, kw)
    col = lambda part: (lambda i, r, hp, j: (i, 0, (r * group_blocks + c0 + part * hpairs) // n_hp + hp))
    qcol = lambda i, r, hp, j: (i, j, (r * group_blocks + c0) // n_hp + hp)
    oblk = pl.BlockSpec((1, q_rows, lanes), lambda i, r, hp, j: (i, j, r * hp_blocks + hp))
    in_specs = [pl.BlockSpec((1, q_rows, lanes), qcol),
                pl.BlockSpec((1, seq_len, lanes), col(1)),
                pl.BlockSpec((1, seq_len, lanes), col(2)),
                pl.BlockSpec((n_hp, 3, 2 * DIL_QBLOCK, kw), lambda i, r, hp, j: (hp, 0, 0, 0))]
    args = [src, src, src, bias]
    scratch = []
    if prev is not None:
        ratio = DIL_CHAIN_RATIO
        pblk = lambda m: pl.BlockSpec((1, q_rows // ratio, lanes),
                                      lambda i, r, hp, j: (i, j, (m * dil + r) * hp_blocks + hp))
        for a in prev:
            in_specs += [pblk(m) for m in range(ratio)]
            args += [a] * ratio
        scratch = [pltpu.VMEM((n_hp, q_rows, LANES), F32)] * 2
    oshape = (batch, seq_len, dil * width)
    if final:
        out_specs, out_shape = oblk, jax.ShapeDtypeStruct(oshape, BF16)
    else:
        out_specs, out_shape = [oblk, oblk], [jax.ShapeDtypeStruct(oshape, F32)] * 2
    return pl.pallas_call(
        functools.partial(_dilated_kernel, seq_len=seq_len, half=half, q_rows=q_rows, n_hp=n_hp,
                          has_prev=prev is not None, final=final),
        grid=(batch, dil, hp_blocks, seq_len // q_rows),
        in_specs=in_specs,
        out_specs=out_specs,
        out_shape=out_shape,
        scratch_shapes=scratch,
        compiler_params=_cparams(("parallel", "parallel", "parallel", "arbitrary")),
        name=f"l0_dilated_d{dil}",
    )(*args)


def _dilated_attention(p0, regrouped):
    b, s, _ = p0.shape
    by_dil = dict(zip(DIL_STRIDED, regrouped))
    prev = None
    for n, (window, dil) in enumerate(reversed(DIL_PATTERNS)):
        if dil == 1:
            src, col0 = p0, DIL_COL0
        else:
            src, col0 = by_dil[dil].reshape(b, s // dil, dil * DIL_COLS), 0
        prev = _dilated_pattern(src, col0, b, window, dil, prev, final=n == len(DIL_PATTERNS) - 1)
    return prev


def _l0_out_ffn_kernel(x_ref, ret_ref, dil_ref, wo_ref, g_ref, wg_ref, wu_ref, wd_ref, o_ref,
                       x1_ref, hn_ref, acc_ref):
    j = pl.program_id(1)

    @pl.when(j == 0)
    def _():
        half = wo_ref.shape[0] // 2
        x1 = (x_ref[...]
              + jnp.dot(ret_ref[...], wo_ref[0:half, :], preferred_element_type=F32)
              + jnp.dot(dil_ref[...], wo_ref[half:, :], preferred_element_type=F32))
        x1_ref[...] = x1
        hn_ref[...] = _rms(x1, g_ref[...]).astype(BF16)
        acc_ref[...] = jnp.zeros_like(acc_ref)

    h = hn_ref[...]
    a = jnp.dot(h, wg_ref[...], preferred_element_type=F32)
    u = jnp.dot(h, wu_ref[...], preferred_element_type=F32)
    act = (a * jax.nn.sigmoid(a) * u).astype(BF16)
    acc_ref[...] += jnp.dot(act, wd_ref[...], preferred_element_type=F32)

    @pl.when(j == pl.num_programs(1) - 1)
    def _():
        o_ref[...] = x1_ref[...] + acc_ref[...]


def _l0_out_ffn(x2d, ret2d, dil2d, w_out, ffn_norm, w_gate, w_up, w_down, tm=512):
    t, d = x2d.shape
    f = w_gate.shape[1]
    tf = f // 2 if (f // 2) % LANES == 0 else f
    half = w_out.shape[0] // 2
    return pl.pallas_call(
        _l0_out_ffn_kernel,
        grid=(t // tm, f // tf),
        in_specs=[
            pl.BlockSpec((tm, d), lambda i, j: (i, 0)),
            pl.BlockSpec((tm, half), lambda i, j: (i, 0)),
            pl.BlockSpec((tm, half), lambda i, j: (i, 0)),
            pl.BlockSpec((2 * half, d), lambda i, j: (0, 0)),
            pl.BlockSpec((1, d), lambda i, j: (0, 0)),
            pl.BlockSpec((d, tf), lambda i, j: (0, j)),
            pl.BlockSpec((d, tf), lambda i, j: (0, j)),
            pl.BlockSpec((tf, d), lambda i, j: (j, 0)),
        ],
        out_specs=pl.BlockSpec((tm, d), lambda i, j: (i, 0)),
        out_shape=jax.ShapeDtypeStruct((t, d), F32),
        scratch_shapes=[pltpu.VMEM((tm, d), F32), pltpu.VMEM((tm, d), BF16), pltpu.VMEM((tm, d), F32)],
        compiler_params=_cparams(("parallel", "arbitrary")),
        name="l0_out_proj_ffn",
    )(x2d, ret2d, dil2d, w_out, ffn_norm.reshape(1, d), w_gate, w_up, w_down)


L1_W1_COLS = MLA_Q_RANK + MLA_KV_RANK + 2 * LANES + S5_CH
HEAD_BLOCK = LANES


def _l1_proj_kernel(x_ref, an_ref, w1_ref, qn_ref, wq_ref, kvn_ref, wkv_ref, ct_ref, st_ref,
                    q_out, k_out, v_out, u_out):
    xn = _rms(x_ref[0], an_ref[...]).astype(BF16)
    proj = jnp.dot(xn, w1_ref[...], preferred_element_type=F32)
    c0 = MLA_Q_RANK
    c1 = c0 + MLA_KV_RANK
    cq = proj[:, 0:c0]
    ckv = proj[:, c0:c1]
    ka = proj[:, c1:c1 + LANES]
    kb = proj[:, c1 + LANES:c1 + 2 * LANES]
    u_out[...] = proj[:, c1 + 2 * LANES:]

    ct = ct_ref[...]
    st = st_ref[...]
    ct8 = jnp.concatenate([ct] * MLA_HEADS, axis=1)
    st8 = jnp.concatenate([st] * MLA_HEADS, axis=1)
    width = MLA_HEADS * HEAD_BLOCK

    q2 = jnp.dot(_rms(cq, qn_ref[...]).astype(BF16), wq_ref[...], preferred_element_type=F32)
    scale = (MLA_NOPE + MLA_ROPE) ** -0.5 * math.log2(math.e)
    q_out[0] = ((q2[:, 0:width] * ct8 + q2[:, width:] * st8) * scale).astype(q_out.dtype)

    kv = jnp.dot(_rms(ckv, kvn_ref[...]).astype(BF16), wkv_ref[...], preferred_element_type=F32)
    krot = ka * ct + kb * st
    k_out[0] = (kv[:, 0:width] + jnp.concatenate([krot] * MLA_HEADS, axis=1)).astype(k_out.dtype)
    lane = lax.broadcasted_iota(jnp.int32, (1, width), 1)
    ones = jnp.where(lane % HEAD_BLOCK >= MLA_V, 1.0, 0.0)
    v_out[0] = (kv[:, width:] + ones).astype(v_out.dtype)


def _l1_weights(w_in, w_uq, w_ukv):
    c0 = MLA_Q_RANK
    c1 = c0 + MLA_KV_RANK
    c2 = c1 + MLA_ROPE
    hr = MLA_ROPE // 2
    kpe = w_in[:, c1:c2]
    zeros = lambda n: jnp.zeros((w_in.shape[0], n), w_in.dtype)
    ka = jnp.concatenate([zeros(MLA_NOPE), kpe, zeros(LANES - MLA_NOPE - MLA_ROPE)], axis=1)
    kb = jnp.concatenate([zeros(MLA_NOPE), kpe[:, hr:], kpe[:, :hr], zeros(LANES - MLA_NOPE - MLA_ROPE)], axis=1)
    w1 = jnp.concatenate([w_in[:, :c1], ka, kb, w_in[:, c2:]], axis=1).astype(BF16)

    qd = MLA_NOPE + MLA_ROPE
    wq = w_uq.reshape(MLA_Q_RANK, MLA_HEADS, qd)
    zq = lambda n: jnp.zeros((MLA_Q_RANK, MLA_HEADS, n), w_uq.dtype)
    qa = jnp.concatenate([wq, zq(HEAD_BLOCK - qd)], axis=2)
    qb = jnp.concatenate([zq(MLA_NOPE), wq[:, :, MLA_NOPE + hr:], wq[:, :, MLA_NOPE:MLA_NOPE + hr],
                          zq(HEAD_BLOCK - qd)], axis=2)
    wq2 = jnp.concatenate([qa.reshape(MLA_Q_RANK, -1), qb.reshape(MLA_Q_RANK, -1)], axis=1).astype(BF16)

    wkv = w_ukv.reshape(MLA_KV_RANK, MLA_HEADS, MLA_NOPE + MLA_V)
    kpart = jnp.concatenate([wkv[:, :, :MLA_NOPE],
                             jnp.zeros((MLA_KV_RANK, MLA_HEADS, HEAD_BLOCK - MLA_NOPE), w_ukv.dtype)], axis=2)
    vpart = jnp.concatenate([wkv[:, :, MLA_NOPE:],
                             jnp.zeros((MLA_KV_RANK, MLA_HEADS, HEAD_BLOCK - MLA_V), w_ukv.dtype)], axis=2)
    wkv2 = jnp.concatenate([kpart.reshape(MLA_KV_RANK, -1), vpart.reshape(MLA_KV_RANK, -1)], axis=1).astype(BF16)
    return w1, wq2, wkv2


def _rope_lane_tables(s):
    hr = MLA_ROPE // 2
    inv = ROPE_BASE ** (-jnp.arange(0, MLA_ROPE, 2, dtype=F32) / MLA_ROPE)
    ang = jnp.arange(s, dtype=F32)[:, None] * inv[None, :]
    cos, sin = jnp.cos(ang), jnp.sin(ang)
    pad = jnp.zeros((s, HEAD_BLOCK - MLA_NOPE - MLA_ROPE), F32)
    ct = jnp.concatenate([jnp.ones((s, MLA_NOPE), F32), cos, cos, pad], axis=1)
    st = jnp.concatenate([jnp.zeros((s, MLA_NOPE), F32), -sin, sin, pad], axis=1)
    assert ct.shape[1] == HEAD_BLOCK and hr * 2 == MLA_ROPE
    return ct, st


def _l1_proj(x, attn_norm, w1, q_norm, wq2, kv_norm, wkv2, ct, st, nb_pad, tm=512):
    b, s, d = x.shape
    width = MLA_HEADS * HEAD_BLOCK
    const = lambda shape: pl.BlockSpec(shape, lambda i, j: (0, 0))
    tok = lambda n: pl.BlockSpec((1, tm, n), lambda i, j: (i, j, 0))
    return pl.pallas_call(
        _l1_proj_kernel,
        grid=(b, s // tm),
        in_specs=[tok(d), const((1, d)), const(w1.shape), const((1, MLA_Q_RANK)), const(wq2.shape),
                  const((1, MLA_KV_RANK)), const(wkv2.shape),
                  pl.BlockSpec((tm, HEAD_BLOCK), lambda i, j: (j, 0)),
                  pl.BlockSpec((tm, HEAD_BLOCK), lambda i, j: (j, 0))],
        out_specs=[tok(width), tok(width), tok(width),
                   pl.BlockSpec((tm, S5_CH), lambda i, j: (j, i))],
        out_shape=[jax.ShapeDtypeStruct((b, s, width), BF16),
                   jax.ShapeDtypeStruct((b, s, width), BF16),
                   jax.ShapeDtypeStruct((b, s, width), BF16),
                   jax.ShapeDtypeStruct((s, nb_pad * S5_CH), F32)],
        compiler_params=_cparams(("parallel", "parallel")),
        name="l1_norm_in_proj",
    )(x, attn_norm.reshape(1, d), w1, q_norm.reshape(1, -1), wq2, kv_norm.reshape(1, -1), wkv2, ct, st)


def _mla_kernel(q_ref, k_ref, v_ref, o_ref, *, seq_len, tk, unroll):
    tq = q_ref.shape[1]
    qs = [q_ref[0, :, 0:HEAD_BLOCK], q_ref[0, :, HEAD_BLOCK:]]

    def body(j, carry):
        sl = pl.ds(pl.multiple_of(j * tk, tk), tk)
        new = []
        for hh in range(2):
            m, acc = carry[hh]
            hs = slice(hh * HEAD_BLOCK, (hh + 1) * HEAD_BLOCK)
            s = lax.dot_general(qs[hh], k_ref[0, sl, hs], (((1,), (1,)), ((), ())), preferred_element_type=F32)
            m_new = jnp.maximum(m, jnp.max(s, axis=-1, keepdims=True))
            p = jnp.exp2(s - m_new).astype(BF16)
            acc_new = jnp.exp2(m - m_new) * acc + jnp.dot(p, v_ref[0, sl, hs], preferred_element_type=F32)
            new.append((m_new, acc_new))
        return tuple(new)

    init = tuple((jnp.full((tq, 1), NEG_INF, F32), jnp.zeros((tq, HEAD_BLOCK), F32)) for _ in range(2))
    (_, a0), (_, a1) = lax.fori_loop(0, seq_len // tk, body, init, unroll=unroll)
    lane = lax.broadcasted_iota(jnp.int32, (tq, LANES), 1)
    o0 = a0 / pltpu.roll(a0, MLA_V, 1)
    o1 = a1 / pltpu.roll(a1, MLA_V, 1)
    o_ref[0] = jnp.where(lane < MLA_V, o0, pltpu.roll(o1, MLA_V, 1)).astype(o_ref.dtype)


def _mla_attention(q, k, v, tq=1024, tk=2048, unroll=2):
    b, s, _ = q.shape
    tq, tk = min(tq, s), min(tk, s)
    unroll = min(unroll, s // tk)
    hpairs = MLA_HEADS // 2
    pair = 2 * HEAD_BLOCK
    return pl.pallas_call(
        functools.partial(_mla_kernel, seq_len=s, tk=tk, unroll=unroll),
        grid=(b, hpairs, s // tq),
        in_specs=[pl.BlockSpec((1, tq, pair), lambda i, hp, j: (i, j, hp)),
                  pl.BlockSpec((1, s, pair), lambda i, hp, j: (i, 0, hp)),
                  pl.BlockSpec((1, s, pair), lambda i, hp, j: (i, 0, hp))],
        out_specs=pl.BlockSpec((1, tq, LANES), lambda i, hp, j: (i, j, hp)),
        out_shape=jax.ShapeDtypeStruct((b, s, MLA_HEADS * MLA_V), BF16),
        compiler_params=_cparams(("parallel", "parallel", "arbitrary")),
        name="l1_latent_attention",
    )(q, k, v)


S5_BLOCKS = S5_CH // LANES
S5_BLOCK_STATES = S5_NSTATE // S5_BLOCKS
S5_UNROLL = 4


def _s5_kernel(u_ref, bb_ref, are_ref, aim_ref, cb_ref, y_ref, bu_ref, x_ref, *, reverse, paired):
    tc, nb, _ = u_ref.shape
    n = S5_NSTATE
    sb = S5_BLOCK_STATES

    @pl.when(pl.program_id(0) == 0)
    def _():
        x_ref[...] = jnp.zeros_like(x_ref)

    u2 = u_ref[...].reshape(tc * nb, S5_CH)
    if paired:
        row = lax.broadcasted_iota(jnp.int32, (tc * nb, LANES), 0)
        is_fwd = (row % nb) < nb // 2
    for k in range(S5_BLOCKS):
        uk = u2[:, k * LANES:(k + 1) * LANES]
        if paired:
            zero = jnp.zeros_like(uk)
            uk = jnp.concatenate([jnp.where(is_fwd, uk, zero), jnp.where(is_fwd, zero, uk)], axis=1)
        r = jnp.dot(uk.astype(BF16), bb_ref[k], preferred_element_type=F32)
        bu_ref[:, :, k * sb:(k + 1) * sb] = r[:, 0:sb].reshape(tc, nb, sb)
        bu_ref[:, :, n + k * sb:n + (k + 1) * sb] = r[:, sb:].reshape(tc, nb, sb)

    def step(i, carry):
        t = tc - 1 - i if reverse else i
        xr, xi = carry
        a_re = are_ref[...]
        a_im = aim_ref[...]
        nr = a_re * xr - a_im * xi + bu_ref[t, :, 0:n]
        ni = a_re * xi + a_im * xr + bu_ref[t, :, n:]
        bu_ref[t, :, 0:n] = nr
        bu_ref[t, :, n:] = ni
        return nr, ni

    xr, xi = lax.fori_loop(0, tc, step, (x_ref[:, 0:n], x_ref[:, n:]), unroll=S5_UNROLL)
    x_ref[:, 0:n] = xr
    x_ref[:, n:] = xi

    for k in range(S5_BLOCKS):
        xs = jnp.concatenate([bu_ref[:, :, k * sb:(k + 1) * sb], bu_ref[:, :, n + k * sb:n + (k + 1) * sb]], axis=2)
        yk = jnp.dot(xs.reshape(tc * nb, 2 * sb).astype(BF16), cb_ref[k], preferred_element_type=F32)
        if paired:
            yk = jnp.where(is_fwd, yk[:, 0:LANES], yk[:, LANES:])
        y_ref[:, :, k * LANES:(k + 1) * LANES] = yk.reshape(tc, nb, LANES)


def _s5_direction_params(lam_re, lam_im, log_step, b_re, b_im, c_re, c_im, nb):
    lam = lax.complex(jnp.minimum(lam_re.astype(F32), -1e-4), lam_im.astype(F32))
    step = jnp.exp(log_step.astype(F32))[:, None]
    lam_bar = jnp.exp(lam * step)
    bmat = lax.complex(b_re.astype(F32), b_im.astype(F32))
    b_bar = ((lam_bar - 1.0) / lam)[:, :, None] * bmat
    gpb = S5_GROUPS // S5_BLOCKS
    eye = jnp.eye(gpb, dtype=F32)

    def in_map(t):
        t = t.reshape(S5_BLOCKS, gpb, S5_STATE, S5_GROUP_CH)
        return jnp.einsum('kgpc,gh->kgchp', t, eye).reshape(S5_BLOCKS, LANES, S5_BLOCK_STATES)

    def out_map(t):
        t = t.reshape(S5_BLOCKS, gpb, S5_GROUP_CH, S5_STATE)
        return jnp.einsum('kgcp,gh->kgphc', t, eye).reshape(S5_BLOCKS, S5_BLOCK_STATES, LANES)

    bblk = jnp.concatenate([in_map(jnp.real(b_bar)), in_map(jnp.imag(b_bar))], axis=2).astype(BF16)
    cblk = jnp.concatenate([out_map(c_re.astype(F32)), -out_map(c_im.astype(F32))], axis=1).astype(BF16)
    a_re = jnp.broadcast_to(jnp.real(lam_bar).reshape(1, S5_NSTATE), (nb, S5_NSTATE)).astype(F32)
    a_im = jnp.broadcast_to(jnp.imag(lam_bar).reshape(1, S5_NSTATE), (nb, S5_NSTATE)).astype(F32)
    return bblk, a_re, a_im, cblk


def _s5_pair_params(fwd, bwd):
    half = fwd[1].shape[0] // 2
    return (jnp.concatenate([fwd[0], bwd[0]], axis=1),
            jnp.concatenate([fwd[1][:half], bwd[1][:half]], axis=0),
            jnp.concatenate([fwd[2][:half], bwd[2][:half]], axis=0),
            jnp.concatenate([fwd[3], bwd[3]], axis=2))


def _s5_scan(u_tm, params, reverse, paired=False, tc=64):
    s, nb, _ = u_tm.shape
    bdense, a_re, a_im, cdense = params
    nchunks = s // tc
    tmap = (lambda i: (nchunks - 1 - i, 0, 0)) if reverse else (lambda i: (i, 0, 0))
    const = lambda shape: pl.BlockSpec(shape, lambda i: (0,) * len(shape))
    return pl.pallas_call(
        functools.partial(_s5_kernel, reverse=reverse, paired=paired),
        grid=(nchunks,),
        in_specs=[pl.BlockSpec((tc, nb, S5_CH), tmap), const(bdense.shape), const(a_re.shape),
                  const(a_im.shape), const(cdense.shape)],
        out_specs=pl.BlockSpec((tc, nb, S5_CH), tmap),
        out_shape=jax.ShapeDtypeStruct((s, nb, S5_CH), F32),
        scratch_shapes=[pltpu.VMEM((tc, nb, 2 * S5_NSTATE), F32), pltpu.VMEM((nb, 2 * S5_NSTATE), F32)],
        compiler_params=_cparams(("arbitrary",)),
        name="l1_s5_scan_pair" if paired else ("l1_s5_scan_bwd" if reverse else "l1_s5_scan_fwd"),
    )(u_tm, bdense, a_re, a_im, cdense)


def _l1_out_router_kernel(x_ref, mla_ref, yf_ref, yb_ref, u_ref, dskip_ref, gw_ref, gb_ref, wo_ref,
                          fn_ref, rt_ref, x1_out, route_out):
    tm = x_ref.shape[1]
    u = u_ref[...]
    y = yf_ref[...] + yb_ref[...] + dskip_ref[...] * u
    z = jax.nn.gelu(y)
    gate = jax.nn.sigmoid(jnp.dot(z.astype(BF16), gw_ref[...], preferred_element_type=F32) + gb_ref[...])
    ssm = (z * gate).astype(BF16)
    half = wo_ref.shape[0] // 2
    x1 = (x_ref[0]
          + jnp.dot(mla_ref[0], wo_ref[0:half, :], preferred_element_type=F32)
          + jnp.dot(ssm, wo_ref[half:, :], preferred_element_type=F32))
    _store_token_slabs(x1_out.at[0], x1, tm)
    hn = _rms(x1, fn_ref[...])

    h_hi = hn.astype(BF16)
    h_lo = (hn - h_hi.astype(F32)).astype(BF16)
    logits = (jnp.dot(h_hi, rt_ref[0], preferred_element_type=F32)
              + jnp.dot(h_hi, rt_ref[1], preferred_element_type=F32)
              + jnp.dot(h_lo, rt_ref[0], preferred_element_type=F32))
    lane = lax.broadcasted_iota(jnp.int32, logits.shape, 1)
    lg = jnp.where(lane < N_EXPERTS, logits, -jnp.inf)
    m1 = jnp.max(lg, axis=-1, keepdims=True)
    i1 = jnp.min(jnp.where(lg == m1, lane, LANES), axis=-1, keepdims=True)
    lg2 = jnp.where(lane == i1, -jnp.inf, lg)
    m2 = jnp.max(lg2, axis=-1, keepdims=True)
    i2 = jnp.min(jnp.where(lg2 == m2, lane, LANES), axis=-1, keepdims=True)
    e2 = jnp.exp(m2 - m1)
    w1 = 1.0 / (1.0 + e2)
    w2 = e2 / (1.0 + e2)
    route_out[0] = jnp.where(lane == 0, i1.astype(F32),
                             jnp.where(lane == 1, i2.astype(F32),
                                       jnp.where(lane == 2, w1, jnp.where(lane == 3, w2, 0.0))))


def _l1_out_router(x, mla, yf, yb, u_tm, d_skip, glu_w, glu_b, w_out, ffn_norm, router_pad, tm=256):
    b, s, d = x.shape
    tok = lambda n: pl.BlockSpec((1, tm, n), lambda i, j: (i, j, 0))
    tmaj = pl.BlockSpec((tm, S5_CH), lambda i, j: (j, i))
    const = lambda shape: pl.BlockSpec(shape, lambda i, j: (0,) * len(shape))
    return pl.pallas_call(
        _l1_out_router_kernel,
        grid=(b, s // tm),
        in_specs=[tok(d), tok(MLA_HEADS * MLA_V), tmaj, tmaj, tmaj, const((1, S5_CH)),
                  const(glu_w.shape), const((1, S5_CH)), const(w_out.shape), const((1, d)),
                  const(router_pad.shape)],
        out_specs=[pl.BlockSpec((1, tm * SLAB_ROWS, LANES), lambda i, j: (i, j, 0)), tok(LANES)],
        out_shape=[jax.ShapeDtypeStruct((b, s * SLAB_ROWS, LANES), F32),
                   jax.ShapeDtypeStruct((b, s, LANES), F32)],
        compiler_params=_cparams(("parallel", "parallel")),
        name="l1_out_proj_router",
    )(x, mla, yf, yb, u_tm, d_skip.reshape(1, -1), glu_w, glu_b.reshape(1, -1), w_out,
      ffn_norm.reshape(1, d), router_pad)


def _start_row_gather(idx_ref, src_hbm, dst, sem, n_rows):
    def issue(r, carry):
        src = pl.multiple_of(idx_ref[r] * SLAB_ROWS, SLAB_ROWS)
        pltpu.make_async_copy(src_hbm.at[pl.ds(src, SLAB_ROWS)],
                              dst.at[pl.ds(pl.multiple_of(r * SLAB_ROWS, SLAB_ROWS), SLAB_ROWS)], sem).start()
        return carry

    lax.fori_loop(0, n_rows, issue, 0, unroll=8)


def _wait_row_gather(src_hbm, dst, sem, n_rows):
    pltpu.make_async_copy(src_hbm.at[pl.ds(0, n_rows * SLAB_ROWS)], dst, sem).wait()


def _expert_ffn_kernel(te_ref, nt_ref, idx0_ref, idxn_ref, x_hbm, gw_ref, g_ref, wg_ref, wu_ref, wd_ref,
                       o_ref, xbuf, sem, hn_ref, acc_ref):
    i = pl.program_id(0)
    j = pl.program_id(1)
    last = pl.num_programs(1) - 1
    tm = hn_ref.shape[0]
    n_used = nt_ref[0]
    slot = i % 2

    @pl.when(jnp.logical_and(i == 0, j == 0))
    def _():
        _start_row_gather(idx0_ref, x_hbm, xbuf.at[0], sem.at[0], tm)

    @pl.when(jnp.logical_and(j == 0, i < n_used))
    def _():
        _wait_row_gather(x_hbm, xbuf.at[slot], sem.at[slot], tm)
        x = _load_token_slabs(xbuf.at[slot], tm)
        hn_ref[...] = _rms(x, g_ref[...]).astype(BF16)
        acc_ref[...] = jnp.zeros_like(acc_ref)

    @pl.when(jnp.logical_and(j == 0, i + 1 < n_used))
    def _():
        _start_row_gather(idxn_ref, x_hbm, xbuf.at[1 - slot], sem.at[1 - slot], tm)

    @pl.when(i < n_used)
    def _():
        h = hn_ref[...]
        a = jnp.dot(h, wg_ref[0], preferred_element_type=F32)
        u = jnp.dot(h, wu_ref[0], preferred_element_type=F32)
        act = (a * jax.nn.sigmoid(a) * u).astype(BF16)
        acc_ref[...] += jnp.dot(act, wd_ref[0], preferred_element_type=F32)

        @pl.when(j == last)
        def _():
            _store_token_slabs(o_ref, gw_ref[...] * acc_ref[...], tm)

    @pl.when(jnp.logical_and(i >= n_used, j == last))
    def _():
        o_ref[...] = jnp.zeros_like(o_ref)


def _expert_ffn(x_slabs, src_tok, gate_w, tile_expert, n_tiles_used, ffn_norm, wg, wu, wd, tm, tf=1792):
    a_pad = src_tok.shape[0]
    n_tiles = a_pad // tm
    d = D_MODEL
    f = wg.shape[2]
    grid_spec = pltpu.PrefetchScalarGridSpec(
        num_scalar_prefetch=2,
        grid=(n_tiles, f // tf),
        in_specs=[
            pl.BlockSpec((tm,), lambda i, j, te, nt: (0,), memory_space=pltpu.SMEM),
            pl.BlockSpec((tm,), lambda i, j, te, nt: (jnp.minimum(i + 1, n_tiles - 1),), memory_space=pltpu.SMEM),
            pl.BlockSpec(memory_space=pl.ANY),
            pl.BlockSpec((tm, 1), lambda i, j, te, nt: (i, 0)),
            pl.BlockSpec((1, d), lambda i, j, te, nt: (0, 0)),
            pl.BlockSpec((1, d, tf), lambda i, j, te, nt: (te[i], 0, j)),
            pl.BlockSpec((1, d, tf), lambda i, j, te, nt: (te[i], 0, j)),
            pl.BlockSpec((1, tf, d), lambda i, j, te, nt: (te[i], j, 0)),
        ],
        out_specs=pl.BlockSpec((tm * SLAB_ROWS, LANES), lambda i, j, te, nt: (i, 0)),
        scratch_shapes=[pltpu.VMEM((2, tm * SLAB_ROWS, LANES), F32), pltpu.SemaphoreType.DMA((2,)),
                        pltpu.VMEM((tm, d), BF16), pltpu.VMEM((tm, d), F32)],
    )
    return pl.pallas_call(
        _expert_ffn_kernel,
        grid_spec=grid_spec,
        out_shape=jax.ShapeDtypeStruct((a_pad * SLAB_ROWS, LANES), F32),
        compiler_params=_cparams(("arbitrary", "arbitrary")),
        name="l1_expert_ffn",
    )(tile_expert, n_tiles_used, src_tok, src_tok, x_slabs, gate_w, ffn_norm.reshape(1, d), wg, wu, wd)


def _combine_norm_kernel(ia0_ref, ib0_ref, ian_ref, ibn_ref, x_ref, ys_hbm, g_ref, o_ref, buf, sem):
    tm = o_ref.shape[0]
    i = pl.program_id(0)
    slot = i % 2

    @pl.when(i == 0)
    def _():
        _start_row_gather(ia0_ref, ys_hbm, buf.at[0, 0], sem.at[0, 0], tm)
        _start_row_gather(ib0_ref, ys_hbm, buf.at[0, 1], sem.at[0, 1], tm)

    @pl.when(i + 1 < pl.num_programs(0))
    def _():
        _start_row_gather(ian_ref, ys_hbm, buf.at[1 - slot, 0], sem.at[1 - slot, 0], tm)
        _start_row_gather(ibn_ref, ys_hbm, buf.at[1 - slot, 1], sem.at[1 - slot, 1], tm)

    _wait_row_gather(ys_hbm, buf.at[slot, 0], sem.at[slot, 0], tm)
    _wait_row_gather(ys_hbm, buf.at[slot, 1], sem.at[slot, 1], tm)
    y = _load_token_slabs(x_ref, tm) + (_load_token_slabs(buf.at[slot, 0], tm)
                                        + _load_token_slabs(buf.at[slot, 1], tm))
    o_ref[...] = _rms(y, g_ref[...])


def _combine_norm(x_slabs, ys_slabs, dest, final_norm, tm=256):
    t = x_slabs.shape[0] // SLAB_ROWS
    nblk = t // tm
    slab = (tm * SLAB_ROWS, LANES)
    return pl.pallas_call(
        _combine_norm_kernel,
        grid=(nblk,),
        in_specs=[pl.BlockSpec((tm,), lambda i: (0,), memory_space=pltpu.SMEM),
                  pl.BlockSpec((tm,), lambda i: (nblk,), memory_space=pltpu.SMEM),
                  pl.BlockSpec((tm,), lambda i: (jnp.minimum(i + 1, nblk - 1),), memory_space=pltpu.SMEM),
                  pl.BlockSpec((tm,), lambda i: (jnp.minimum(i + 1, nblk - 1) + nblk,), memory_space=pltpu.SMEM),
                  pl.BlockSpec(slab, lambda i: (i, 0)),
                  pl.BlockSpec(memory_space=pl.ANY),
                  pl.BlockSpec((1, D_MODEL), lambda i: (0, 0))],
        out_specs=pl.BlockSpec((tm, D_MODEL), lambda i: (i, 0)),
        out_shape=jax.ShapeDtypeStruct((t, D_MODEL), F32),
        scratch_shapes=[pltpu.VMEM((2, 2) + slab, F32), pltpu.SemaphoreType.DMA((2, 2))],
        compiler_params=_cparams(("arbitrary",)),
        name="l1_combine_final_norm",
    )(dest, dest, dest, dest, x_slabs, ys_slabs, final_norm.reshape(1, D_MODEL))


def _moe(x_slabs, route, ffn_norm, wg, wu, wd, final_norm, tm=512):
    t = x_slabs.shape[0] // SLAB_ROWS
    e_idx = jnp.concatenate([route[:, 0], route[:, 1]]).astype(jnp.int32)
    e_w = jnp.concatenate([route[:, 2], route[:, 3]])
    n_assign = TOP_K * t
    order = jnp.argsort(e_idx, stable=True).astype(jnp.int32)
    inv = jnp.argsort(order).astype(jnp.int32)
    counts = jnp.sum(e_idx[:, None] == jnp.arange(N_EXPERTS, dtype=jnp.int32)[None, :], axis=0).astype(jnp.int32)
    starts = jnp.cumsum(counts) - counts
    padded = ((counts + tm - 1) // tm) * tm
    pad_ends = jnp.cumsum(padded)
    pad_starts = pad_ends - padded
    a_pad = n_assign + N_EXPERTS * tm
    n_tiles = a_pad // tm

    tile_start = jnp.arange(n_tiles, dtype=jnp.int32) * tm
    tile_expert = jnp.minimum(jnp.sum(tile_start[:, None] >= pad_ends[None, :], axis=1), N_EXPERTS - 1).astype(jnp.int32)
    n_tiles_used = (pad_ends[-1] // tm).astype(jnp.int32).reshape(1)

    slot = jnp.arange(a_pad, dtype=jnp.int32)
    slot_e = jnp.repeat(tile_expert, tm)
    within = slot - pad_starts[slot_e]
    valid = within < counts[slot_e]
    src = order[jnp.clip(starts[slot_e] + within, 0, n_assign - 1)]
    src_tok = jnp.where(valid, src % t, 0).astype(jnp.int32)
    gate_w = jnp.where(valid, e_w[src], 0.0).astype(F32).reshape(a_pad, 1)

    ys = _expert_ffn(x_slabs, src_tok, gate_w, tile_expert, n_tiles_used, ffn_norm, wg, wu, wd, tm)
    dest = (pad_starts[e_idx] + inv - starts[e_idx]).astype(jnp.int32)
    return _combine_norm(x_slabs, ys, dest, final_norm)


def _l0_in_weight(w_in):
    hq = RET_HEADS * RET_DK
    hv = RET_HEADS * RET_DV
    hd = DIL_HEADS * DIL_DH
    rq, rk, rv, rg, dq, dk, dv = jnp.split(
        w_in, [hq, 2 * hq, 2 * hq + hv, 2 * hq + 2 * hv, 2 * hq + 2 * hv + hd, 2 * hq + 2 * hv + 2 * hd], axis=1)
    dup = lambda t: jnp.concatenate([t.reshape(-1, RET_HEADS, 1, RET_DK)] * 2, axis=2).reshape(-1, 2 * hq)
    w = jnp.concatenate([dup(rq) * RET_DK ** -0.5, dup(rk), rv, rg, dq * DIL_DH ** -0.5, dk, dv], axis=1)
    assert w.shape[1] == L0_COLS
    return w.astype(BF16)


def _prepare(l0_w_in, l0_ret_decay_f, l0_ret_decay_b, l0_w_out, l0_ffn_w_gate, l0_ffn_w_up, l0_ffn_w_down,
             l1_w_in, l1_mla_w_uq, l1_mla_w_ukv, l1_s5_glu_w, l1_w_out, l1_router,
             l1_exp_w_gate, l1_exp_w_up, l1_exp_w_down):
    w1, wq2, wkv2 = _l1_weights(l1_w_in, l1_mla_w_uq, l1_mla_w_ukv)
    router_f32 = jnp.zeros((D_MODEL, LANES), F32).at[:, :N_EXPERTS].set(l1_router.astype(F32))
    router_hi = router_f32.astype(BF16)
    router_pad = jnp.stack([router_hi, (router_f32 - router_hi.astype(F32)).astype(BF16)])
    return dict(
        l0_w_in=_l0_in_weight(l0_w_in),
        ret_tables=_retention_tables(l0_ret_decay_f, l0_ret_decay_b),
        l0_w_out=l0_w_out.astype(BF16),
        l0_wg=l0_ffn_w_gate.astype(BF16), l0_wu=l0_ffn_w_up.astype(BF16), l0_wd=l0_ffn_w_down.astype(BF16),
        l1_w1=w1, l1_wq2=wq2, l1_wkv2=wkv2,
        glu_w=l1_s5_glu_w.astype(BF16), l1_w_out=l1_w_out.astype(BF16), router_pad=router_pad,
        exp_wg=l1_exp_w_gate.astype(BF16), exp_wu=l1_exp_w_up.astype(BF16), exp_wd=l1_exp_w_down.astype(BF16),
    )


def _trunk(x, prep, p):
    b, s, d = x.shape
    t = b * s
    p0, *regrouped = _l0_in_proj(x.reshape(t, d), p['l0_attn_norm'], prep['l0_w_in'])
    p0 = p0.reshape(b, s, L0_COLS)
    ret = _retention(p0, prep['ret_tables'], p['l0_ret_gn'])
    dil = _dilated_attention(p0, regrouped)
    x = _l0_out_ffn(x.reshape(t, d), ret.reshape(t, -1), dil.reshape(t, -1), prep['l0_w_out'],
                    p['l0_ffn_norm'], prep['l0_wg'], prep['l0_wu'], prep['l0_wd']).reshape(b, s, d)
    nb = -(-b // SUBLANES) * SUBLANES
    ct, st = _rope_lane_tables(s)
    q, k, v, u_tm = _l1_proj(x, p['l1_attn_norm'], prep['l1_w1'], p['l1_mla_q_norm'], prep['l1_wq2'],
                             p['l1_mla_kv_norm'], prep['l1_wkv2'], ct, st, b)
    mla = _mla_attention(q, k, v)
    u3 = u_tm.reshape(s, b, S5_CH)
    sf = _s5_direction_params(p['l1_s5_lam_re_f'], p['l1_s5_lam_im_f'], p['l1_s5_log_step_f'],
                              p['l1_s5_b_re'], p['l1_s5_b_im'], p['l1_s5_c_re'], p['l1_s5_c_im'], nb)
    sb = _s5_direction_params(p['l1_s5_lam_re_b'], p['l1_s5_lam_im_b'], p['l1_s5_log_step_b'],
                              p['l1_s5_b_re'], p['l1_s5_b_im'], p['l1_s5_c_re'], p['l1_s5_c_im'], nb)
    if 2 * b == SUBLANES:
        y = _s5_scan(jnp.concatenate([u3, jnp.flip(u3, axis=0)], axis=1), _s5_pair_params(sf, sb),
                     reverse=False, paired=True)
        yf = y[:, :b].reshape(s, b * S5_CH)
        yb = jnp.flip(y[:, b:], axis=0).reshape(s, b * S5_CH)
    else:
        if nb != b:
            u3 = jnp.pad(u3, ((0, 0), (0, nb - b), (0, 0)))
        yf = _s5_scan(u3, sf, reverse=False).reshape(s, nb * S5_CH)
        yb = _s5_scan(u3, sb, reverse=True).reshape(s, nb * S5_CH)
    x1s, route = _l1_out_router(x, mla, yf, yb, u_tm, p['l1_s5_d'], prep['glu_w'], p['l1_s5_glu_b'],
                                prep['l1_w_out'], p['l1_ffn_norm'], prep['router_pad'])
    out = _moe(x1s.reshape(t * SLAB_ROWS, LANES), route.reshape(t, LANES), p['l1_ffn_norm'],
               prep['exp_wg'], prep['exp_wu'], prep['exp_wd'], p['final_norm'])
    return out.reshape(b, s, d)


def kernel(x_prompt, x_sample, l0_attn_norm, l0_w_in, l0_ret_decay_f, l0_ret_decay_b, l0_ret_gn, l0_w_out,
           l0_ffn_norm, l0_ffn_w_gate, l0_ffn_w_up, l0_ffn_w_down, l1_attn_norm, l1_w_in, l1_mla_q_norm,
           l1_mla_w_uq, l1_mla_kv_norm, l1_mla_w_ukv, l1_s5_lam_re_f, l1_s5_lam_im_f, l1_s5_log_step_f,
           l1_s5_lam_re_b, l1_s5_lam_im_b, l1_s5_log_step_b, l1_s5_b_re, l1_s5_b_im, l1_s5_c_re, l1_s5_c_im,
           l1_s5_d, l1_s5_glu_w, l1_s5_glu_b, l1_w_out, l1_ffn_norm, l1_router, l1_exp_w_gate, l1_exp_w_up,
           l1_exp_w_down, final_norm):
    p = dict(locals())
    prep = _prepare(l0_w_in, l0_ret_decay_f, l0_ret_decay_b, l0_w_out, l0_ffn_w_gate, l0_ffn_w_up,
                    l0_ffn_w_down, l1_w_in, l1_mla_w_uq, l1_mla_w_ukv, l1_s5_glu_w, l1_w_out, l1_router,
                    l1_exp_w_gate, l1_exp_w_up, l1_exp_w_down)
    return (_trunk(x_prompt, prep, p), _trunk(x_sample, prep, p))
```

```python
import functools
import math

import jax
import jax.numpy as jnp
from jax import lax
from jax.experimental import pallas as pl
from jax.experimental.pallas import tpu as pltpu

F32 = jnp.float32
BF16 = jnp.bfloat16

D_MODEL = 1024
EPS = 1e-6
NEG_INF = -1e30
RET_HEADS = 4
RET_DK = 64
RET_DV = 128
RET_CHUNK = 128
RET_UNROLL = 32
DIL_HEADS = 8
DIL_DH = 64
DIL_PATTERNS = ((128, 1), (512, 4), (2048, 16))
DIL_QBLOCK = 128
DIL_UNROLL = 16
MLA_HEADS = 8
MLA_Q_RANK = 256
MLA_KV_RANK = 128
MLA_NOPE = 64
MLA_ROPE = 32
MLA_V = 64
ROPE_BASE = 10000.0
S5_GROUPS = 32
S5_GROUP_CH = 16
S5_STATE = 64
S5_CH = S5_GROUPS * S5_GROUP_CH
S5_NSTATE = S5_GROUPS * S5_STATE
N_EXPERTS = 8
TOP_K = 2

LANES = 128
SUBLANES = 8
VMEM_LIMIT = 56 * 1024 * 1024

TM_L0_PROJ = 512
TM_L0_FFN = 512
TM_L1_PROJ = 512
TM_ROUTER = 256
MLA_TQ, MLA_TK, MLA_UNROLL = 1024, 2048, 2
S5_TC = 128
TM_EXPERT = 512
TF_EXPERT = 1792
TM_COMBINE = 256

L0_COLS = 3584
L0_BLOCKS = L0_COLS // LANES
L0_RQ, L0_RK, L0_RV, L0_RG, L0_DQ, L0_DK, L0_DV = 0, 4, 8, 12, 16, 20, 24


def _cparams(sem):
    return pltpu.CompilerParams(dimension_semantics=sem, vmem_limit_bytes=VMEM_LIMIT)


def _rms(x, g):
    return x * lax.rsqrt(jnp.mean(x * x, axis=-1, keepdims=True) + EPS) * g


SLAB_ROWS = D_MODEL // LANES


def _store_token_slabs(ref2d, x, n_tok):
    for s in range(SLAB_ROWS):
        ref2d[pl.ds(s, n_tok, stride=SLAB_ROWS), :] = x[:, s * LANES:(s + 1) * LANES]


def _load_token_slabs(ref2d, n_tok):
    return jnp.concatenate([ref2d[pl.ds(s, n_tok, stride=SLAB_ROWS), :] for s in range(SLAB_ROWS)], axis=1)


DIL_COLS = 3 * DIL_HEADS * DIL_DH
DIL_COL0 = L0_DQ * LANES
DIL_STRIDED = tuple(dil for _, dil in DIL_PATTERNS if dil > 1)


def _l0_in_proj_kernel(x_ref, g_ref, w_ref, o_ref, *rest):
    dil_refs, dsc = rest[:-1], rest[-1]
    tm = x_ref.shape[0]
    xn = _rms(x_ref[...], g_ref[...]).astype(BF16)
    res = jnp.dot(xn, w_ref[...], preferred_element_type=F32)
    o_ref[...] = res.astype(o_ref.dtype)
    for c in range(DIL_COLS // LANES):
        dsc[c] = res[:, DIL_COL0 + c * LANES:DIL_COL0 + (c + 1) * LANES]
    for ref, dil in zip(dil_refs, DIL_STRIDED):
        for r in range(dil):
            for c in range(DIL_COLS // LANES):
                lo = r * DIL_COLS + c * LANES
                ref[:, lo:lo + LANES] = dsc[c, pl.ds(r, tm // dil, stride=dil), :].astype(ref.dtype)


def _l0_in_proj(x2d, gain, w, tm=TM_L0_PROJ):
    t, d = x2d.shape
    n = w.shape[1]
    dil_specs = [pl.BlockSpec((tm // dil, dil * DIL_COLS), lambda i: (i, 0)) for dil in DIL_STRIDED]
    dil_shapes = [jax.ShapeDtypeStruct((t // dil, dil * DIL_COLS), BF16) for dil in DIL_STRIDED]
    return pl.pallas_call(
        _l0_in_proj_kernel,
        grid=(t // tm,),
        in_specs=[
            pl.BlockSpec((tm, d), lambda i: (i, 0)),
            pl.BlockSpec((1, d), lambda i: (0, 0)),
            pl.BlockSpec((d, n), lambda i: (0, 0)),
        ],
        out_specs=[pl.BlockSpec((tm, n), lambda i: (i, 0))] + dil_specs,
        out_shape=[jax.ShapeDtypeStruct((t, n), BF16)] + dil_shapes,
        scratch_shapes=[pltpu.VMEM((DIL_COLS // LANES, tm, LANES), F32)],
        compiler_params=_cparams(("parallel",)),
        name="l0_norm_in_proj",
    )(x2d, gain.reshape(1, d), w)


def _retention_kernel(q_ref, k_ref, v_ref, g_ref, d_ref, qw_ref, kw_ref, cd_ref, gn_ref,
                      o_ref, kv_ref, p_ref, *, n_chunks):
    c = RET_CHUNK
    kw = kw_ref[0]
    qw = qw_ref[0]
    dmat = d_ref[0]
    gn = gn_ref[...]

    def kv_body(n, carry):
        sl = pl.ds(pl.multiple_of(n * c, c), c)
        kc = (k_ref[0, sl, :].astype(F32) * kw).T.astype(BF16)
        kv_ref[n] = jnp.dot(kc, v_ref[0, sl, :], preferred_element_type=F32)
        return carry

    lax.fori_loop(0, n_chunks, kv_body, 0, unroll=RET_UNROLL)

    half = c // 2
    dec_f = cd_ref[0, 0:half, :]
    dec_b = cd_ref[0, half:c, :]

    def fwd_body(n, s):
        p_ref[n, 0:half, :] = s.astype(BF16)
        return s * dec_f + kv_ref[n, 0:half, :]

    lax.fori_loop(0, n_chunks, fwd_body, jnp.zeros((half, RET_DV), F32))

    def bwd_body(i, s):
        n = n_chunks - 1 - i
        p_ref[n, half:c, :] = s.astype(BF16)
        return s * dec_b + kv_ref[n, half:c, :]

    lax.fori_loop(0, n_chunks, bwd_body, jnp.zeros((half, RET_DV), F32))

    def out_body(n, carry):
        sl = pl.ds(pl.multiple_of(n * c, c), c)
        qc = q_ref[0, sl, :]
        s = lax.dot_general(qc, k_ref[0, sl, :], (((1,), (1,)), ((), ())), preferred_element_type=F32)
        intra = jnp.dot((s * dmat).astype(BF16), v_ref[0, sl, :], preferred_element_type=F32)
        qq = (qc.astype(F32) * qw).astype(BF16)
        y = intra + jnp.dot(qq, p_ref[n], preferred_element_type=F32)
        mu = jnp.mean(y, axis=-1, keepdims=True)
        yc = y - mu
        var = jnp.mean(yc * yc, axis=-1, keepdims=True)
        yn = yc * lax.rsqrt(var + EPS) * gn
        gg = g_ref[0, sl, :].astype(F32)
        o_ref[0, sl, :] = (gg * jax.nn.sigmoid(gg) * yn).astype(o_ref.dtype)
        return carry

    lax.fori_loop(0, n_chunks, out_body, 0, unroll=RET_UNROLL)


def _retention_tables(decay_f, decay_b):
    c = RET_CHUNK
    lg_f = jax.nn.log_sigmoid(decay_f.astype(F32))[:, None, None]
    lg_b = jax.nn.log_sigmoid(decay_b.astype(F32))[:, None, None]
    j = jnp.arange(c, dtype=F32)
    diff = j[:, None] - j[None, :]
    dmat = 0.5 * jnp.where(diff >= 0, jnp.exp(lg_f * jnp.maximum(diff, 0.0)),
                           jnp.exp(lg_b * jnp.maximum(-diff, 0.0)))
    lane_f = (jnp.arange(LANES) < RET_DK)[None, None, :]
    jj = j[None, :, None]
    qw = jnp.where(lane_f, jnp.exp(lg_f * (jj + 1.0)), jnp.exp(lg_b * (c - jj)))
    kw = jnp.where(lane_f, jnp.exp(lg_f * (c - 1.0 - jj)), jnp.exp(lg_b * jj))
    row_f = (jnp.arange(c) < c // 2)[None, :, None]
    cd = jnp.where(row_f, jnp.exp(lg_f * c), jnp.exp(lg_b * c)) * jnp.ones((1, 1, RET_DV), F32)
    return dmat.astype(F32), qw.astype(F32), kw.astype(F32), cd.astype(F32)


def _retention(p0, tables, gn):
    b, s, _ = p0.shape
    dmat, qw, kw, cd = tables
    n_chunks = s // RET_CHUNK
    seq = lambda col: pl.BlockSpec((1, s, LANES), lambda i, h: (i, 0, col + h))
    tab = pl.BlockSpec((1, RET_CHUNK, LANES), lambda i, h: (h, 0, 0))
    return pl.pallas_call(
        functools.partial(_retention_kernel, n_chunks=n_chunks),
        grid=(b, RET_HEADS),
        in_specs=[seq(L0_RQ), seq(L0_RK), seq(L0_RV), seq(L0_RG), tab, tab, tab, tab,
                  pl.BlockSpec((1, LANES), lambda i, h: (0, h))],
        out_specs=pl.BlockSpec((1, s, LANES), lambda i, h: (i, 0, h)),
        out_shape=jax.ShapeDtypeStruct((b, s, RET_HEADS * RET_DV), BF16),
        scratch_shapes=[pltpu.VMEM((n_chunks, RET_CHUNK, RET_DV), F32),
                        pltpu.VMEM((n_chunks, RET_CHUNK, RET_DV), BF16)],
        compiler_params=_cparams(("parallel", "parallel")),
        name="l0_retention",
    )(p0, p0, p0, p0, dmat, qw, kw, cd, gn.reshape(1, -1))


def _dilated_kernel(*refs, seq_len, half, q_rows, n_hp, has_prev, final):
    q_ref, k_ref, v_ref, bias_ref = refs[:4]
    refs = refs[4:]
    ratio = DIL_CHAIN_RATIO
    if has_prev:
        prev_o, prev_l, refs = refs[:ratio], refs[ratio:2 * ratio], refs[2 * ratio:]
        po_ref, pl_ref = refs[-2:]
        refs = refs[:-2]
        for m in range(ratio):
            for hp in range(n_hp):
                ls = slice(hp * LANES, (hp + 1) * LANES)
                po_ref[hp, pl.ds(m, q_rows // ratio, stride=ratio), :] = prev_o[m][0, :, ls]
                pl_ref[hp, pl.ds(m, q_rows // ratio, stride=ratio), :] = prev_l[m][0, :, ls]
    o_ref = refs[0]
    lse_ref = None if final else refs[1]
    tq = DIL_QBLOCK
    kw = tq + 2 * half
    base = pl.program_id(3) * q_rows
    lane = lax.broadcasted_iota(jnp.int32, (tq, LANES), 1)
    lo = lane < DIL_DH

    def blk(i, carry):
        off = pl.multiple_of(i * tq, tq)
        rows = pl.ds(off, tq)
        qs = base + off
        ks = pl.multiple_of(jnp.clip(qs - half, 0, seq_len - kw), half)
        shift = (qs - ks) // half
        for hp in range(n_hp):
            ls = slice(hp * LANES, (hp + 1) * LANES)
            q = q_ref[0, rows, ls]
            kwin = k_ref[0, pl.ds(ks, kw), ls]
            vwin = v_ref[0, pl.ds(ks, kw), ls]
            zero = jnp.zeros_like(q)
            q2 = jnp.concatenate([jnp.where(lo, q, zero), jnp.where(lo, zero, q)], axis=0)
            s = lax.dot_general(q2, kwin, (((1,), (1,)), ((), ())), preferred_element_type=F32)
            s = s + bias_ref[hp, shift]
            m = jnp.max(s, axis=-1, keepdims=True)
            p = jnp.exp(s - m)
            den = jnp.sum(p, axis=-1, keepdims=True)
            pv = jnp.dot(p.astype(BF16), vwin, preferred_element_type=F32) / den
            lse2 = m + jnp.log(den)
            o = jnp.where(lo, pv[0:tq], pv[tq:])
            lse = jnp.where(lo, lse2[0:tq], lse2[tq:])
            if has_prev:
                lse_p = pl_ref[hp, rows, :]
                top = jnp.maximum(lse_p, lse)
                wa = jnp.exp(lse_p - top)
                wb = jnp.exp(lse - top)
                den = wa + wb
                o = (wa * po_ref[hp, rows, :] + wb * o) / den
                lse = top + jnp.log(den)
            o_ref[0, rows, ls] = o.astype(o_ref.dtype)
            if not final:
                lse_ref[0, rows, ls] = lse
        return carry

    n_blk = q_rows // tq
    lax.fori_loop(0, n_blk, blk, 0, unroll=max(1, min(DIL_UNROLL // n_hp, n_blk)))


DIL_MAX_RESIDENT_ROWS = 2048


DIL_CHAIN_RATIO = 4
assert all(DIL_PATTERNS[n + 1][1] == DIL_CHAIN_RATIO * DIL_PATTERNS[n][1] for n in range(len(DIL_PATTERNS) - 1))


def _dilated_pattern(src, col0, batch, window, dil, prev, final):
    width = DIL_HEADS * DIL_DH
    seq_len = src.shape[1]
    group_blocks = src.shape[2] // dil // LANES
    half = window // (2 * dil)
    hpairs = DIL_HEADS // 2
    n_hp = hpairs if seq_len <= DIL_MAX_RESIDENT_ROWS else 1
    q_rows = min(seq_len, max(1024, DIL_QBLOCK * DIL_UNROLL // n_hp))
    lanes = n_hp * LANES
    hp_blocks = hpairs // n_hp
    c0 = col0 // LANES
    slopes = (jnp.exp2(-8.0 * jnp.arange(1, DIL_HEADS + 1, dtype=F32) / DIL_HEADS) * dil).reshape(hpairs, 2, 1, 1, 1)
    kw = DIL_QBLOCK + 2 * half
    shift = jnp.arange(3, dtype=jnp.int32)[:, None, None] * half
    dist = jnp.abs(jnp.arange(DIL_QBLOCK, dtype=jnp.int32)[None, :, None] + shift
                   - jnp.arange(kw, dtype=jnp.int32)[None, None, :])
    bias = jnp.where(dist <= half, -(slopes * dist.astype(F32)), NEG_INF).astype(F32)
    bias = bias.transpose(0, 2, 1, 3, 4).reshape(hpairs, 3, 2 * DIL_QBLOCK, kw)
    col = lambda part: (lambda i, r, hp, j: (i, 0, (r * group_blocks + c0 + part * hpairs) // n_hp + hp))
    qcol = lambda i, r, hp, j: (i, j, (r * group_blocks + c0) // n_hp + hp)
    oblk = pl.BlockSpec((1, q_rows, lanes), lambda i, r, hp, j: (i, j, r * hp_blocks + hp))
    in_specs = [pl.BlockSpec((1, q_rows, lanes), qcol),
                pl.BlockSpec((1, seq_len, lanes), col(1)),
                pl.BlockSpec((1, seq_len, lanes), col(2)),
                pl.BlockSpec((n_hp, 3, 2 * DIL_QBLOCK, kw), lambda i, r, hp, j: (hp, 0, 0, 0))]
    args = [src, src, src, bias]
    scratch = []
    if prev is not None:
        ratio = DIL_CHAIN_RATIO
        pblk = lambda m: pl.BlockSpec((1, q_rows // ratio, lanes),
                                      lambda i, r, hp, j: (i, j, (m * dil + r) * hp_blocks + hp))
        for a in prev:
            in_specs += [pblk(m) for m in range(ratio)]
            args += [a] * ratio
        scratch = [pltpu.VMEM((n_hp, q_rows, LANES), F32)] * 2
    oshape = (batch, seq_len, dil * width)
    if final:
        out_specs, out_shape = oblk, jax.ShapeDtypeStruct(oshape, BF16)
    else:
        out_specs, out_shape = [oblk, oblk], [jax.ShapeDtypeStruct(oshape, F32)] * 2
    return pl.pallas_call(
        functools.partial(_dilated_kernel, seq_len=seq_len, half=half, q_rows=q_rows, n_hp=n_hp,
                          has_prev=prev is not None, final=final),
        grid=(batch, dil, hp_blocks, seq_len // q_rows),
        in_specs=in_specs,
        out_specs=out_specs,
        out_shape=out_shape,
        scratch_shapes=scratch,
        compiler_params=_cparams(("parallel", "parallel", "parallel", "arbitrary")),
        name=f"l0_dilated_d{dil}",
    )(*args)


def _dilated_attention(p0, regrouped):
    b, s, _ = p0.shape
    by_dil = dict(zip(DIL_STRIDED, regrouped))
    prev = None
    for n, (window, dil) in enumerate(reversed(DIL_PATTERNS)):
        if dil == 1:
            src, col0 = p0, DIL_COL0
        else:
            src, col0 = by_dil[dil].reshape(b, s // dil, dil * DIL_COLS), 0
        prev = _dilated_pattern(src, col0, b, window, dil, prev, final=n == len(DIL_PATTERNS) - 1)
    return prev


def _l0_out_ffn_kernel(x_ref, ret_ref, dil_ref, wo_ref, g_ref, wg_ref, wu_ref, wd_ref, o_ref,
                       x1_ref, hn_ref, acc_ref):
    j = pl.program_id(1)

    @pl.when(j == 0)
    def _():
        half = wo_ref.shape[0] // 2
        x1 = (x_ref[...]
              + jnp.dot(ret_ref[...], wo_ref[0:half, :], preferred_element_type=F32)
              + jnp.dot(dil_ref[...], wo_ref[half:, :], preferred_element_type=F32))
        x1_ref[...] = x1
        hn_ref[...] = _rms(x1, g_ref[...]).astype(BF16)
        acc_ref[...] = jnp.zeros_like(acc_ref)

    h = hn_ref[...]
    a = jnp.dot(h, wg_ref[...], preferred_element_type=F32)
    u = jnp.dot(h, wu_ref[...], preferred_element_type=F32)
    act = (a * jax.nn.sigmoid(a) * u).astype(BF16)
    acc_ref[...] += jnp.dot(act, wd_ref[...], preferred_element_type=F32)

    @pl.when(j == pl.num_programs(1) - 1)
    def _():
        o_ref[...] = x1_ref[...] + acc_ref[...]


def _l0_out_ffn(x2d, ret2d, dil2d, w_out, ffn_norm, w_gate, w_up, w_down, tm=TM_L0_FFN):
    t, d = x2d.shape
    f = w_gate.shape[1]
    tf = f // 2 if (f // 2) % LANES == 0 else f
    half = w_out.shape[0] // 2
    return pl.pallas_call(
        _l0_out_ffn_kernel,
        grid=(t // tm, f // tf),
        in_specs=[
            pl.BlockSpec((tm, d), lambda i, j: (i, 0)),
            pl.BlockSpec((tm, half), lambda i, j: (i, 0)),
            pl.BlockSpec((tm, half), lambda i, j: (i, 0)),
            pl.BlockSpec((2 * half, d), lambda i, j: (0, 0)),
            pl.BlockSpec((1, d), lambda i, j: (0, 0)),
            pl.BlockSpec((d, tf), lambda i, j: (0, j)),
            pl.BlockSpec((d, tf), lambda i, j: (0, j)),
            pl.BlockSpec((tf, d), lambda i, j: (j, 0)),
        ],
        out_specs=pl.BlockSpec((tm, d), lambda i, j: (i, 0)),
        out_shape=jax.ShapeDtypeStruct((t, d), F32),
        scratch_shapes=[pltpu.VMEM((tm, d), F32), pltpu.VMEM((tm, d), BF16), pltpu.VMEM((tm, d), F32)],
        compiler_params=_cparams(("parallel", "arbitrary")),
        name="l0_out_proj_ffn",
    )(x2d, ret2d, dil2d, w_out, ffn_norm.reshape(1, d), w_gate, w_up, w_down)


L1_W1_COLS = MLA_Q_RANK + MLA_KV_RANK + 2 * LANES + S5_CH
HEAD_BLOCK = LANES


def _l1_proj_kernel(x_ref, an_ref, w1_ref, qn_ref, wq_ref, kvn_ref, wkv_ref, ct_ref, st_ref,
                    q_out, k_out, v_out, u_out):
    xn = _rms(x_ref[0], an_ref[...]).astype(BF16)
    proj = jnp.dot(xn, w1_ref[...], preferred_element_type=F32)
    c0 = MLA_Q_RANK
    c1 = c0 + MLA_KV_RANK
    cq = proj[:, 0:c0]
    ckv = proj[:, c0:c1]
    ka = proj[:, c1:c1 + LANES]
    kb = proj[:, c1 + LANES:c1 + 2 * LANES]
    u_out[...] = proj[:, c1 + 2 * LANES:]

    ct = ct_ref[...]
    st = st_ref[...]
    ct8 = jnp.concatenate([ct] * MLA_HEADS, axis=1)
    st8 = jnp.concatenate([st] * MLA_HEADS, axis=1)
    width = MLA_HEADS * HEAD_BLOCK

    q2 = jnp.dot(_rms(cq, qn_ref[...]).astype(BF16), wq_ref[...], preferred_element_type=F32)
    scale = (MLA_NOPE + MLA_ROPE) ** -0.5 * math.log2(math.e)
    q_out[0] = ((q2[:, 0:width] * ct8 + q2[:, width:] * st8) * scale).astype(q_out.dtype)

    kv = jnp.dot(_rms(ckv, kvn_ref[...]).astype(BF16), wkv_ref[...], preferred_element_type=F32)
    krot = ka * ct + kb * st
    k_out[0] = (kv[:, 0:width] + jnp.concatenate([krot] * MLA_HEADS, axis=1)).astype(k_out.dtype)
    lane = lax.broadcasted_iota(jnp.int32, (1, width), 1)
    ones = jnp.where(lane % HEAD_BLOCK >= MLA_V, 1.0, 0.0)
    v_out[0] = (kv[:, width:] + ones).astype(v_out.dtype)


def _l1_weights(w_in, w_uq, w_ukv):
    c0 = MLA_Q_RANK
    c1 = c0 + MLA_KV_RANK
    c2 = c1 + MLA_ROPE
    hr = MLA_ROPE // 2
    kpe = w_in[:, c1:c2]
    zeros = lambda n: jnp.zeros((w_in.shape[0], n), w_in.dtype)
    ka = jnp.concatenate([zeros(MLA_NOPE), kpe, zeros(LANES - MLA_NOPE - MLA_ROPE)], axis=1)
    kb = jnp.concatenate([zeros(MLA_NOPE), kpe[:, hr:], kpe[:, :hr], zeros(LANES - MLA_NOPE - MLA_ROPE)], axis=1)
    w1 = jnp.concatenate([w_in[:, :c1], ka, kb, w_in[:, c2:]], axis=1).astype(BF16)

    qd = MLA_NOPE + MLA_ROPE
    wq = w_uq.reshape(MLA_Q_RANK, MLA_HEADS, qd)
    zq = lambda n: jnp.zeros((MLA_Q_RANK, MLA_HEADS, n), w_uq.dtype)
    qa = jnp.concatenate([wq, zq(HEAD_BLOCK - qd)], axis=2)
    qb = jnp.concatenate([zq(MLA_NOPE), wq[:, :, MLA_NOPE + hr:], wq[:, :, MLA_NOPE:MLA_NOPE + hr],
                          zq(HEAD_BLOCK - qd)], axis=2)
    wq2 = jnp.concatenate([qa.reshape(MLA_Q_RANK, -1), qb.reshape(MLA_Q_RANK, -1)], axis=1).astype(BF16)

    wkv = w_ukv.reshape(MLA_KV_RANK, MLA_HEADS, MLA_NOPE + MLA_V)
    kpart = jnp.concatenate([wkv[:, :, :MLA_NOPE],
                             jnp.zeros((MLA_KV_RANK, MLA_HEADS, HEAD_BLOCK - MLA_NOPE), w_ukv.dtype)], axis=2)
    vpart = jnp.concatenate([wkv[:, :, MLA_NOPE:],
                             jnp.zeros((MLA_KV_RANK, MLA_HEADS, HEAD_BLOCK - MLA_V), w_ukv.dtype)], axis=2)
    wkv2 = jnp.concatenate([kpart.reshape(MLA_KV_RANK, -1), vpart.reshape(MLA_KV_RANK, -1)], axis=1).astype(BF16)
    return w1, wq2, wkv2


def _rope_lane_tables(s):
    hr = MLA_ROPE // 2
    inv = ROPE_BASE ** (-jnp.arange(0, MLA_ROPE, 2, dtype=F32) / MLA_ROPE)
    ang = jnp.arange(s, dtype=F32)[:, None] * inv[None, :]
    cos, sin = jnp.cos(ang), jnp.sin(ang)
    pad = jnp.zeros((s, HEAD_BLOCK - MLA_NOPE - MLA_ROPE), F32)
    ct = jnp.concatenate([jnp.ones((s, MLA_NOPE), F32), cos, cos, pad], axis=1)
    st = jnp.concatenate([jnp.zeros((s, MLA_NOPE), F32), -sin, sin, pad], axis=1)
    assert ct.shape[1] == HEAD_BLOCK and hr * 2 == MLA_ROPE
    return ct, st


def _l1_proj(x, attn_norm, w1, q_norm, wq2, kv_norm, wkv2, ct, st, nb_pad, tm=TM_L1_PROJ):
    b, s, d = x.shape
    width = MLA_HEADS * HEAD_BLOCK
    const = lambda shape: pl.BlockSpec(shape, lambda i, j: (0, 0))
    tok = lambda n: pl.BlockSpec((1, tm, n), lambda i, j: (i, j, 0))
    return pl.pallas_call(
        _l1_proj_kernel,
        grid=(b, s // tm),
        in_specs=[tok(d), const((1, d)), const(w1.shape), const((1, MLA_Q_RANK)), const(wq2.shape),
                  const((1, MLA_KV_RANK)), const(wkv2.shape),
                  pl.BlockSpec((tm, HEAD_BLOCK), lambda i, j: (j, 0)),
                  pl.BlockSpec((tm, HEAD_BLOCK), lambda i, j: (j, 0))],
        out_specs=[tok(width), tok(width), tok(width),
                   pl.BlockSpec((tm, S5_CH), lambda i, j: (j, i))],
        out_shape=[jax.ShapeDtypeStruct((b, s, width), BF16),
                   jax.ShapeDtypeStruct((b, s, width), BF16),
                   jax.ShapeDtypeStruct((b, s, width), BF16),
                   jax.ShapeDtypeStruct((s, nb_pad * S5_CH), F32)],
        compiler_params=_cparams(("parallel", "parallel")),
        name="l1_norm_in_proj",
    )(x, attn_norm.reshape(1, d), w1, q_norm.reshape(1, -1), wq2, kv_norm.reshape(1, -1), wkv2, ct, st)


def _mla_kernel(q_ref, k_ref, v_ref, o_ref, *, seq_len, tk, unroll):
    tq = q_ref.shape[1]
    qs = [q_ref[0, :, 0:HEAD_BLOCK], q_ref[0, :, HEAD_BLOCK:]]

    def body(j, carry):
        sl = pl.ds(pl.multiple_of(j * tk, tk), tk)
        new = []
        for hh in range(2):
            m, acc = carry[hh]
            hs = slice(hh * HEAD_BLOCK, (hh + 1) * HEAD_BLOCK)
            s = lax.dot_general(qs[hh], k_ref[0, sl, hs], (((1,), (1,)), ((), ())), preferred_element_type=F32)
            m_new = jnp.maximum(m, jnp.max(s, axis=-1, keepdims=True))
            p = jnp.exp2(s - m_new).astype(BF16)
            acc_new = jnp.exp2(m - m_new) * acc + jnp.dot(p, v_ref[0, sl, hs], preferred_element_type=F32)
            new.append((m_new, acc_new))
        return tuple(new)

    init = tuple((jnp.full((tq, 1), NEG_INF, F32), jnp.zeros((tq, HEAD_BLOCK), F32)) for _ in range(2))
    (_, a0), (_, a1) = lax.fori_loop(0, seq_len // tk, body, init, unroll=unroll)
    lane = lax.broadcasted_iota(jnp.int32, (tq, LANES), 1)
    o0 = a0 / pltpu.roll(a0, MLA_V, 1)
    o1 = a1 / pltpu.roll(a1, MLA_V, 1)
    o_ref[0] = jnp.where(lane < MLA_V, o0, pltpu.roll(o1, MLA_V, 1)).astype(o_ref.dtype)


def _mla_attention(q, k, v, tq=MLA_TQ, tk=MLA_TK, unroll=MLA_UNROLL):
    b, s, _ = q.shape
    tq, tk = min(tq, s), min(tk, s)
    unroll = min(unroll, s // tk)
    hpairs = MLA_HEADS // 2
    pair = 2 * HEAD_BLOCK
    return pl.pallas_call(
        functools.partial(_mla_kernel, seq_len=s, tk=tk, unroll=unroll),
        grid=(b, hpairs, s // tq),
        in_specs=[pl.BlockSpec((1, tq, pair), lambda i, hp, j: (i, j, hp)),
                  pl.BlockSpec((1, s, pair), lambda i, hp, j: (i, 0, hp)),
                  pl.BlockSpec((1, s, pair), lambda i, hp, j: (i, 0, hp))],
        out_specs=pl.BlockSpec((1, tq, LANES), lambda i, hp, j: (i, j, hp)),
        out_shape=jax.ShapeDtypeStruct((b, s, MLA_HEADS * MLA_V), BF16),
        compiler_params=_cparams(("parallel", "parallel", "arbitrary")),
        name="l1_latent_attention",
    )(q, k, v)


S5_BLOCKS = S5_CH // LANES
S5_BLOCK_STATES = S5_NSTATE // S5_BLOCKS
S5_UNROLL = 4


def _s5_kernel(u_ref, bb_ref, are_ref, aim_ref, cb_ref, y_ref, bu_ref, x_ref, *, reverse, paired):
    tc, nb, _ = u_ref.shape
    n = S5_NSTATE
    sb = S5_BLOCK_STATES

    @pl.when(pl.program_id(0) == 0)
    def _():
        x_ref[...] = jnp.zeros_like(x_ref)

    u2 = u_ref[...].reshape(tc * nb, S5_CH)
    if paired:
        row = lax.broadcasted_iota(jnp.int32, (tc * nb, LANES), 0)
        is_fwd = (row % nb) < nb // 2
    for k in range(S5_BLOCKS):
        uk = u2[:, k * LANES:(k + 1) * LANES]
        if paired:
            zero = jnp.zeros_like(uk)
            uk = jnp.concatenate([jnp.where(is_fwd, uk, zero), jnp.where(is_fwd, zero, uk)], axis=1)
        r = jnp.dot(uk.astype(BF16), bb_ref[k], preferred_element_type=F32)
        bu_ref[:, :, k * sb:(k + 1) * sb] = r[:, 0:sb].reshape(tc, nb, sb)
        bu_ref[:, :, n + k * sb:n + (k + 1) * sb] = r[:, sb:].reshape(tc, nb, sb)

    def step(i, carry):
        t = tc - 1 - i if reverse else i
        xr, xi = carry
        a_re = are_ref[...]
        a_im = aim_ref[...]
        nr = a_re * xr - a_im * xi + bu_ref[t, :, 0:n]
        ni = a_re * xi + a_im * xr + bu_ref[t, :, n:]
        bu_ref[t, :, 0:n] = nr
        bu_ref[t, :, n:] = ni
        return nr, ni

    xr, xi = lax.fori_loop(0, tc, step, (x_ref[:, 0:n], x_ref[:, n:]), unroll=S5_UNROLL)
    x_ref[:, 0:n] = xr
    x_ref[:, n:] = xi

    for k in range(S5_BLOCKS):
        xs = jnp.concatenate([bu_ref[:, :, k * sb:(k + 1) * sb], bu_ref[:, :, n + k * sb:n + (k + 1) * sb]], axis=2)
        yk = jnp.dot(xs.reshape(tc * nb, 2 * sb).astype(BF16), cb_ref[k], preferred_element_type=F32)
        if paired:
            yk = jnp.where(is_fwd, yk[:, 0:LANES], yk[:, LANES:])
        y_ref[:, :, k * LANES:(k + 1) * LANES] = yk.reshape(tc, nb, LANES)


def _s5_direction_params(lam_re, lam_im, log_step, b_re, b_im, c_re, c_im, nb):
    lam = lax.complex(jnp.minimum(lam_re.astype(F32), -1e-4), lam_im.astype(F32))
    step = jnp.exp(log_step.astype(F32))[:, None]
    lam_bar = jnp.exp(lam * step)
    bmat = lax.complex(b_re.astype(F32), b_im.astype(F32))
    b_bar = ((lam_bar - 1.0) / lam)[:, :, None] * bmat
    gpb = S5_GROUPS // S5_BLOCKS
    eye = jnp.eye(gpb, dtype=F32)

    def in_map(t):
        t = t.reshape(S5_BLOCKS, gpb, S5_STATE, S5_GROUP_CH)
        return jnp.einsum('kgpc,gh->kgchp', t, eye).reshape(S5_BLOCKS, LANES, S5_BLOCK_STATES)

    def out_map(t):
        t = t.reshape(S5_BLOCKS, gpb, S5_GROUP_CH, S5_STATE)
        return jnp.einsum('kgcp,gh->kgphc', t, eye).reshape(S5_BLOCKS, S5_BLOCK_STATES, LANES)

    bblk = jnp.concatenate([in_map(jnp.real(b_bar)), in_map(jnp.imag(b_bar))], axis=2).astype(BF16)
    cblk = jnp.concatenate([out_map(c_re.astype(F32)), -out_map(c_im.astype(F32))], axis=1).astype(BF16)
    a_re = jnp.broadcast_to(jnp.real(lam_bar).reshape(1, S5_NSTATE), (nb, S5_NSTATE)).astype(F32)
    a_im = jnp.broadcast_to(jnp.imag(lam_bar).reshape(1, S5_NSTATE), (nb, S5_NSTATE)).astype(F32)
    return bblk, a_re, a_im, cblk


def _s5_pair_params(fwd, bwd):
    half = fwd[1].shape[0] // 2
    return (jnp.concatenate([fwd[0], bwd[0]], axis=1),
            jnp.concatenate([fwd[1][:half], bwd[1][:half]], axis=0),
            jnp.concatenate([fwd[2][:half], bwd[2][:half]], axis=0),
            jnp.concatenate([fwd[3], bwd[3]], axis=2))


def _s5_scan(u_tm, params, reverse, paired=False, tc=S5_TC):
    s, nb, _ = u_tm.shape
    bdense, a_re, a_im, cdense = params
    nchunks = s // tc
    tmap = (lambda i: (nchunks - 1 - i, 0, 0)) if reverse else (lambda i: (i, 0, 0))
    const = lambda shape: pl.BlockSpec(shape, lambda i: (0,) * len(shape))
    return pl.pallas_call(
        functools.partial(_s5_kernel, reverse=reverse, paired=paired),
        grid=(nchunks,),
        in_specs=[pl.BlockSpec((tc, nb, S5_CH), tmap), const(bdense.shape), const(a_re.shape),
                  const(a_im.shape), const(cdense.shape)],
        out_specs=pl.BlockSpec((tc, nb, S5_CH), tmap),
        out_shape=jax.ShapeDtypeStruct((s, nb, S5_CH), F32),
        scratch_shapes=[pltpu.VMEM((tc, nb, 2 * S5_NSTATE), F32), pltpu.VMEM((nb, 2 * S5_NSTATE), F32)],
        compiler_params=_cparams(("arbitrary",)),
        name="l1_s5_scan_pair" if paired else ("l1_s5_scan_bwd" if reverse else "l1_s5_scan_fwd"),
    )(u_tm, bdense, a_re, a_im, cdense)


def _l1_out_router_kernel(x_ref, mla_ref, yf_ref, yb_ref, u_ref, dskip_ref, gw_ref, gb_ref, wo_ref,
                          fn_ref, rt_ref, x1_out, route_out):
    tm = x_ref.shape[1]
    u = u_ref[...]
    y = yf_ref[...] + yb_ref[...] + dskip_ref[...] * u
    z = jax.nn.gelu(y)
    gate = jax.nn.sigmoid(jnp.dot(z.astype(BF16), gw_ref[...], preferred_element_type=F32) + gb_ref[...])
    ssm = (z * gate).astype(BF16)
    half = wo_ref.shape[0] // 2
    x1 = (x_ref[0]
          + jnp.dot(mla_ref[0], wo_ref[0:half, :], preferred_element_type=F32)
          + jnp.dot(ssm, wo_ref[half:, :], preferred_element_type=F32))
    _store_token_slabs(x1_out.at[0], x1, tm)
    hn = _rms(x1, fn_ref[...])

    h_hi = hn.astype(BF16)
    h_lo = (hn - h_hi.astype(F32)).astype(BF16)
    logits = (jnp.dot(h_hi, rt_ref[0], preferred_element_type=F32)
              + jnp.dot(h_hi, rt_ref[1], preferred_element_type=F32)
              + jnp.dot(h_lo, rt_ref[0], preferred_element_type=F32))
    lane = lax.broadcasted_iota(jnp.int32, logits.shape, 1)
    lg = jnp.where(lane < N_EXPERTS, logits, -jnp.inf)
    m1 = jnp.max(lg, axis=-1, keepdims=True)
    i1 = jnp.min(jnp.where(lg == m1, lane, LANES), axis=-1, keepdims=True)
    lg2 = jnp.where(lane == i1, -jnp.inf, lg)
    m2 = jnp.max(lg2, axis=-1, keepdims=True)
    i2 = jnp.min(jnp.where(lg2 == m2, lane, LANES), axis=-1, keepdims=True)
    e2 = jnp.exp(m2 - m1)
    w1 = 1.0 / (1.0 + e2)
    w2 = e2 / (1.0 + e2)
    route_out[0] = jnp.where(lane == 0, i1.astype(F32),
                             jnp.where(lane == 1, i2.astype(F32),
                                       jnp.where(lane == 2, w1, jnp.where(lane == 3, w2, 0.0))))


def _l1_out_router(x, mla, yf, yb, u_tm, d_skip, glu_w, glu_b, w_out, ffn_norm, router_pad, tm=TM_ROUTER):
    b, s, d = x.shape
    tok = lambda n: pl.BlockSpec((1, tm, n), lambda i, j: (i, j, 0))
    tmaj = pl.BlockSpec((tm, S5_CH), lambda i, j: (j, i))
    const = lambda shape: pl.BlockSpec(shape, lambda i, j: (0,) * len(shape))
    return pl.pallas_call(
        _l1_out_router_kernel,
        grid=(b, s // tm),
        in_specs=[tok(d), tok(MLA_HEADS * MLA_V), tmaj, tmaj, tmaj, const((1, S5_CH)),
                  const(glu_w.shape), const((1, S5_CH)), const(w_out.shape), const((1, d)),
                  const(router_pad.shape)],
        out_specs=[pl.BlockSpec((1, tm * SLAB_ROWS, LANES), lambda i, j: (i, j, 0)), tok(LANES)],
        out_shape=[jax.ShapeDtypeStruct((b, s * SLAB_ROWS, LANES), F32),
                   jax.ShapeDtypeStruct((b, s, LANES), F32)],
        compiler_params=_cparams(("parallel", "parallel")),
        name="l1_out_proj_router",
    )(x, mla, yf, yb, u_tm, d_skip.reshape(1, -1), glu_w, glu_b.reshape(1, -1), w_out,
      ffn_norm.reshape(1, d), router_pad)


def _start_row_gather(idx_ref, src_hbm, dst, sem, n_rows):
    def issue(r, carry):
        src = pl.multiple_of(idx_ref[r] * SLAB_ROWS, SLAB_ROWS)
        pltpu.make_async_copy(src_hbm.at[pl.ds(src, SLAB_ROWS)],
                              dst.at[pl.ds(pl.multiple_of(r * SLAB_ROWS, SLAB_ROWS), SLAB_ROWS)], sem).start()
        return carry

    lax.fori_loop(0, n_rows, issue, 0, unroll=8)


def _wait_row_gather(src_hbm, dst, sem, n_rows):
    pltpu.make_async_copy(src_hbm.at[pl.ds(0, n_rows * SLAB_ROWS)], dst, sem).wait()


def _expert_ffn_kernel(te_ref, nt_ref, idx0_ref, idxn_ref, x_hbm, gw_ref, g_ref, wg_ref, wu_ref, wd_ref,
                       o_ref, xbuf, sem, hn_ref, acc_ref):
    i = pl.program_id(0)
    j = pl.program_id(1)
    last = pl.num_programs(1) - 1
    tm = hn_ref.shape[0]
    n_used = nt_ref[0]
    slot = i % 2

    @pl.when(jnp.logical_and(i == 0, j == 0))
    def _():
        _start_row_gather(idx0_ref, x_hbm, xbuf.at[0], sem.at[0], tm)

    @pl.when(jnp.logical_and(j == 0, i < n_used))
    def _():
        _wait_row_gather(x_hbm, xbuf.at[slot], sem.at[slot], tm)
        x = _load_token_slabs(xbuf.at[slot], tm)
        hn_ref[...] = _rms(x, g_ref[...]).astype(BF16)
        acc_ref[...] = jnp.zeros_like(acc_ref)

    @pl.when(jnp.logical_and(j == 0, i + 1 < n_used))
    def _():
        _start_row_gather(idxn_ref, x_hbm, xbuf.at[1 - slot], sem.at[1 - slot], tm)

    @pl.when(i < n_used)
    def _():
        h = hn_ref[...]
        a = jnp.dot(h, wg_ref[0], preferred_element_type=F32)
        u = jnp.dot(h, wu_ref[0], preferred_element_type=F32)
        act = (a * jax.nn.sigmoid(a) * u).astype(BF16)
        acc_ref[...] += jnp.dot(act, wd_ref[0], preferred_element_type=F32)

        @pl.when(j == last)
        def _():
            _store_token_slabs(o_ref, gw_ref[...] * acc_ref[...], tm)

    @pl.when(jnp.logical_and(i >= n_used, j == last))
    def _():
        o_ref[...] = jnp.zeros_like(o_ref)


def _expert_ffn(x_slabs, src_tok, gate_w, tile_expert, n_tiles_used, ffn_norm, wg, wu, wd, tm, tf=TF_EXPERT):
    a_pad = src_tok.shape[0]
    n_tiles = a_pad // tm
    d = D_MODEL
    f = wg.shape[2]
    grid_spec = pltpu.PrefetchScalarGridSpec(
        num_scalar_prefetch=2,
        grid=(n_tiles, f // tf),
        in_specs=[
            pl.BlockSpec((tm,), lambda i, j, te, nt: (0,), memory_space=pltpu.SMEM),
            pl.BlockSpec((tm,), lambda i, j, te, nt: (jnp.minimum(i + 1, n_tiles - 1),), memory_space=pltpu.SMEM),
            pl.BlockSpec(memory_space=pl.ANY),
            pl.BlockSpec((tm, 1), lambda i, j, te, nt: (i, 0)),
            pl.BlockSpec((1, d), lambda i, j, te, nt: (0, 0)),
            pl.BlockSpec((1, d, tf), lambda i, j, te, nt: (te[i], 0, j)),
            pl.BlockSpec((1, d, tf), lambda i, j, te, nt: (te[i], 0, j)),
            pl.BlockSpec((1, tf, d), lambda i, j, te, nt: (te[i], j, 0)),
        ],
        out_specs=pl.BlockSpec((tm * SLAB_ROWS, LANES), lambda i, j, te, nt: (i, 0)),
        scratch_shapes=[pltpu.VMEM((2, tm * SLAB_ROWS, LANES), F32), pltpu.SemaphoreType.DMA((2,)),
                        pltpu.VMEM((tm, d), BF16), pltpu.VMEM((tm, d), F32)],
    )
    return pl.pallas_call(
        _expert_ffn_kernel,
        grid_spec=grid_spec,
        out_shape=jax.ShapeDtypeStruct((a_pad * SLAB_ROWS, LANES), F32),
        compiler_params=_cparams(("arbitrary", "arbitrary")),
        name="l1_expert_ffn",
    )(tile_expert, n_tiles_used, src_tok, src_tok, x_slabs, gate_w, ffn_norm.reshape(1, d), wg, wu, wd)


def _combine_norm_kernel(ia0_ref, ib0_ref, ian_ref, ibn_ref, x_ref, ys_hbm, g_ref, o_ref, buf, sem):
    tm = o_ref.shape[0]
    i = pl.program_id(0)
    slot = i % 2

    @pl.when(i == 0)
    def _():
        _start_row_gather(ia0_ref, ys_hbm, buf.at[0, 0], sem.at[0, 0], tm)
        _start_row_gather(ib0_ref, ys_hbm, buf.at[0, 1], sem.at[0, 1], tm)

    @pl.when(i + 1 < pl.num_programs(0))
    def _():
        _start_row_gather(ian_ref, ys_hbm, buf.at[1 - slot, 0], sem.at[1 - slot, 0], tm)
        _start_row_gather(ibn_ref, ys_hbm, buf.at[1 - slot, 1], sem.at[1 - slot, 1], tm)

    _wait_row_gather(ys_hbm, buf.at[slot, 0], sem.at[slot, 0], tm)
    _wait_row_gather(ys_hbm, buf.at[slot, 1], sem.at[slot, 1], tm)
    y = _load_token_slabs(x_ref, tm) + (_load_token_slabs(buf.at[slot, 0], tm)
                                        + _load_token_slabs(buf.at[slot, 1], tm))
    o_ref[...] = _rms(y, g_ref[...])


def _combine_norm(x_slabs, ys_slabs, dest, final_norm, tm=TM_COMBINE):
    t = x_slabs.shape[0] // SLAB_ROWS
    nblk = t // tm
    slab = (tm * SLAB_ROWS, LANES)
    return pl.pallas_call(
        _combine_norm_kernel,
        grid=(nblk,),
        in_specs=[pl.BlockSpec((tm,), lambda i: (0,), memory_space=pltpu.SMEM),
                  pl.BlockSpec((tm,), lambda i: (nblk,), memory_space=pltpu.SMEM),
                  pl.BlockSpec((tm,), lambda i: (jnp.minimum(i + 1, nblk - 1),), memory_space=pltpu.SMEM),
                  pl.BlockSpec((tm,), lambda i: (jnp.minimum(i + 1, nblk - 1) + nblk,), memory_space=pltpu.SMEM),
                  pl.BlockSpec(slab, lambda i: (i, 0)),
                  pl.BlockSpec(memory_space=pl.ANY),
                  pl.BlockSpec((1, D_MODEL), lambda i: (0, 0))],
        out_specs=pl.BlockSpec((tm, D_MODEL), lambda i: (i, 0)),
        out_shape=jax.ShapeDtypeStruct((t, D_MODEL), F32),
        scratch_shapes=[pltpu.VMEM((2, 2) + slab, F32), pltpu.SemaphoreType.DMA((2, 2))],
        compiler_params=_cparams(("arbitrary",)),
        name="l1_combine_final_norm",
    )(dest, dest, dest, dest, x_slabs, ys_slabs, final_norm.reshape(1, D_MODEL))


def _moe(x_slabs, route, ffn_norm, wg, wu, wd, final_norm, tm=TM_EXPERT):
    t = x_slabs.shape[0] // SLAB_ROWS
    e_idx = jnp.concatenate([route[:, 0], route[:, 1]]).astype(jnp.int32)
    e_w = jnp.concatenate([route[:, 2], route[:, 3]])
    n_assign = TOP_K * t
    order = jnp.argsort(e_idx, stable=True).astype(jnp.int32)
    inv = jnp.argsort(order).astype(jnp.int32)
    counts = jnp.sum(e_idx[:, None] == jnp.arange(N_EXPERTS, dtype=jnp.int32)[None, :], axis=0).astype(jnp.int32)
    starts = jnp.cumsum(counts) - counts
    padded = ((counts + tm - 1) // tm) * tm
    pad_ends = jnp.cumsum(padded)
    pad_starts = pad_ends - padded
    a_pad = n_assign + N_EXPERTS * tm
    n_tiles = a_pad // tm

    tile_start = jnp.arange(n_tiles, dtype=jnp.int32) * tm
    tile_expert = jnp.minimum(jnp.sum(tile_start[:, None] >= pad_ends[None, :], axis=1), N_EXPERTS - 1).astype(jnp.int32)
    n_tiles_used = (pad_ends[-1] // tm).astype(jnp.int32).reshape(1)

    slot = jnp.arange(a_pad, dtype=jnp.int32)
    slot_e = jnp.repeat(tile_expert, tm)
    within = slot - pad_starts[slot_e]
    valid = within < counts[slot_e]
    src = order[jnp.clip(starts[slot_e] + within, 0, n_assign - 1)]
    src_tok = jnp.where(valid, src % t, 0).astype(jnp.int32)
    gate_w = jnp.where(valid, e_w[src], 0.0).astype(F32).reshape(a_pad, 1)

    ys = _expert_ffn(x_slabs, src_tok, gate_w, tile_expert, n_tiles_used, ffn_norm, wg, wu, wd, tm)
    dest = (pad_starts[e_idx] + inv - starts[e_idx]).astype(jnp.int32)
    return _combine_norm(x_slabs, ys, dest, final_norm)


def _l0_in_weight(w_in):
    hq = RET_HEADS * RET_DK
    hv = RET_HEADS * RET_DV
    hd = DIL_HEADS * DIL_DH
    rq, rk, rv, rg, dq, dk, dv = jnp.split(
        w_in, [hq, 2 * hq, 2 * hq + hv, 2 * hq + 2 * hv, 2 * hq + 2 * hv + hd, 2 * hq + 2 * hv + 2 * hd], axis=1)
    dup = lambda t: jnp.concatenate([t.reshape(-1, RET_HEADS, 1, RET_DK)] * 2, axis=2).reshape(-1, 2 * hq)
    w = jnp.concatenate([dup(rq) * RET_DK ** -0.5, dup(rk), rv, rg, dq * DIL_DH ** -0.5, dk, dv], axis=1)
    assert w.shape[1] == L0_COLS
    return w.astype(BF16)


def _prepare(l0_w_in, l0_ret_decay_f, l0_ret_decay_b, l0_w_out, l0_ffn_w_gate, l0_ffn_w_up, l0_ffn_w_down,
             l1_w_in, l1_mla_w_uq, l1_mla_w_ukv, l1_s5_glu_w, l1_w_out, l1_router,
             l1_exp_w_gate, l1_exp_w_up, l1_exp_w_down):
    w1, wq2, wkv2 = _l1_weights(l1_w_in, l1_mla_w_uq, l1_mla_w_ukv)
    router_f32 = jnp.zeros((D_MODEL, LANES), F32).at[:, :N_EXPERTS].set(l1_router.astype(F32))
    router_hi = router_f32.astype(BF16)
    router_pad = jnp.stack([router_hi, (router_f32 - router_hi.astype(F32)).astype(BF16)])
    return dict(
        l0_w_in=_l0_in_weight(l0_w_in),
        ret_tables=_retention_tables(l0_ret_decay_f, l0_ret_decay_b),
        l0_w_out=l0_w_out.astype(BF16),
        l0_wg=l0_ffn_w_gate.astype(BF16), l0_wu=l0_ffn_w_up.astype(BF16), l0_wd=l0_ffn_w_down.astype(BF16),
        l1_w1=w1, l1_wq2=wq2, l1_wkv2=wkv2,
        glu_w=l1_s5_glu_w.astype(BF16), l1_w_out=l1_w_out.astype(BF16), router_pad=router_pad,
        exp_wg=l1_exp_w_gate.astype(BF16), exp_wu=l1_exp_w_up.astype(BF16), exp_wd=l1_exp_w_down.astype(BF16),
    )


def _trunk(x, prep, p):
    b, s, d = x.shape
    t = b * s
    p0, *regrouped = _l0_in_proj(x.reshape(t, d), p['l0_attn_norm'], prep['l0_w_in'])
    p0 = p0.reshape(b, s, L0_COLS)
    ret = _retention(p0, prep['ret_tables'], p['l0_ret_gn'])
    dil = _dilated_attention(p0, regrouped)
    x = _l0_out_ffn(x.reshape(t, d), ret.reshape(t, -1), dil.reshape(t, -1), prep['l0_w_out'],
                    p['l0_ffn_norm'], prep['l0_wg'], prep['l0_wu'], prep['l0_wd']).reshape(b, s, d)
    nb = -(-b // SUBLANES) * SUBLANES
    ct, st = _rope_lane_tables(s)
    q, k, v, u_tm = _l1_proj(x, p['l1_attn_norm'], prep['l1_w1'], p['l1_mla_q_norm'], prep['l1_wq2'],
                             p['l1_mla_kv_norm'], prep['l1_wkv2'], ct, st, b)
    mla = _mla_attention(q, k, v)
    u3 = u_tm.reshape(s, b, S5_CH)
    sf = _s5_direction_params(p['l1_s5_lam_re_f'], p['l1_s5_lam_im_f'], p['l1_s5_log_step_f'],
                              p['l1_s5_b_re'], p['l1_s5_b_im'], p['l1_s5_c_re'], p['l1_s5_c_im'], nb)
    sb = _s5_direction_params(p['l1_s5_lam_re_b'], p['l1_s5_lam_im_b'], p['l1_s5_log_step_b'],
                              p['l1_s5_b_re'], p['l1_s5_b_im'], p['l1_s5_c_re'], p['l1_s5_c_im'], nb)
    if 2 * b == SUBLANES:
        y = _s5_scan(jnp.concatenate([u3, jnp.flip(u3, axis=0)], axis=1), _s5_pair_params(sf, sb),
                     reverse=False, paired=True)
        yf = y[:, :b].reshape(s, b * S5_CH)
        yb = jnp.flip(y[:, b:], axis=0).reshape(s, b * S5_CH)
    else:
        if nb != b:
            u3 = jnp.pad(u3, ((0, 0), (0, nb - b), (0, 0)))
        yf = _s5_scan(u3, sf, reverse=False).reshape(s, nb * S5_CH)
        yb = _s5_scan(u3, sb, reverse=True).reshape(s, nb * S5_CH)
    x1s, route = _l1_out_router(x, mla, yf, yb, u_tm, p['l1_s5_d'], prep['glu_w'], p['l1_s5_glu_b'],
                                prep['l1_w_out'], p['l1_ffn_norm'], prep['router_pad'])
    out = _moe(x1s.reshape(t * SLAB_ROWS, LANES), route.reshape(t, LANES), p['l1_ffn_norm'],
               prep['exp_wg'], prep['exp_wu'], prep['exp_wd'], p['final_norm'])
    return out.reshape(b, s, d)


def kernel(x_prompt, x_sample, l0_attn_norm, l0_w_in, l0_ret_decay_f, l0_ret_decay_b, l0_ret_gn, l0_w_out,
           l0_ffn_norm, l0_ffn_w_gate, l0_ffn_w_up, l0_ffn_w_down, l1_attn_norm, l1_w_in, l1_mla_q_norm,
           l1_mla_w_uq, l1_mla_kv_norm, l1_mla_w_ukv, l1_s5_lam_re_f, l1_s5_lam_im_f, l1_s5_log_step_f,
           l1_s5_lam_re_b, l1_s5_lam_im_b, l1_s5_log_step_b, l1_s5_b_re, l1_s5_b_im, l1_s5_c_re, l1_s5_c_im,
           l1_s5_d, l1_s5_glu_w, l1_s5_glu_b, l1_w_out, l1_ffn_norm, l1_router, l1_exp_w_gate, l1_exp_w_up,
           l1_exp_w_down, final_norm):
    p = dict(locals())
    prep = _prepare(l0_w_in, l0_ret_decay_f, l0_ret_decay_b, l0_w_out, l0_ffn_w_gate, l0_ffn_w_up,
                    l0_ffn_w_down, l1_w_in, l1_mla_w_uq, l1_mla_w_ukv, l1_s5_glu_w, l1_w_out, l1_router,
                    l1_exp_w_gate, l1_exp_w_up, l1_exp_w_down)
    return (_trunk(x_prompt, prep, p), _trunk(x_sample, prep, p))
```

```python
import functools
import math

import jax
import jax.numpy as jnp
from jax import lax
from jax.experimental import pallas as pl
from jax.experimental.pallas import tpu as pltpu

F32 = jnp.float32
BF16 = jnp.bfloat16

D_MODEL = 1024
EPS = 1e-6
NEG_INF = -1e30
RET_HEADS = 4
RET_DK = 64
RET_DV = 128
RET_CHUNK = 128
RET_UNROLL = 32
DIL_HEADS = 8
DIL_DH = 64
DIL_PATTERNS = ((128, 1), (512, 4), (2048, 16))
DIL_QBLOCK = 128
DIL_UNROLL = 32
MLA_HEADS = 8
MLA_Q_RANK = 256
MLA_KV_RANK = 128
MLA_NOPE = 64
MLA_ROPE = 32
MLA_V = 64
ROPE_BASE = 10000.0
S5_GROUPS = 32
S5_GROUP_CH = 16
S5_STATE = 64
S5_CH = S5_GROUPS * S5_GROUP_CH
S5_NSTATE = S5_GROUPS * S5_STATE
N_EXPERTS = 8
TOP_K = 2

LANES = 128
SUBLANES = 8
VMEM_LIMIT = 56 * 1024 * 1024

TM_L0_PROJ = 512
TM_L0_FFN = 512
TM_L1_PROJ = 512
TM_ROUTER = 256
MLA_TQ, MLA_TK, MLA_UNROLL = 1024, 2048, 2
S5_TC = 128
TM_EXPERT = 512
TF_EXPERT = 1792
TM_COMBINE = 256

L0_COLS = 3584
L0_BLOCKS = L0_COLS // LANES
L0_RQ, L0_RK, L0_RV, L0_RG, L0_DQ, L0_DK, L0_DV = 0, 4, 8, 12, 16, 20, 24


def _cparams(sem):
    return pltpu.CompilerParams(dimension_semantics=sem, vmem_limit_bytes=VMEM_LIMIT)


def _rms(x, g):
    return x * lax.rsqrt(jnp.mean(x * x, axis=-1, keepdims=True) + EPS) * g


SLAB_ROWS = D_MODEL // LANES


def _store_token_slabs(ref2d, x, n_tok):
    for s in range(SLAB_ROWS):
        ref2d[pl.ds(s, n_tok, stride=SLAB_ROWS), :] = x[:, s * LANES:(s + 1) * LANES]


def _load_token_slabs(ref2d, n_tok):
    return jnp.concatenate([ref2d[pl.ds(s, n_tok, stride=SLAB_ROWS), :] for s in range(SLAB_ROWS)], axis=1)


DIL_COLS = 3 * DIL_HEADS * DIL_DH
DIL_COL0 = L0_DQ * LANES
DIL_STRIDED = tuple(dil for _, dil in DIL_PATTERNS if dil > 1)


def _l0_in_proj_kernel(x_ref, g_ref, w_ref, o_ref, *rest):
    dil_refs, dsc = rest[:-1], rest[-1]
    tm = x_ref.shape[0]
    xn = _rms(x_ref[...], g_ref[...]).astype(BF16)
    res = jnp.dot(xn, w_ref[...], preferred_element_type=F32)
    o_ref[...] = res.astype(o_ref.dtype)
    for c in range(DIL_COLS // LANES):
        dsc[c] = res[:, DIL_COL0 + c * LANES:DIL_COL0 + (c + 1) * LANES]
    for ref, dil in zip(dil_refs, DIL_STRIDED):
        for r in range(dil):
            for c in range(DIL_COLS // LANES):
                lo = r * DIL_COLS + c * LANES
                ref[:, lo:lo + LANES] = dsc[c, pl.ds(r, tm // dil, stride=dil), :].astype(ref.dtype)


def _l0_in_proj(x2d, gain, w, tm=TM_L0_PROJ):
    t, d = x2d.shape
    n = w.shape[1]
    dil_specs = [pl.BlockSpec((tm // dil, dil * DIL_COLS), lambda i: (i, 0)) for dil in DIL_STRIDED]
    dil_shapes = [jax.ShapeDtypeStruct((t // dil, dil * DIL_COLS), BF16) for dil in DIL_STRIDED]
    return pl.pallas_call(
        _l0_in_proj_kernel,
        grid=(t // tm,),
        in_specs=[
            pl.BlockSpec((tm, d), lambda i: (i, 0)),
            pl.BlockSpec((1, d), lambda i: (0, 0)),
            pl.BlockSpec((d, n), lambda i: (0, 0)),
        ],
        out_specs=[pl.BlockSpec((tm, n), lambda i: (i, 0))] + dil_specs,
        out_shape=[jax.ShapeDtypeStruct((t, n), BF16)] + dil_shapes,
        scratch_shapes=[pltpu.VMEM((DIL_COLS // LANES, tm, LANES), F32)],
        compiler_params=_cparams(("parallel",)),
        name="l0_norm_in_proj",
    )(x2d, gain.reshape(1, d), w)


def _retention_kernel(q_ref, k_ref, v_ref, g_ref, d_ref, qw_ref, kw_ref, cd_ref, gn_ref,
                      o_ref, kv_ref, p_ref, *, n_chunks):
    c = RET_CHUNK
    kw = kw_ref[0]
    qw = qw_ref[0]
    dmat = d_ref[0]
    gn = gn_ref[...]

    def kv_body(n, carry):
        sl = pl.ds(pl.multiple_of(n * c, c), c)
        kc = (k_ref[0, sl, :].astype(F32) * kw).T.astype(BF16)
        kv_ref[n] = jnp.dot(kc, v_ref[0, sl, :], preferred_element_type=F32)
        return carry

    lax.fori_loop(0, n_chunks, kv_body, 0, unroll=RET_UNROLL)

    half = c // 2
    dec_f = cd_ref[0, 0:half, :]
    dec_b = cd_ref[0, half:c, :]

    def fwd_body(n, s):
        p_ref[n, 0:half, :] = s.astype(BF16)
        return s * dec_f + kv_ref[n, 0:half, :]

    lax.fori_loop(0, n_chunks, fwd_body, jnp.zeros((half, RET_DV), F32))

    def bwd_body(i, s):
        n = n_chunks - 1 - i
        p_ref[n, half:c, :] = s.astype(BF16)
        return s * dec_b + kv_ref[n, half:c, :]

    lax.fori_loop(0, n_chunks, bwd_body, jnp.zeros((half, RET_DV), F32))

    def out_body(n, carry):
        sl = pl.ds(pl.multiple_of(n * c, c), c)
        qc = q_ref[0, sl, :]
        s = lax.dot_general(qc, k_ref[0, sl, :], (((1,), (1,)), ((), ())), preferred_element_type=F32)
        intra = jnp.dot((s * dmat).astype(BF16), v_ref[0, sl, :], preferred_element_type=F32)
        qq = (qc.astype(F32) * qw).astype(BF16)
        y = intra + jnp.dot(qq, p_ref[n], preferred_element_type=F32)
        mu = jnp.mean(y, axis=-1, keepdims=True)
        yc = y - mu
        var = jnp.mean(yc * yc, axis=-1, keepdims=True)
        yn = yc * lax.rsqrt(var + EPS) * gn
        gg = g_ref[0, sl, :].astype(F32)
        o_ref[0, sl, :] = (gg * jax.nn.sigmoid(gg) * yn).astype(o_ref.dtype)
        return carry

    lax.fori_loop(0, n_chunks, out_body, 0, unroll=RET_UNROLL)


def _retention_tables(decay_f, decay_b):
    c = RET_CHUNK
    lg_f = jax.nn.log_sigmoid(decay_f.astype(F32))[:, None, None]
    lg_b = jax.nn.log_sigmoid(decay_b.astype(F32))[:, None, None]
    j = jnp.arange(c, dtype=F32)
    diff = j[:, None] - j[None, :]
    dmat = 0.5 * jnp.where(diff >= 0, jnp.exp(lg_f * jnp.maximum(diff, 0.0)),
                           jnp.exp(lg_b * jnp.maximum(-diff, 0.0)))
    lane_f = (jnp.arange(LANES) < RET_DK)[None, None, :]
    jj = j[None, :, None]
    qw = jnp.where(lane_f, jnp.exp(lg_f * (jj + 1.0)), jnp.exp(lg_b * (c - jj)))
    kw = jnp.where(lane_f, jnp.exp(lg_f * (c - 1.0 - jj)), jnp.exp(lg_b * jj))
    row_f = (jnp.arange(c) < c // 2)[None, :, None]
    cd = jnp.where(row_f, jnp.exp(lg_f * c), jnp.exp(lg_b * c)) * jnp.ones((1, 1, RET_DV), F32)
    return dmat.astype(F32), qw.astype(F32), kw.astype(F32), cd.astype(F32)


def _retention(p0, tables, gn):
    b, s, _ = p0.shape
    dmat, qw, kw, cd = tables
    n_chunks = s // RET_CHUNK
    seq = lambda col: pl.BlockSpec((1, s, LANES), lambda i, h: (i, 0, col + h))
    tab = pl.BlockSpec((1, RET_CHUNK, LANES), lambda i, h: (h, 0, 0))
    return pl.pallas_call(
        functools.partial(_retention_kernel, n_chunks=n_chunks),
        grid=(b, RET_HEADS),
        in_specs=[seq(L0_RQ), seq(L0_RK), seq(L0_RV), seq(L0_RG), tab, tab, tab, tab,
                  pl.BlockSpec((1, LANES), lambda i, h: (0, h))],
        out_specs=pl.BlockSpec((1, s, LANES), lambda i, h: (i, 0, h)),
        out_shape=jax.ShapeDtypeStruct((b, s, RET_HEADS * RET_DV), BF16),
        scratch_shapes=[pltpu.VMEM((n_chunks, RET_CHUNK, RET_DV), F32),
                        pltpu.VMEM((n_chunks, RET_CHUNK, RET_DV), BF16)],
        compiler_params=_cparams(("parallel", "parallel")),
        name="l0_retention",
    )(p0, p0, p0, p0, dmat, qw, kw, cd, gn.reshape(1, -1))


def _dilated_kernel(*refs, seq_len, half, q_rows, n_hp, has_prev, final):
    q_ref, k_ref, v_ref, bias_ref = refs[:4]
    refs = refs[4:]
    ratio = DIL_CHAIN_RATIO
    if has_prev:
        prev_o, prev_l, refs = refs[:ratio], refs[ratio:2 * ratio], refs[2 * ratio:]
        po_ref, pl_ref = refs[-2:]
        refs = refs[:-2]
        for m in range(ratio):
            for hp in range(n_hp):
                ls = slice(hp * LANES, (hp + 1) * LANES)
                po_ref[hp, pl.ds(m, q_rows // ratio, stride=ratio), :] = prev_o[m][0, :, ls]
                pl_ref[hp, pl.ds(m, q_rows // ratio, stride=ratio), :] = prev_l[m][0, :, ls]
    o_ref = refs[0]
    lse_ref = None if final else refs[1]
    tq = DIL_QBLOCK
    kw = tq + 2 * half
    base = pl.program_id(3) * q_rows
    lane = lax.broadcasted_iota(jnp.int32, (tq, LANES), 1)
    lo = lane < DIL_DH

    def blk(i, carry):
        off = pl.multiple_of(i * tq, tq)
        rows = pl.ds(off, tq)
        qs = base + off
        ks = pl.multiple_of(jnp.clip(qs - half, 0, seq_len - kw), half)
        shift = (qs - ks) // half
        for hp in range(n_hp):
            ls = slice(hp * LANES, (hp + 1) * LANES)
            q = q_ref[0, rows, ls]
            kwin = k_ref[0, pl.ds(ks, kw), ls]
            vwin = v_ref[0, pl.ds(ks, kw), ls]
            zero = jnp.zeros_like(q)
            q2 = jnp.concatenate([jnp.where(lo, q, zero), jnp.where(lo, zero, q)], axis=0)
            s = lax.dot_general(q2, kwin, (((1,), (1,)), ((), ())), preferred_element_type=F32)
            s = s + bias_ref[hp, shift]
            m = jnp.max(s, axis=-1, keepdims=True)
            p = jnp.exp(s - m)
            den = jnp.sum(p, axis=-1, keepdims=True)
            pv = jnp.dot(p.astype(BF16), vwin, preferred_element_type=F32) / den
            lse2 = m + jnp.log(den)
            o = jnp.where(lo, pv[0:tq], pv[tq:])
            lse = jnp.where(lo, lse2[0:tq], lse2[tq:])
            if has_prev:
                lse_p = pl_ref[hp, rows, :]
                top = jnp.maximum(lse_p, lse)
                wa = jnp.exp(lse_p - top)
                wb = jnp.exp(lse - top)
                den = wa + wb
                o = (wa * po_ref[hp, rows, :] + wb * o) / den
                lse = top + jnp.log(den)
            o_ref[0, rows, ls] = o.astype(o_ref.dtype)
            if not final:
                lse_ref[0, rows, ls] = lse
        return carry

    n_blk = q_rows // tq
    lax.fori_loop(0, n_blk, blk, 0, unroll=max(1, min(DIL_UNROLL // n_hp, n_blk)))


DIL_MAX_RESIDENT_ROWS = 2048


DIL_CHAIN_RATIO = 4
assert all(DIL_PATTERNS[n + 1][1] == DIL_CHAIN_RATIO * DIL_PATTERNS[n][1] for n in range(len(DIL_PATTERNS) - 1))


def _dilated_pattern(src, col0, batch, window, dil, prev, final):
    width = DIL_HEADS * DIL_DH
    seq_len = src.shape[1]
    group_blocks = src.shape[2] // dil // LANES
    half = window // (2 * dil)
    hpairs = DIL_HEADS // 2
    n_hp = hpairs if seq_len <= DIL_MAX_RESIDENT_ROWS else 1
    q_rows = min(seq_len, max(1024, DIL_QBLOCK * DIL_UNROLL // n_hp))
    lanes = n_hp * LANES
    hp_blocks = hpairs // n_hp
    c0 = col0 // LANES
    slopes = (jnp.exp2(-8.0 * jnp.arange(1, DIL_HEADS + 1, dtype=F32) / DIL_HEADS) * dil).reshape(hpairs, 2, 1, 1, 1)
    kw = DIL_QBLOCK + 2 * half
    shift = jnp.arange(3, dtype=jnp.int32)[:, None, None] * half
    dist = jnp.abs(jnp.arange(DIL_QBLOCK, dtype=jnp.int32)[None, :, None] + shift
                   - jnp.arange(kw, dtype=jnp.int32)[None, None, :])
    bias = jnp.where(dist <= half, -(slopes * dist.astype(F32)), NEG_INF).astype(F32)
    bias = bias.transpose(0, 2, 1, 3, 4).reshape(hpairs, 3, 2 * DIL_QBLOCK, kw)
    col = lambda part: (lambda i, r, hp, j: (i, 0, (r * group_blocks + c0 + part * hpairs) // n_hp + hp))
    qcol = lambda i, r, hp, j: (i, j, (r * group_blocks + c0) // n_hp + hp)
    oblk = pl.BlockSpec((1, q_rows, lanes), lambda i, r, hp, j: (i, j, r * hp_blocks + hp))
    in_specs = [pl.BlockSpec((1, q_rows, lanes), qcol),
                pl.BlockSpec((1, seq_len, lanes), col(1)),
                pl.BlockSpec((1, seq_len, lanes), col(2)),
                pl.BlockSpec((n_hp, 3, 2 * DIL_QBLOCK, kw), lambda i, r, hp, j: (hp, 0, 0, 0))]
    args = [src, src, src, bias]
    scratch = []
    if prev is not None:
        ratio = DIL_CHAIN_RATIO
        pblk = lambda m: pl.BlockSpec((1, q_rows // ratio, lanes),
                                      lambda i, r, hp, j: (i, j, (m * dil + r) * hp_blocks + hp))
        for a in prev:
            in_specs += [pblk(m) for m in range(ratio)]
            args += [a] * ratio
        scratch = [pltpu.VMEM((n_hp, q_rows, LANES), F32)] * 2
    oshape = (batch, seq_len, dil * width)
    if final:
        out_specs, out_shape = oblk, jax.ShapeDtypeStruct(oshape, BF16)
    else:
        out_specs, out_shape = [oblk, oblk], [jax.ShapeDtypeStruct(oshape, F32)] * 2
    return pl.pallas_call(
        functools.partial(_dilated_kernel, seq_len=seq_len, half=half, q_rows=q_rows, n_hp=n_hp,
                          has_prev=prev is not None, final=final),
        grid=(batch, dil, hp_blocks, seq_len // q_rows),
        in_specs=in_specs,
        out_specs=out_specs,
        out_shape=out_shape,
        scratch_shapes=scratch,
        compiler_params=_cparams(("parallel", "parallel", "parallel", "arbitrary")),
        name=f"l0_dilated_d{dil}",
    )(*args)


def _dilated_attention(p0, regrouped):
    b, s, _ = p0.shape
    by_dil = dict(zip(DIL_STRIDED, regrouped))
    prev = None
    for n, (window, dil) in enumerate(reversed(DIL_PATTERNS)):
        if dil == 1:
            src, col0 = p0, DIL_COL0
        else:
            src, col0 = by_dil[dil].reshape(b, s // dil, dil * DIL_COLS), 0
        prev = _dilated_pattern(src, col0, b, window, dil, prev, final=n == len(DIL_PATTERNS) - 1)
    return prev


def _l0_out_ffn_kernel(x_ref, ret_ref, dil_ref, wo_ref, g_ref, wg_ref, wu_ref, wd_ref, o_ref,
                       x1_ref, hn_ref, acc_ref):
    j = pl.program_id(1)

    @pl.when(j == 0)
    def _():
        half = wo_ref.shape[0] // 2
        x1 = (x_ref[...]
              + jnp.dot(ret_ref[...], wo_ref[0:half, :], preferred_element_type=F32)
              + jnp.dot(dil_ref[...], wo_ref[half:, :], preferred_element_type=F32))
        x1_ref[...] = x1
        hn_ref[...] = _rms(x1, g_ref[...]).astype(BF16)
        acc_ref[...] = jnp.zeros_like(acc_ref)

    h = hn_ref[...]
    a = jnp.dot(h, wg_ref[...], preferred_element_type=F32)
    u = jnp.dot(h, wu_ref[...], preferred_element_type=F32)
    act = (a * jax.nn.sigmoid(a) * u).astype(BF16)
    acc_ref[...] += jnp.dot(act, wd_ref[...], preferred_element_type=F32)

    @pl.when(j == pl.num_programs(1) - 1)
    def _():
        o_ref[...] = x1_ref[...] + acc_ref[...]


def _l0_out_ffn(x2d, ret2d, dil2d, w_out, ffn_norm, w_gate, w_up, w_down, tm=TM_L0_FFN):
    t, d = x2d.shape
    f = w_gate.shape[1]
    tf = f // 2 if (f // 2) % LANES == 0 else f
    half = w_out.shape[0] // 2
    return pl.pallas_call(
        _l0_out_ffn_kernel,
        grid=(t // tm, f // tf),
        in_specs=[
            pl.BlockSpec((tm, d), lambda i, j: (i, 0)),
            pl.BlockSpec((tm, half), lambda i, j: (i, 0)),
            pl.BlockSpec((tm, half), lambda i, j: (i, 0)),
            pl.BlockSpec((2 * half, d), lambda i, j: (0, 0)),
            pl.BlockSpec((1, d), lambda i, j: (0, 0)),
            pl.BlockSpec((d, tf), lambda i, j: (0, j)),
            pl.BlockSpec((d, tf), lambda i, j: (0, j)),
            pl.BlockSpec((tf, d), lambda i, j: (j, 0)),
        ],
        out_specs=pl.BlockSpec((tm, d), lambda i, j: (i, 0)),
        out_shape=jax.ShapeDtypeStruct((t, d), F32),
        scratch_shapes=[pltpu.VMEM((tm, d), F32), pltpu.VMEM((tm, d), BF16), pltpu.VMEM((tm, d), F32)],
        compiler_params=_cparams(("parallel", "arbitrary")),
        name="l0_out_proj_ffn",
    )(x2d, ret2d, dil2d, w_out, ffn_norm.reshape(1, d), w_gate, w_up, w_down)


L1_W1_COLS = MLA_Q_RANK + MLA_KV_RANK + 2 * LANES + S5_CH
HEAD_BLOCK = LANES


def _l1_proj_kernel(x_ref, an_ref, w1_ref, qn_ref, wq_ref, kvn_ref, wkv_ref, ct_ref, st_ref,
                    q_out, k_out, v_out, u_out):
    xn = _rms(x_ref[0], an_ref[...]).astype(BF16)
    proj = jnp.dot(xn, w1_ref[...], preferred_element_type=F32)
    c0 = MLA_Q_RANK
    c1 = c0 + MLA_KV_RANK
    cq = proj[:, 0:c0]
    ckv = proj[:, c0:c1]
    ka = proj[:, c1:c1 + LANES]
    kb = proj[:, c1 + LANES:c1 + 2 * LANES]
    u_out[...] = proj[:, c1 + 2 * LANES:]

    ct = ct_ref[...]
    st = st_ref[...]
    ct8 = jnp.concatenate([ct] * MLA_HEADS, axis=1)
    st8 = jnp.concatenate([st] * MLA_HEADS, axis=1)
    width = MLA_HEADS * HEAD_BLOCK

    q2 = jnp.dot(_rms(cq, qn_ref[...]).astype(BF16), wq_ref[...], preferred_element_type=F32)
    scale = (MLA_NOPE + MLA_ROPE) ** -0.5 * math.log2(math.e)
    q_out[0] = ((q2[:, 0:width] * ct8 + q2[:, width:] * st8) * scale).astype(q_out.dtype)

    kv = jnp.dot(_rms(ckv, kvn_ref[...]).astype(BF16), wkv_ref[...], preferred_element_type=F32)
    krot = ka * ct + kb * st
    k_out[0] = (kv[:, 0:width] + jnp.concatenate([krot] * MLA_HEADS, axis=1)).astype(k_out.dtype)
    lane = lax.broadcasted_iota(jnp.int32, (1, width), 1)
    ones = jnp.where(lane % HEAD_BLOCK >= MLA_V, 1.0, 0.0)
    v_out[0] = (kv[:, width:] + ones).astype(v_out.dtype)


def _l1_weights(w_in, w_uq, w_ukv):
    c0 = MLA_Q_RANK
    c1 = c0 + MLA_KV_RANK
    c2 = c1 + MLA_ROPE
    hr = MLA_ROPE // 2
    kpe = w_in[:, c1:c2]
    zeros = lambda n: jnp.zeros((w_in.shape[0], n), w_in.dtype)
    ka = jnp.concatenate([zeros(MLA_NOPE), kpe, zeros(LANES - MLA_NOPE - MLA_ROPE)], axis=1)
    kb = jnp.concatenate([zeros(MLA_NOPE), kpe[:, hr:], kpe[:, :hr], zeros(LANES - MLA_NOPE - MLA_ROPE)], axis=1)
    w1 = jnp.concatenate([w_in[:, :c1], ka, kb, w_in[:, c2:]], axis=1).astype(BF16)

    qd = MLA_NOPE + MLA_ROPE
    wq = w_uq.reshape(MLA_Q_RANK, MLA_HEADS, qd)
    zq = lambda n: jnp.zeros((MLA_Q_RANK, MLA_HEADS, n), w_uq.dtype)
    qa = jnp.concatenate([wq, zq(HEAD_BLOCK - qd)], axis=2)
    qb = jnp.concatenate([zq(MLA_NOPE), wq[:, :, MLA_NOPE + hr:], wq[:, :, MLA_NOPE:MLA_NOPE + hr],
                          zq(HEAD_BLOCK - qd)], axis=2)
    wq2 = jnp.concatenate([qa.reshape(MLA_Q_RANK, -1), qb.reshape(MLA_Q_RANK, -1)], axis=1).astype(BF16)

    wkv = w_ukv.reshape(MLA_KV_RANK, MLA_HEADS, MLA_NOPE + MLA_V)
    kpart = jnp.concatenate([wkv[:, :, :MLA_NOPE],
                             jnp.zeros((MLA_KV_RANK, MLA_HEADS, HEAD_BLOCK - MLA_NOPE), w_ukv.dtype)], axis=2)
    vpart = jnp.concatenate([wkv[:, :, MLA_NOPE:],
                             jnp.zeros((MLA_KV_RANK, MLA_HEADS, HEAD_BLOCK - MLA_V), w_ukv.dtype)], axis=2)
    wkv2 = jnp.concatenate([kpart.reshape(MLA_KV_RANK, -1), vpart.reshape(MLA_KV_RANK, -1)], axis=1).astype(BF16)
    return w1, wq2, wkv2


def _rope_lane_tables(s):
    hr = MLA_ROPE // 2
    inv = ROPE_BASE ** (-jnp.arange(0, MLA_ROPE, 2, dtype=F32) / MLA_ROPE)
    ang = jnp.arange(s, dtype=F32)[:, None] * inv[None, :]
    cos, sin = jnp.cos(ang), jnp.sin(ang)
    pad = jnp.zeros((s, HEAD_BLOCK - MLA_NOPE - MLA_ROPE), F32)
    ct = jnp.concatenate([jnp.ones((s, MLA_NOPE), F32), cos, cos, pad], axis=1)
    st = jnp.concatenate([jnp.zeros((s, MLA_NOPE), F32), -sin, sin, pad], axis=1)
    assert ct.shape[1] == HEAD_BLOCK and hr * 2 == MLA_ROPE
    return ct, st


def _l1_proj(x, attn_norm, w1, q_norm, wq2, kv_norm, wkv2, ct, st, nb_pad, tm=TM_L1_PROJ):
    b, s, d = x.shape
    width = MLA_HEADS * HEAD_BLOCK
    const = lambda shape: pl.BlockSpec(shape, lambda i, j: (0, 0))
    tok = lambda n: pl.BlockSpec((1, tm, n), lambda i, j: (i, j, 0))
    return pl.pallas_call(
        _l1_proj_kernel,
        grid=(b, s // tm),
        in_specs=[tok(d), const((1, d)), const(w1.shape), const((1, MLA_Q_RANK)), const(wq2.shape),
                  const((1, MLA_KV_RANK)), const(wkv2.shape),
                  pl.BlockSpec((tm, HEAD_BLOCK), lambda i, j: (j, 0)),
                  pl.BlockSpec((tm, HEAD_BLOCK), lambda i, j: (j, 0))],
        out_specs=[tok(width), tok(width), tok(width),
                   pl.BlockSpec((tm, S5_CH), lambda i, j: (j, i))],
        out_shape=[jax.ShapeDtypeStruct((b, s, width), BF16),
                   jax.ShapeDtypeStruct((b, s, width), BF16),
                   jax.ShapeDtypeStruct((b, s, width), BF16),
                   jax.ShapeDtypeStruct((s, nb_pad * S5_CH), F32)],
        compiler_params=_cparams(("parallel", "parallel")),
        name="l1_norm_in_proj",
    )(x, attn_norm.reshape(1, d), w1, q_norm.reshape(1, -1), wq2, kv_norm.reshape(1, -1), wkv2, ct, st)


def _mla_kernel(q_ref, k_ref, v_ref, o_ref, *, seq_len, tk, unroll):
    tq = q_ref.shape[1]
    qs = [q_ref[0, :, 0:HEAD_BLOCK], q_ref[0, :, HEAD_BLOCK:]]

    def body(j, carry):
        sl = pl.ds(pl.multiple_of(j * tk, tk), tk)
        new = []
        for hh in range(2):
            m, acc = carry[hh]
            hs = slice(hh * HEAD_BLOCK, (hh + 1) * HEAD_BLOCK)
            s = lax.dot_general(qs[hh], k_ref[0, sl, hs], (((1,), (1,)), ((), ())), preferred_element_type=F32)
            m_new = jnp.maximum(m, jnp.max(s, axis=-1, keepdims=True))
            p = jnp.exp2(s - m_new).astype(BF16)
            acc_new = jnp.exp2(m - m_new) * acc + jnp.dot(p, v_ref[0, sl, hs], preferred_element_type=F32)
            new.append((m_new, acc_new))
        return tuple(new)

    init = tuple((jnp.full((tq, 1), NEG_INF, F32), jnp.zeros((tq, HEAD_BLOCK), F32)) for _ in range(2))
    (_, a0), (_, a1) = lax.fori_loop(0, seq_len // tk, body, init, unroll=unroll)
    lane = lax.broadcasted_iota(jnp.int32, (tq, LANES), 1)
    o0 = a0 / pltpu.roll(a0, MLA_V, 1)
    o1 = a1 / pltpu.roll(a1, MLA_V, 1)
    o_ref[0] = jnp.where(lane < MLA_V, o0, pltpu.roll(o1, MLA_V, 1)).astype(o_ref.dtype)


def _mla_attention(q, k, v, tq=MLA_TQ, tk=MLA_TK, unroll=MLA_UNROLL):
    b, s, _ = q.shape
    tq, tk = min(tq, s), min(tk, s)
    unroll = min(unroll, s // tk)
    hpairs = MLA_HEADS // 2
    pair = 2 * HEAD_BLOCK
    return pl.pallas_call(
        functools.partial(_mla_kernel, seq_len=s, tk=tk, unroll=unroll),
        grid=(b, hpairs, s // tq),
        in_specs=[pl.BlockSpec((1, tq, pair), lambda i, hp, j: (i, j, hp)),
                  pl.BlockSpec((1, s, pair), lambda i, hp, j: (i, 0, hp)),
                  pl.BlockSpec((1, s, pair), lambda i, hp, j: (i, 0, hp))],
        out_specs=pl.BlockSpec((1, tq, LANES), lambda i, hp, j: (i, j, hp)),
        out_shape=jax.ShapeDtypeStruct((b, s, MLA_HEADS * MLA_V), BF16),
        compiler_params=_cparams(("parallel", "parallel", "arbitrary")),
        name="l1_latent_attention",
    )(q, k, v)


S5_BLOCKS = S5_CH // LANES
S5_BLOCK_STATES = S5_NSTATE // S5_BLOCKS
S5_UNROLL = 4


def _s5_kernel(u_ref, bb_ref, are_ref, aim_ref, cb_ref, y_ref, bu_ref, x_ref, *, reverse, paired):
    tc, nb, _ = u_ref.shape
    n = S5_NSTATE
    sb = S5_BLOCK_STATES

    @pl.when(pl.program_id(0) == 0)
    def _():
        x_ref[...] = jnp.zeros_like(x_ref)

    u2 = u_ref[...].reshape(tc * nb, S5_CH)
    if paired:
        row = lax.broadcasted_iota(jnp.int32, (tc * nb, LANES), 0)
        is_fwd = (row % nb) < nb // 2
    for k in range(S5_BLOCKS):
        uk = u2[:, k * LANES:(k + 1) * LANES]
        if paired:
            zero = jnp.zeros_like(uk)
            uk = jnp.concatenate([jnp.where(is_fwd, uk, zero), jnp.where(is_fwd, zero, uk)], axis=1)
        r = jnp.dot(uk.astype(BF16), bb_ref[k], preferred_element_type=F32)
        bu_ref[:, :, k * sb:(k + 1) * sb] = r[:, 0:sb].reshape(tc, nb, sb)
        bu_ref[:, :, n + k * sb:n + (k + 1) * sb] = r[:, sb:].reshape(tc, nb, sb)

    def step(i, carry):
        t = tc - 1 - i if reverse else i
        xr, xi = carry
        a_re = are_ref[...]
        a_im = aim_ref[...]
        nr = a_re * xr - a_im * xi + bu_ref[t, :, 0:n]
        ni = a_re * xi + a_im * xr + bu_ref[t, :, n:]
        bu_ref[t, :, 0:n] = nr
        bu_ref[t, :, n:] = ni
        return nr, ni

    xr, xi = lax.fori_loop(0, tc, step, (x_ref[:, 0:n], x_ref[:, n:]), unroll=S5_UNROLL)
    x_ref[:, 0:n] = xr
    x_ref[:, n:] = xi

    for k in range(S5_BLOCKS):
        xs = jnp.concatenate([bu_ref[:, :, k * sb:(k + 1) * sb], bu_ref[:, :, n + k * sb:n + (k + 1) * sb]], axis=2)
        yk = jnp.dot(xs.reshape(tc * nb, 2 * sb).astype(BF16), cb_ref[k], preferred_element_type=F32)
        if paired:
            yk = jnp.where(is_fwd, yk[:, 0:LANES], yk[:, LANES:])
        y_ref[:, :, k * LANES:(k + 1) * LANES] = yk.reshape(tc, nb, LANES)


def _s5_direction_params(lam_re, lam_im, log_step, b_re, b_im, c_re, c_im, nb):
    lam = lax.complex(jnp.minimum(lam_re.astype(F32), -1e-4), lam_im.astype(F32))
    step = jnp.exp(log_step.astype(F32))[:, None]
    lam_bar = jnp.exp(lam * step)
    bmat = lax.complex(b_re.astype(F32), b_im.astype(F32))
    b_bar = ((lam_bar - 1.0) / lam)[:, :, None] * bmat
    gpb = S5_GROUPS // S5_BLOCKS
    eye = jnp.eye(gpb, dtype=F32)

    def in_map(t):
        t = t.reshape(S5_BLOCKS, gpb, S5_STATE, S5_GROUP_CH)
        return jnp.einsum('kgpc,gh->kgchp', t, eye).reshape(S5_BLOCKS, LANES, S5_BLOCK_STATES)

    def out_map(t):
        t = t.reshape(S5_BLOCKS, gpb, S5_GROUP_CH, S5_STATE)
        return jnp.einsum('kgcp,gh->kgphc', t, eye).reshape(S5_BLOCKS, S5_BLOCK_STATES, LANES)

    bblk = jnp.concatenate([in_map(jnp.real(b_bar)), in_map(jnp.imag(b_bar))], axis=2).astype(BF16)
    cblk = jnp.concatenate([out_map(c_re.astype(F32)), -out_map(c_im.astype(F32))], axis=1).astype(BF16)
    a_re = jnp.broadcast_to(jnp.real(lam_bar).reshape(1, S5_NSTATE), (nb, S5_NSTATE)).astype(F32)
    a_im = jnp.broadcast_to(jnp.imag(lam_bar).reshape(1, S5_NSTATE), (nb, S5_NSTATE)).astype(F32)
    return bblk, a_re, a_im, cblk


def _s5_pair_params(fwd, bwd):
    half = fwd[1].shape[0] // 2
    return (jnp.concatenate([fwd[0], bwd[0]], axis=1),
            jnp.concatenate([fwd[1][:half], bwd[1][:half]], axis=0),
            jnp.concatenate([fwd[2][:half], bwd[2][:half]], axis=0),
            jnp.concatenate([fwd[3], bwd[3]], axis=2))


def _s5_scan(u_tm, params, reverse, paired=False, tc=S5_TC):
    s, nb, _ = u_tm.shape
    bdense, a_re, a_im, cdense = params
    nchunks = s // tc
    tmap = (lambda i: (nchunks - 1 - i, 0, 0)) if reverse else (lambda i: (i, 0, 0))
    const = lambda shape: pl.BlockSpec(shape, lambda i: (0,) * len(shape))
    return pl.pallas_call(
        functools.partial(_s5_kernel, reverse=reverse, paired=paired),
        grid=(nchunks,),
        in_specs=[pl.BlockSpec((tc, nb, S5_CH), tmap), const(bdense.shape), const(a_re.shape),
                  const(a_im.shape), const(cdense.shape)],
        out_specs=pl.BlockSpec((tc, nb, S5_CH), tmap),
        out_shape=jax.ShapeDtypeStruct((s, nb, S5_CH), F32),
        scratch_shapes=[pltpu.VMEM((tc, nb, 2 * S5_NSTATE), F32), pltpu.VMEM((nb, 2 * S5_NSTATE), F32)],
        compiler_params=_cparams(("arbitrary",)),
        name="l1_s5_scan_pair" if paired else ("l1_s5_scan_bwd" if reverse else "l1_s5_scan_fwd"),
    )(u_tm, bdense, a_re, a_im, cdense)


def _l1_out_router_kernel(x_ref, mla_ref, yf_ref, yb_ref, u_ref, dskip_ref, gw_ref, gb_ref, wo_ref,
                          fn_ref, rt_ref, x1_out, route_out):
    tm = x_ref.shape[1]
    u = u_ref[...]
    y = yf_ref[...] + yb_ref[...] + dskip_ref[...] * u
    z = jax.nn.gelu(y)
    gate = jax.nn.sigmoid(jnp.dot(z.astype(BF16), gw_ref[...], preferred_element_type=F32) + gb_ref[...])
    ssm = (z * gate).astype(BF16)
    half = wo_ref.shape[0] // 2
    x1 = (x_ref[0]
          + jnp.dot(mla_ref[0], wo_ref[0:half, :], preferred_element_type=F32)
          + jnp.dot(ssm, wo_ref[half:, :], preferred_element_type=F32))
    _store_token_slabs(x1_out.at[0], x1, tm)
    hn = _rms(x1, fn_ref[...])

    h_hi = hn.astype(BF16)
    h_lo = (hn - h_hi.astype(F32)).astype(BF16)
    logits = (jnp.dot(h_hi, rt_ref[0], preferred_element_type=F32)
              + jnp.dot(h_hi, rt_ref[1], preferred_element_type=F32)
              + jnp.dot(h_lo, rt_ref[0], preferred_element_type=F32))
    lane = lax.broadcasted_iota(jnp.int32, logits.shape, 1)
    lg = jnp.where(lane < N_EXPERTS, logits, -jnp.inf)
    m1 = jnp.max(lg, axis=-1, keepdims=True)
    i1 = jnp.min(jnp.where(lg == m1, lane, LANES), axis=-1, keepdims=True)
    lg2 = jnp.where(lane == i1, -jnp.inf, lg)
    m2 = jnp.max(lg2, axis=-1, keepdims=True)
    i2 = jnp.min(jnp.where(lg2 == m2, lane, LANES), axis=-1, keepdims=True)
    e2 = jnp.exp(m2 - m1)
    w1 = 1.0 / (1.0 + e2)
    w2 = e2 / (1.0 + e2)
    route_out[0] = jnp.where(lane == 0, i1.astype(F32),
                             jnp.where(lane == 1, i2.astype(F32),
                                       jnp.where(lane == 2, w1, jnp.where(lane == 3, w2, 0.0))))


def _l1_out_router(x, mla, yf, yb, u_tm, d_skip, glu_w, glu_b, w_out, ffn_norm, router_pad, tm=TM_ROUTER):
    b, s, d = x.shape
    tok = lambda n: pl.BlockSpec((1, tm, n), lambda i, j: (i, j, 0))
    tmaj = pl.BlockSpec((tm, S5_CH), lambda i, j: (j, i))
    const = lambda shape: pl.BlockSpec(shape, lambda i, j: (0,) * len(shape))
    return pl.pallas_call(
        _l1_out_router_kernel,
        grid=(b, s // tm),
        in_specs=[tok(d), tok(MLA_HEADS * MLA_V), tmaj, tmaj, tmaj, const((1, S5_CH)),
                  const(glu_w.shape), const((1, S5_CH)), const(w_out.shape), const((1, d)),
                  const(router_pad.shape)],
        out_specs=[pl.BlockSpec((1, tm * SLAB_ROWS, LANES), lambda i, j: (i, j, 0)), tok(LANES)],
        out_shape=[jax.ShapeDtypeStruct((b, s * SLAB_ROWS, LANES), F32),
                   jax.ShapeDtypeStruct((b, s, LANES), F32)],
        compiler_params=_cparams(("parallel", "parallel")),
        name="l1_out_proj_router",
    )(x, mla, yf, yb, u_tm, d_skip.reshape(1, -1), glu_w, glu_b.reshape(1, -1), w_out,
      ffn_norm.reshape(1, d), router_pad)


def _start_row_gather(idx_ref, src_hbm, dst, sem, n_rows):
    def issue(r, carry):
        src = pl.multiple_of(idx_ref[r] * SLAB_ROWS, SLAB_ROWS)
        pltpu.make_async_copy(src_hbm.at[pl.ds(src, SLAB_ROWS)],
                              dst.at[pl.ds(pl.multiple_of(r * SLAB_ROWS, SLAB_ROWS), SLAB_ROWS)], sem).start()
        return carry

    lax.fori_loop(0, n_rows, issue, 0, unroll=8)


def _wait_row_gather(src_hbm, dst, sem, n_rows):
    pltpu.make_async_copy(src_hbm.at[pl.ds(0, n_rows * SLAB_ROWS)], dst, sem).wait()


def _expert_ffn_kernel(te_ref, nt_ref, idx0_ref, idxn_ref, x_hbm, gw_ref, g_ref, wg_ref, wu_ref, wd_ref,
                       o_ref, xbuf, sem, hn_ref, acc_ref):
    i = pl.program_id(0)
    j = pl.program_id(1)
    last = pl.num_programs(1) - 1
    tm = hn_ref.shape[0]
    n_used = nt_ref[0]
    slot = i % 2

    @pl.when(jnp.logical_and(i == 0, j == 0))
    def _():
        _start_row_gather(idx0_ref, x_hbm, xbuf.at[0], sem.at[0], tm)

    @pl.when(jnp.logical_and(j == 0, i < n_used))
    def _():
        _wait_row_gather(x_hbm, xbuf.at[slot], sem.at[slot], tm)
        x = _load_token_slabs(xbuf.at[slot], tm)
        hn_ref[...] = _rms(x, g_ref[...]).astype(BF16)
        acc_ref[...] = jnp.zeros_like(acc_ref)

    @pl.when(jnp.logical_and(j == 0, i + 1 < n_used))
    def _():
        _start_row_gather(idxn_ref, x_hbm, xbuf.at[1 - slot], sem.at[1 - slot], tm)

    @pl.when(i < n_used)
    def _():
        h = hn_ref[...]
        a = jnp.dot(h, wg_ref[0], preferred_element_type=F32)
        u = jnp.dot(h, wu_ref[0], preferred_element_type=F32)
        act = (a * jax.nn.sigmoid(a) * u).astype(BF16)
        acc_ref[...] += jnp.dot(act, wd_ref[0], preferred_element_type=F32)

        @pl.when(j == last)
        def _():
            _store_token_slabs(o_ref, gw_ref[...] * acc_ref[...], tm)

    @pl.when(jnp.logical_and(i >= n_used, j == last))
    def _():
        o_ref[...] = jnp.zeros_like(o_ref)


def _expert_ffn(x_slabs, src_tok, gate_w, tile_expert, n_tiles_used, ffn_norm, wg, wu, wd, tm, tf=TF_EXPERT):
    a_pad = src_tok.shape[0]
    n_tiles = a_pad // tm
    d = D_MODEL
    f = wg.shape[2]
    grid_spec = pltpu.PrefetchScalarGridSpec(
        num_scalar_prefetch=2,
        grid=(n_tiles, f // tf),
        in_specs=[
            pl.BlockSpec((tm,), lambda i, j, te, nt: (0,), memory_space=pltpu.SMEM),
            pl.BlockSpec((tm,), lambda i, j, te, nt: (jnp.minimum(i + 1, n_tiles - 1),), memory_space=pltpu.SMEM),
            pl.BlockSpec(memory_space=pl.ANY),
            pl.BlockSpec((tm, 1), lambda i, j, te, nt: (i, 0)),
            pl.BlockSpec((1, d), lambda i, j, te, nt: (0, 0)),
            pl.BlockSpec((1, d, tf), lambda i, j, te, nt: (te[i], 0, j)),
            pl.BlockSpec((1, d, tf), lambda i, j, te, nt: (te[i], 0, j)),
            pl.BlockSpec((1, tf, d), lambda i, j, te, nt: (te[i], j, 0)),
        ],
        out_specs=pl.BlockSpec((tm * SLAB_ROWS, LANES), lambda i, j, te, nt: (i, 0)),
        scratch_shapes=[pltpu.VMEM((2, tm * SLAB_ROWS, LANES), F32), pltpu.SemaphoreType.DMA((2,)),
                        pltpu.VMEM((tm, d), BF16), pltpu.VMEM((tm, d), F32)],
    )
    return pl.pallas_call(
        _expert_ffn_kernel,
        grid_spec=grid_spec,
        out_shape=jax.ShapeDtypeStruct((a_pad * SLAB_ROWS, LANES), F32),
        compiler_params=_cparams(("arbitrary", "arbitrary")),
        name="l1_expert_ffn",
    )(tile_expert, n_tiles_used, src_tok, src_tok, x_slabs, gate_w, ffn_norm.reshape(1, d), wg, wu, wd)


def _combine_norm_kernel(ia0_ref, ib0_ref, ian_ref, ibn_ref, x_ref, ys_hbm, g_ref, o_ref, buf, sem):
    tm = o_ref.shape[0]
    i = pl.program_id(0)
    slot = i % 2

    @pl.when(i == 0)
    def _():
        _start_row_gather(ia0_ref, ys_hbm, buf.at[0, 0], sem.at[0, 0], tm)
        _start_row_gather(ib0_ref, ys_hbm, buf.at[0, 1], sem.at[0, 1], tm)

    @pl.when(i + 1 < pl.num_programs(0))
    def _():
        _start_row_gather(ian_ref, ys_hbm, buf.at[1 - slot, 0], sem.at[1 - slot, 0], tm)
        _start_row_gather(ibn_ref, ys_hbm, buf.at[1 - slot, 1], sem.at[1 - slot, 1], tm)

    _wait_row_gather(ys_hbm, buf.at[slot, 0], sem.at[slot, 0], tm)
    _wait_row_gather(ys_hbm, buf.at[slot, 1], sem.at[slot, 1], tm)
    y = _load_token_slabs(x_ref, tm) + (_load_token_slabs(buf.at[slot, 0], tm)
                                        + _load_token_slabs(buf.at[slot, 1], tm))
    o_ref[...] = _rms(y, g_ref[...])


def _combine_norm(x_slabs, ys_slabs, dest, final_norm, tm=TM_COMBINE):
    t = x_slabs.shape[0] // SLAB_ROWS
    nblk = t // tm
    slab = (tm * SLAB_ROWS, LANES)
    return pl.pallas_call(
        _combine_norm_kernel,
        grid=(nblk,),
        in_specs=[pl.BlockSpec((tm,), lambda i: (0,), memory_space=pltpu.SMEM),
                  pl.BlockSpec((tm,), lambda i: (nblk,), memory_space=pltpu.SMEM),
                  pl.BlockSpec((tm,), lambda i: (jnp.minimum(i + 1, nblk - 1),), memory_space=pltpu.SMEM),
                  pl.BlockSpec((tm,), lambda i: (jnp.minimum(i + 1, nblk - 1) + nblk,), memory_space=pltpu.SMEM),
                  pl.BlockSpec(slab, lambda i: (i, 0)),
                  pl.BlockSpec(memory_space=pl.ANY),
                  pl.BlockSpec((1, D_MODEL), lambda i: (0, 0))],
        out_specs=pl.BlockSpec((tm, D_MODEL), lambda i: (i, 0)),
        out_shape=jax.ShapeDtypeStruct((t, D_MODEL), F32),
        scratch_shapes=[pltpu.VMEM((2, 2) + slab, F32), pltpu.SemaphoreType.DMA((2, 2))],
        compiler_params=_cparams(("arbitrary",)),
        name="l1_combine_final_norm",
    )(dest, dest, dest, dest, x_slabs, ys_slabs, final_norm.reshape(1, D_MODEL))


def _moe(x_slabs, route, ffn_norm, wg, wu, wd, final_norm, tm=TM_EXPERT):
    t = x_slabs.shape[0] // SLAB_ROWS
    e_idx = jnp.concatenate([route[:, 0], route[:, 1]]).astype(jnp.int32)
    e_w = jnp.concatenate([route[:, 2], route[:, 3]])
    n_assign = TOP_K * t
    order = jnp.argsort(e_idx, stable=True).astype(jnp.int32)
    inv = jnp.argsort(order).astype(jnp.int32)
    counts = jnp.sum(e_idx[:, None] == jnp.arange(N_EXPERTS, dtype=jnp.int32)[None, :], axis=0).astype(jnp.int32)
    starts = jnp.cumsum(counts) - counts
    padded = ((counts + tm - 1) // tm) * tm
    pad_ends = jnp.cumsum(padded)
    pad_starts = pad_ends - padded
    a_pad = n_assign + N_EXPERTS * tm
    n_tiles = a_pad // tm

    tile_start = jnp.arange(n_tiles, dtype=jnp.int32) * tm
    tile_expert = jnp.minimum(jnp.sum(tile_start[:, None] >= pad_ends[None, :], axis=1), N_EXPERTS - 1).astype(jnp.int32)
    n_tiles_used = (pad_ends[-1] // tm).astype(jnp.int32).reshape(1)

    slot = jnp.arange(a_pad, dtype=jnp.int32)
    slot_e = jnp.repeat(tile_expert, tm)
    within = slot - pad_starts[slot_e]
    valid = within < counts[slot_e]
    src = order[jnp.clip(starts[slot_e] + within, 0, n_assign - 1)]
    src_tok = jnp.where(valid, src % t, 0).astype(jnp.int32)
    gate_w = jnp.where(valid, e_w[src], 0.0).astype(F32).reshape(a_pad, 1)

    ys = _expert_ffn(x_slabs, src_tok, gate_w, tile_expert, n_tiles_used, ffn_norm, wg, wu, wd, tm)
    dest = (pad_starts[e_idx] + inv - starts[e_idx]).astype(jnp.int32)
    return _combine_norm(x_slabs, ys, dest, final_norm)


def _l0_in_weight(w_in):
    hq = RET_HEADS * RET_DK
    hv = RET_HEADS * RET_DV
    hd = DIL_HEADS * DIL_DH
    rq, rk, rv, rg, dq, dk, dv = jnp.split(
        w_in, [hq, 2 * hq, 2 * hq + hv, 2 * hq + 2 * hv, 2 * hq + 2 * hv + hd, 2 * hq + 2 * hv + 2 * hd], axis=1)
    dup = lambda t: jnp.concatenate([t.reshape(-1, RET_HEADS, 1, RET_DK)] * 2, axis=2).reshape(-1, 2 * hq)
    w = jnp.concatenate([dup(rq) * RET_DK ** -0.5, dup(rk), rv, rg, dq * DIL_DH ** -0.5, dk, dv], axis=1)
    assert w.shape[1] == L0_COLS
    return w.astype(BF16)


def _prepare(l0_w_in, l0_ret_decay_f, l0_ret_decay_b, l0_w_out, l0_ffn_w_gate, l0_ffn_w_up, l0_ffn_w_down,
             l1_w_in, l1_mla_w_uq, l1_mla_w_ukv, l1_s5_glu_w, l1_w_out, l1_router,
             l1_exp_w_gate, l1_exp_w_up, l1_exp_w_down):
    w1, wq2, wkv2 = _l1_weights(l1_w_in, l1_mla_w_uq, l1_mla_w_ukv)
    router_f32 = jnp.zeros((D_MODEL, LANES), F32).at[:, :N_EXPERTS].set(l1_router.astype(F32))
    router_hi = router_f32.astype(BF16)
    router_pad = jnp.stack([router_hi, (router_f32 - router_hi.astype(F32)).astype(BF16)])
    return dict(
        l0_w_in=_l0_in_weight(l0_w_in),
        ret_tables=_retention_tables(l0_ret_decay_f, l0_ret_decay_b),
        l0_w_out=l0_w_out.astype(BF16),
        l0_wg=l0_ffn_w_gate.astype(BF16), l0_wu=l0_ffn_w_up.astype(BF16), l0_wd=l0_ffn_w_down.astype(BF16),
        l1_w1=w1, l1_wq2=wq2, l1_wkv2=wkv2,
        glu_w=l1_s5_glu_w.astype(BF16), l1_w_out=l1_w_out.astype(BF16), router_pad=router_pad,
        exp_wg=l1_exp_w_gate.astype(BF16), exp_wu=l1_exp_w_up.astype(BF16), exp_wd=l1_exp_w_down.astype(BF16),
    )


def _trunk(x, prep, p):
    b, s, d = x.shape
    t = b * s
    p0, *regrouped = _l0_in_proj(x.reshape(t, d), p['l0_attn_norm'], prep['l0_w_in'])
    p0 = p0.reshape(b, s, L0_COLS)
    ret = _retention(p0, prep['ret_tables'], p['l0_ret_gn'])
    dil = _dilated_attention(p0, regrouped)
    x = _l0_out_ffn(x.reshape(t, d), ret.reshape(t, -1), dil.reshape(t, -1), prep['l0_w_out'],
                    p['l0_ffn_norm'], prep['l0_wg'], prep['l0_wu'], prep['l0_wd']).reshape(b, s, d)
    nb = -(-b // SUBLANES) * SUBLANES
    ct, st = _rope_lane_tables(s)
    q, k, v, u_tm = _l1_proj(x, p['l1_attn_norm'], prep['l1_w1'], p['l1_mla_q_norm'], prep['l1_wq2'],
                             p['l1_mla_kv_norm'], prep['l1_wkv2'], ct, st, b)
    mla = _mla_attention(q, k, v)
    u3 = u_tm.reshape(s, b, S5_CH)
    sf = _s5_direction_params(p['l1_s5_lam_re_f'], p['l1_s5_lam_im_f'], p['l1_s5_log_step_f'],
                              p['l1_s5_b_re'], p['l1_s5_b_im'], p['l1_s5_c_re'], p['l1_s5_c_im'], nb)
    sb = _s5_direction_params(p['l1_s5_lam_re_b'], p['l1_s5_lam_im_b'], p['l1_s5_log_step_b'],
                              p['l1_s5_b_re'], p['l1_s5_b_im'], p['l1_s5_c_re'], p['l1_s5_c_im'], nb)
    if 2 * b == SUBLANES:
        y = _s5_scan(jnp.concatenate([u3, jnp.flip(u3, axis=0)], axis=1), _s5_pair_params(sf, sb),
                     reverse=False, paired=True)
        yf = y[:, :b].reshape(s, b * S5_CH)
        yb = jnp.flip(y[:, b:], axis=0).reshape(s, b * S5_CH)
    else:
        if nb != b:
            u3 = jnp.pad(u3, ((0, 0), (0, nb - b), (0, 0)))
        yf = _s5_scan(u3, sf, reverse=False).reshape(s, nb * S5_CH)
        yb = _s5_scan(u3, sb, reverse=True).reshape(s, nb * S5_CH)
    x1s, route = _l1_out_router(x, mla, yf, yb, u_tm, p['l1_s5_d'], prep['glu_w'], p['l1_s5_glu_b'],
                                prep['l1_w_out'], p['l1_ffn_norm'], prep['router_pad'])
    out = _moe(x1s.reshape(t * SLAB_ROWS, LANES), route.reshape(t, LANES), p['l1_ffn_norm'],
               prep['exp_wg'], prep['exp_wu'], prep['exp_wd'], p['final_norm'])
    return out.reshape(b, s, d)


def kernel(x_prompt, x_sample, l0_attn_norm, l0_w_in, l0_ret_decay_f, l0_ret_decay_b, l0_ret_gn, l0_w_out,
           l0_ffn_norm, l0_ffn_w_gate, l0_ffn_w_up, l0_ffn_w_down, l1_attn_norm, l1_w_in, l1_mla_q_norm,
           l1_mla_w_uq, l1_mla_kv_norm, l1_mla_w_ukv, l1_s5_lam_re_f, l1_s5_lam_im_f, l1_s5_log_step_f,
           l1_s5_lam_re_b, l1_s5_lam_im_b, l1_s5_log_step_b, l1_s5_b_re, l1_s5_b_im, l1_s5_c_re, l1_s5_c_im,
           l1_s5_d, l1_s5_glu_w, l1_s5_glu_b, l1_w_out, l1_ffn_norm, l1_router, l1_exp_w_gate, l1_exp_w_up,
           l1_exp_w_down, final_norm):
    p = dict(locals())
    prep = _prepare(l0_w_in, l0_ret_decay_f, l0_ret_decay_b, l0_w_out, l0_ffn_w_gate, l0_ffn_w_up,
                    l0_ffn_w_down, l1_w_in, l1_mla_w_uq, l1_mla_w_ukv, l1_s5_glu_w, l1_w_out, l1_router,
                    l1_exp_w_gate, l1_exp_w_up, l1_exp_w_down)
    return (_trunk(x_prompt, prep, p), _trunk(x_sample, prep, p))
```
